```python
import jax
import jax.numpy as jnp
from jax import lax
import numpy as np

D_MODEL = 1024
BATCH = 8
SEQ = 4096
DEPTH = 1

N_META = 16
RWKV_HEADS = 8
HEAD_SIZE = 64
RWKV_WIDTH = RWKV_HEADS * HEAD_SIZE
CONV_GROUPS = 8
CONV_WIDTH = D_MODEL // 2
CONV_K = 3
DECAY_LORA = 64
AAA_LORA = 64
GATE_LORA = 128
N_EXPERTS = 32
TOP_K = 4
D_FF = D_MODEL
SWIGLU_ALPHA = 1.702
SWIGLU_LIMIT = 7.0
MOE_BLOCK = 128
NORM_EPS = 1e-5
LNX_EPS = 64e-5
IN_COLS = 3 * RWKV_WIDTH + 3 * CONV_WIDTH + 2 * D_MODEL

kernel_name = "hybrid_rwkv7_shortconv_moe_block"


def _rmsnorm(x, g):
    xf = x.astype(jnp.float32)
    y = xf * lax.rsqrt(jnp.mean(xf * xf, axis=-1, keepdims=True) + NORM_EPS)
    return (y * g.astype(jnp.float32)).astype(x.dtype)


def _token_shift(z):
    return jnp.pad(z[:, :-1], ((0, 0), (1, 0), (0, 0)))


def _wkv7_scan(r, decay, k, v, kk, a):
    b, _, h, n = r.shape

    def step(S, inp):
        r_t, w_t, k_t, v_t, kk_t, a_t = inp
        sa = jnp.einsum('bhij,bhj->bhi', S, -kk_t)
        S = (S * w_t[:, :, None, :]
             + sa[..., None] * (kk_t * a_t)[:, :, None, :]
             + v_t[..., None] * k_t[:, :, None, :])
        y_t = jnp.einsum('bhij,bhj->bhi', S, r_t)
        return S, y_t

    s0 = jnp.zeros((b, h, n, n), jnp.float32)
    seq_major = tuple(jnp.moveaxis(z, 1, 0) for z in (r, decay, k, v, kk, a))
    _, y = lax.scan(step, s0, seq_major)
    return jnp.moveaxis(y, 0, 1)


def _rwkv7_branch(x_n, pr, pk, pv, mu_r, mu_k, mu_v, mu_w, mu_a, mu_g, w0, w_w1, w_w2,
                  a0, w_a1, w_a2, w_g1, w_g2, k_k, k_a, r_k, lnx_g, lnx_b, w_o_rwkv):
    b, L, _ = x_n.shape
    dt = x_n.dtype
    r = pr + (_token_shift(pr) - pr) * mu_r
    k = pk + (_token_shift(pk) - pk) * mu_k
    v = pv + (_token_shift(pv) - pv) * mu_v
    dx = _token_shift(x_n) - x_n
    xw = x_n + dx * mu_w
    xa = x_n + dx * mu_a
    xg = x_n + dx * mu_g
    w_log = -jax.nn.softplus(-(w0 + jnp.tanh(xw @ w_w1) @ w_w2)) - 0.5
    a = jax.nn.sigmoid(a0 + (xa @ w_a1) @ w_a2)
    g = jax.nn.sigmoid(xg @ w_g1) @ w_g2

    heads = lambda z: z.astype(jnp.float32).reshape(b, L, RWKV_HEADS, HEAD_SIZE)
    kk = heads(k * k_k)
    kk = kk / jnp.maximum(jnp.sqrt(jnp.sum(kk * kk, axis=-1, keepdims=True)), 1e-12)
    k = k * (1.0 + (a - 1.0) * k_a)
    r_h, k_h, v_h, a_h = heads(r), heads(k), heads(v), heads(a)
    decay = jnp.exp(-jnp.exp(heads(w_log)))

    y = _wkv7_scan(r_h, decay, k_h, v_h, kk, a_h)
    mean = jnp.mean(y, axis=-1, keepdims=True)
    var = jnp.mean((y - mean) ** 2, axis=-1, keepdims=True)
    yn = ((y - mean) * lax.rsqrt(var + LNX_EPS)).reshape(b, L, RWKV_WIDTH)
    yn = yn * lnx_g.astype(jnp.float32) + lnx_b.astype(jnp.float32)
    rk = r_k.astype(jnp.float32).reshape(RWKV_HEADS, HEAD_SIZE)
    bonus = (jnp.sum(r_h * k_h * rk, axis=-1, keepdims=True) * v_h).reshape(b, L, RWKV_WIDTH)
    return ((yn + bonus).astype(dt) * g) @ w_o_rwkv


def _short_conv_branch(cb, cc, ch, conv_w, w_o_conv):
    L = ch.shape[1]
    u = cc * ch
    upad = jnp.pad(u, ((0, 0), (CONV_K - 1, 0), (0, 0)))
    conv = sum(conv_w[j] * upad[:, j:j + L] for j in range(CONV_K))
    return (cb * conv) @ w_o_conv


def _moe(h, w_router, b_router, w_e_gate, b_e_gate, w_e_up, b_e_up, w_e_down, b_e_down):
    b, L, d = h.shape
    dt = h.dtype
    T = b * L
    xf = h.reshape(T, d)
    logits = (xf @ w_router + b_router).astype(jnp.float32)
    top_v, top_i = lax.top_k(logits, TOP_K)
    gates = jax.nn.softmax(top_v, axis=-1).astype(dt)

    n_assign = T * TOP_K
    flat_e = top_i.reshape(-1)
    flat_w = gates.reshape(-1)
    flat_tok = jnp.repeat(jnp.arange(T, dtype=jnp.int32), TOP_K)
    order = jnp.argsort(flat_e)
    sorted_e = flat_e[order]
    counts = jnp.bincount(flat_e, length=N_EXPERTS).astype(jnp.int32)
    padded = ((counts + MOE_BLOCK - 1) // MOE_BLOCK) * MOE_BLOCK
    start = jnp.cumsum(counts) - counts
    pend = jnp.cumsum(padded)
    pstart = pend - padded
    rank = jnp.arange(n_assign, dtype=jnp.int32) - start[sorted_e]
    dest = pstart[sorted_e] + rank
    P = -(-n_assign // MOE_BLOCK) * MOE_BLOCK + N_EXPERTS * MOE_BLOCK
    n_blk = P // MOE_BLOCK
    row_tok = jnp.zeros((P,), jnp.int32).at[dest].set(flat_tok[order])
    row_w = jnp.zeros((P,), dt).at[dest].set(flat_w[order])
    blk_start = jnp.arange(n_blk, dtype=jnp.int32) * MOE_BLOCK
    blk_e = jnp.minimum(jnp.searchsorted(pend, blk_start, side='right'), N_EXPERTS - 1)

    def expert_block(args):
        e, toks, wts = args
        xb = xf[toks]
        gt = jnp.minimum(xb @ w_e_gate[e] + b_e_gate[e], SWIGLU_LIMIT)
        up = jnp.clip(xb @ w_e_up[e] + b_e_up[e], -SWIGLU_LIMIT, SWIGLU_LIMIT)
        act = (up + 1.0) * (gt * jax.nn.sigmoid(gt * SWIGLU_ALPHA))
        return (act @ w_e_down[e] + b_e_down[e]) * wts[:, None]

    ys = lax.map(expert_block, (blk_e, row_tok.reshape(n_blk, MOE_BLOCK),
                                row_w.reshape(n_blk, MOE_BLOCK)))
    out = jnp.zeros((T, d), dt).at[row_tok].add(ys.reshape(P, d))
    return out.reshape(b, L, d)


def setup_inputs(seed: int = 0) -> dict:
    key = jax.random.key(seed)
    ks = iter(jax.random.split(key, 48))
    f32 = jnp.float32
    R, C, D = RWKV_WIDTH, CONV_WIDTH, D_MODEL

    def nrm(shape, scale):
        return scale * jax.random.normal(next(ks), shape, f32)

    def uni(shape):
        return jax.random.uniform(next(ks), shape, f32, 0.2, 0.8)

    n = jnp.arange(R, dtype=f32) / (R - 1)
    w0_base = (-6.5 + 5.0 * n ** 0.85)[None, :]
    return {
        "x": nrm((BATCH, SEQ, D), 1.0),
        "meta_tokens": nrm((N_META, D), 1.0),
        "ln1_g": 1.0 + nrm((DEPTH, D), 0.01),
        "w_in": nrm((DEPTH, D, IN_COLS), D ** -0.5),
        "mu_r": uni((DEPTH, R)),
        "mu_k": uni((DEPTH, R)),
        "mu_v": uni((DEPTH, R)),
        "mu_w": uni((DEPTH, D)),
        "mu_a": uni((DEPTH, D)),
        "mu_g": uni((DEPTH, D)),
        "w0": w0_base + nrm((DEPTH, R), 0.1),
        "w_w1": nrm((DEPTH, D, DECAY_LORA), D ** -0.5),
        "w_w2": nrm((DEPTH, DECAY_LORA, R), 0.1 * DECAY_LORA ** -0.5),
        "a0": nrm((DEPTH, R), 0.1),
        "w_a1": nrm((DEPTH, D, AAA_LORA), D ** -0.5),
        "w_a2": nrm((DEPTH, AAA_LORA, R), AAA_LORA ** -0.5),
        "w_g1": nrm((DEPTH, D, GATE_LORA), D ** -0.5),
        "w_g2": nrm((DEPTH, GATE_LORA, R), GATE_LORA ** -0.5),
        "k_k": 0.85 + nrm((DEPTH, R), 0.05),
        "k_a": 1.0 + nrm((DEPTH, R), 0.05),
        "r_k": nrm((DEPTH, R), 0.1),
        "lnx_g": 1.0 + nrm((DEPTH, R), 0.01),
        "lnx_b": nrm((DEPTH, R), 0.01),
        "w_o_rwkv": nrm((DEPTH, R, D), R ** -0.5),
        "conv_w": nrm((DEPTH, CONV_K, C), CONV_K ** -0.5),
        "w_o_conv": nrm((DEPTH, C, D), C ** -0.5),
        "w_o": nrm((DEPTH, D, D), D ** -0.5),
        "ln2_g": 1.0 + nrm((DEPTH, D), 0.01),
        "w_router": nrm((DEPTH, D, N_EXPERTS), D ** -0.5),
        "b_router": nrm((DEPTH, N_EXPERTS), 0.01),
        "w_e_gate": nrm((DEPTH, N_EXPERTS, D, D_FF), D ** -0.5),
        "b_e_gate": nrm((DEPTH, N_EXPERTS, D_FF), 0.01),
        "w_e_up": nrm((DEPTH, N_EXPERTS, D, D_FF), D ** -0.5),
        "b_e_up": nrm((DEPTH, N_EXPERTS, D_FF), 0.01),
        "w_e_down": nrm((DEPTH, N_EXPERTS, D_FF, D), D_FF ** -0.5),
        "b_e_down": nrm((DEPTH, N_EXPERTS, D), 0.01),
        "lnf_g": 1.0 + nrm((D,), 0.01),
    }


def reference(x, meta_tokens, ln1_g, w_in, mu_r, mu_k, mu_v, mu_w, mu_a, mu_g, w0, w_w1, w_w2,
              a0, w_a1, w_a2, w_g1, w_g2, k_k, k_a, r_k, lnx_g, lnx_b, w_o_rwkv, conv_w,
              w_o_conv, w_o, ln2_g, w_router, b_router, w_e_gate, b_e_gate, w_e_up, b_e_up,
              w_e_down, b_e_down, lnf_g):
    b = x.shape[0]
    R, C, D = RWKV_WIDTH, CONV_WIDTH, D_MODEL
    meta = jnp.broadcast_to(meta_tokens[None].astype(x.dtype), (b, N_META, D))
    h = jnp.concatenate([meta, x], axis=1)
    cuts = [R, 2 * R, 3 * R, 3 * R + C, 3 * R + 2 * C, 3 * R + 3 * C, 3 * R + 3 * C + D]
    for l in range(DEPTH):
        x_n = _rmsnorm(h, ln1_g[l])
        proj = x_n @ w_in[l]
        pr, pk, pv, cb, cc, ch, gate_a, gate_b = jnp.split(proj, cuts, axis=-1)
        y_a = _rwkv7_branch(x_n, pr, pk, pv, mu_r[l], mu_k[l], mu_v[l], mu_w[l], mu_a[l], mu_g[l],
                            w0[l], w_w1[l], w_w2[l], a0[l], w_a1[l], w_a2[l], w_g1[l], w_g2[l],
                            k_k[l], k_a[l], r_k[l], lnx_g[l], lnx_b[l], w_o_rwkv[l])
        y_b = _short_conv_branch(cb, cc, ch, conv_w[l], w_o_conv[l])
        merged = jax.nn.sigmoid(gate_a) * y_a + jax.nn.sigmoid(gate_b) * y_b
        h = h + merged @ w_o[l]
        h = h + _moe(_rmsnorm(h, ln2_g[l]), w_router[l], b_router[l], w_e_gate[l], b_e_gate[l],
                     w_e_up[l], b_e_up[l], w_e_down[l], b_e_down[l])
    h = _rmsnorm(h, lnf_g)
    return h[:, N_META:]
```

```python
import functools

import jax
import jax.numpy as jnp
from jax import lax
from jax.experimental import pallas as pl
from jax.experimental.pallas import tpu as pltpu

D = 1024
R = 512
NH = 8
HS = 64
NE = 32
TOPK = 4
NMETA = 16
CHUNK = 64
EBLK = 256
NORM_EPS = 1e-5
LNX_EPS = 64e-5
ALPHA = 1.702
LIMIT = 7.0
LORA_W, LORA_A, LORA_G = 64, 64, 128
LORA = LORA_W + LORA_A + LORA_G
NPROJ = 3 * R + 3 * R + 2 * D + 2 * LORA
VMEM_LIMIT = 56 * 1024 * 1024

f32 = jnp.float32
bf16 = jnp.bfloat16


def _bdot(a, b):
    return jnp.dot(a.astype(bf16), b.astype(bf16), preferred_element_type=f32)


def _bdot_nt(a, b):
    return lax.dot_general(a.astype(bf16), b.astype(bf16), (((1,), (1,)), ((), ())),
                           preferred_element_type=f32)


def _bdot_tn(a, b):
    return lax.dot_general(a.astype(bf16), b.astype(bf16), (((0,), (0,)), ((), ())),
                           preferred_element_type=f32)


def _split3(x):
    h = x.astype(bf16)
    r1 = x - h.astype(f32)
    m = r1.astype(bf16)
    l = (r1 - m.astype(f32)).astype(bf16)
    return h, m, l


def _head_sum(x, seg):
    h, m, l = _split3(x)
    dot = functools.partial(jnp.dot, preferred_element_type=f32)
    return dot(h, seg) + dot(m, seg) + dot(l, seg)


def _cparams(sem):
    return pltpu.CompilerParams(dimension_semantics=sem, vmem_limit_bytes=VMEM_LIMIT)


def _inproj_kernel(x_ref, g_ref, w_ref, rkv_ref, cv_ref, gt_ref, lo_ref):
    x = x_ref[...]
    xn = x * lax.rsqrt(jnp.mean(x * x, axis=-1, keepdims=True) + NORM_EPS) * g_ref[...]
    xb = xn.astype(bf16)
    col = 0
    for ref in (rkv_ref, cv_ref, gt_ref, lo_ref):
        width = ref.shape[-1]
        for c in range(0, width, 512):
            ref[:, c:c + 512] = jnp.dot(
                xb, w_ref[:, col + c:col + c + 512], preferred_element_type=f32
            ).astype(ref.dtype)
        col += width


def _inproj(x2d, g, w, tm):
    t = x2d.shape[0]
    widths = (3 * R, 3 * R, 2 * D, 2 * LORA)
    return pl.pallas_call(
        _inproj_kernel,
        name="inproj",
        grid=(t // tm,),
        in_specs=[pl.BlockSpec((tm, D), lambda i: (i, 0)),
                  pl.BlockSpec((1, D), lambda i: (0, 0)),
                  pl.BlockSpec((D, NPROJ), lambda i: (0, 0))],
        out_specs=[pl.BlockSpec((tm, n), lambda i: (i, 0)) for n in widths],
        out_shape=[jax.ShapeDtypeStruct((t, n), bf16) for n in widths],
        compiler_params=_cparams(("parallel",)),
    )(x2d, g, w)


HALO = 16


def _mix_kernel(tiles_per_seq, rkv_ref, cv_ref, lo_ref, rkv_h_ref, cv_h_ref, lo_h_ref,
                rkv_i_ref, cv_i_ref, lo_i_ref, vec_ref, ww2_ref, wa2_ref, wg2_ref, cw_ref,
                seg_ref, r_out, k_out, v_out, kk_out, b_out, lw_out, g_out, cb_out):
    first = (pl.program_id(0) % tiles_per_seq) == 0
    tm = rkv_ref.shape[0]
    row = lax.broadcasted_iota(jnp.int32, (tm, 1), 0)

    def history(h_ref, i_ref):
        return jnp.where(first, i_ref[...].astype(f32), h_ref[...].astype(f32))

    def shifted(cur, hist, n):
        out = pltpu.roll(cur, n, axis=0)
        for j in range(n):
            out = jnp.where(row == j, hist[HALO - n + j:HALO - n + j + 1, :], out)
        return out

    vec = vec_ref[...]
    mu_r, mu_k, mu_v = vec[0:1, :], vec[1:2, :], vec[2:3, :]
    w0, a0, k_k, k_a = vec[3:4, :], vec[4:5, :], vec[5:6, :], vec[6:7, :]

    rkv = rkv_ref[...].astype(f32)
    rkv_prev = shifted(rkv, history(rkv_h_ref, rkv_i_ref), 1)
    pr, pk, pv = rkv[:, :R], rkv[:, R:2 * R], rkv[:, 2 * R:]
    r = pr + (rkv_prev[:, :R] - pr) * mu_r
    k = pk + (rkv_prev[:, R:2 * R] - pk) * mu_k
    v = pv + (rkv_prev[:, 2 * R:] - pv) * mu_v

    lo = lo_ref[...].astype(f32)
    lo_prev = shifted(lo, history(lo_h_ref, lo_i_ref), 1)
    mixed = lo[:, :LORA] + lo_prev[:, LORA:]
    hw = jnp.tanh(mixed[:, :LORA_W])
    ha = mixed[:, LORA_W:LORA_W + LORA_A]
    hg = jax.nn.sigmoid(mixed[:, LORA_W + LORA_A:])
    w_log = -jax.nn.softplus(-(w0 + _bdot(hw, ww2_ref[...]))) - 0.5
    a = jax.nn.sigmoid(a0 + _bdot(ha, wa2_ref[...]))
    g = _bdot(hg, wg2_ref[...])

    kk = k * k_k
    norm = jnp.sqrt(_head_sum(kk * kk, seg_ref[...]))
    kk = kk / jnp.maximum(norm, 1e-12)
    k = k * (1.0 + (a - 1.0) * k_a)

    r_out[...] = r
    k_out[...] = k
    v_out[...] = v
    kk_out[...] = kk
    b_out[...] = kk * a
    lw_out[...] = -jnp.exp(w_log)
    g_out[...] = g

    cv = cv_ref[...].astype(f32)
    hist = history(cv_h_ref, cv_i_ref)
    u = cv[:, R:2 * R] * cv[:, 2 * R:]
    uh = hist[:, R:2 * R] * hist[:, 2 * R:]
    cw = cw_ref[...]
    conv = cw[2:3, :] * u + cw[1:2, :] * shifted(u, uh, 1) + cw[0:1, :] * shifted(u, uh, 2)
    cb_out[...] = cv[:, :R] * conv


def _mix(rkv, cv, lo, init, vec, ww2, wa2, wg2, cw, seg, seq_len, tm):
    t = rkv.shape[0]
    tiles_per_seq = seq_len // tm
    hb = tm // HALO

    def cur(n):
        return pl.BlockSpec((tm, n), lambda i: (i, 0))

    def halo(n):
        return pl.BlockSpec((HALO, n), lambda i: (jnp.maximum(i * hb - 1, 0), 0))

    def whole(shape):
        return pl.BlockSpec(shape, lambda i: (0,) * len(shape))

    out_spec = pl.BlockSpec((tm, R), lambda i: (i, 0))
    return pl.pallas_call(
        functools.partial(_mix_kernel, tiles_per_seq),
        name="mix",
        grid=(t // tm,),
        in_specs=[cur(3 * R), cur(3 * R), cur(2 * LORA),
                  halo(3 * R), halo(3 * R), halo(2 * LORA),
                  whole((HALO, 3 * R)), whole((HALO, 3 * R)), whole((HALO, 2 * LORA)),
                  whole((8, R)), whole((LORA_W, R)), whole((LORA_A, R)), whole((LORA_G, R)),
                  whole((8, R)), whole((R, R))],
        out_specs=[out_spec] * 8,
        out_shape=[jax.ShapeDtypeStruct((t, R), f32)] * 8,
        compiler_params=_cparams(("parallel",)),
    )(rkv, cv, lo, rkv, cv, lo, *init, vec, ww2, wa2, wg2, cw, seg)


def _wkv_kernel(r_ref, k_ref, v_ref, kk_ref, b_ref, lw_ref, s0_ref, y_ref, sT_ref, s_scr):
    c = pl.program_id(1)

    @pl.when(c == 0)
    def _():
        s_scr[...] = s0_ref[...]

    C = CHUNK
    ri = lax.broadcasted_iota(jnp.int32, (C, C), 0)
    ci = lax.broadcasted_iota(jnp.int32, (C, C), 1)
    incl = ri >= ci
    strict = ri > ci
    tri = jnp.where(incl, 1.0, 0.0).astype(bf16)
    eye = jnp.where(ri == ci, 1.0, 0.0).astype(f32)

    lw = lw_ref[...]
    h, m, l = _split3(lw)
    dot = functools.partial(jnp.dot, preferred_element_type=f32)
    cum = dot(tri, h) + dot(tri, m) + dot(tri, l)
    tot = cum[C - 1:C, :]
    r = r_ref[...]
    k = k_ref[...]
    v = v_ref[...]
    b = b_ref[...]
    g_inv = jnp.exp(-cum)
    rt = r * jnp.exp(cum)
    kt = k * g_inv
    bt = b * g_inv
    at = -kk_ref[...] * jnp.exp(cum - lw)
    tail = jnp.exp(tot - cum)
    bh = b * tail
    kh = k * tail
    g_tot = jnp.exp(tot)

    ys = []
    for hd in range(NH):
        sl = slice(hd * HS, (hd + 1) * HS)
        lhs = jnp.concatenate([at[:, sl], rt[:, sl]], axis=0).astype(bf16)
        rhs = jnp.concatenate([bt[:, sl], kt[:, sl]], axis=0).astype(bf16)
        amat = _bdot_nt(lhs, rhs)
        a_ab = jnp.where(strict, amat[:C, :C], 0.0)
        a_ak = jnp.where(strict, amat[:C, C:], 0.0)
        a_rb = jnp.where(incl, amat[C:, :C], 0.0)
        a_rk = jnp.where(incl, amat[C:, C:], 0.0)
        x = eye + a_ab
        p = a_ab
        for _ in range(5):
            pb = p.astype(bf16)
            p = dot(pb, pb)
            x = x + _bdot(x, p)
        s = s_scr[hd]
        sh = _bdot_nt(lhs, s)
        vh = v[:, sl]
        u = _bdot(x, sh[:C] + _bdot(a_ak, vh))
        uv = jnp.concatenate([u, vh], axis=0).astype(bf16)
        y = sh[C:] + _bdot(jnp.concatenate([a_rb, a_rk], axis=1), uv)
        bk = jnp.concatenate([bh[:, sl], kh[:, sl]], axis=0)
        s_scr[hd] = s * g_tot[:, sl] + _bdot_tn(uv, bk)
        ys.append(y)
    y_ref[...] = jnp.concatenate(ys, axis=1)

    @pl.when(c == pl.num_programs(1) - 1)
    def _():
        sT_ref[...] = s_scr[...]


def _wkv(r, k, v, kk, b, lw, s0):
    nb, seq, _ = r.shape
    blk = pl.BlockSpec((None, CHUNK, R), lambda i, c: (i, c, 0))
    return pl.pallas_call(
        _wkv_kernel,
        name="wkv",
        grid=(nb, seq // CHUNK),
        in_specs=[blk] * 6 + [pl.BlockSpec((NH, HS, HS), lambda i, c: (0, 0, 0))],
        out_specs=[blk, pl.BlockSpec((None, NH, HS, HS), lambda i, c: (i, 0, 0, 0))],
        out_shape=[jax.ShapeDtypeStruct((nb, seq, R), f32),
                   jax.ShapeDtypeStruct((nb, NH, HS, HS), f32)],
        scratch_shapes=[pltpu.VMEM((NH, HS, HS), f32)],
        compiler_params=_cparams(("parallel", "arbitrary")),
    )(r, k, v, kk, b, lw, s0)


def _merge_kernel(y_ref, r_ref, k_ref, v_ref, g_ref, cb_ref, gt_ref, x_ref, vec_ref, ln2_ref,
                  seg_ref, worw_ref, woc_ref, wo_ref, wr_ref, br_ref,
                  h_out, x2_out, lg_out):
    seg = seg_ref[...]
    vec = vec_ref[...]
    lnx_g, lnx_b, r_k = vec[0:1, :], vec[1:2, :], vec[2:3, :]
    y = y_ref[...]
    mean = _head_sum(y, seg) * (1.0 / HS)
    yc = y - mean
    var = _head_sum(yc * yc, seg) * (1.0 / HS)
    yn = yc * lax.rsqrt(var + LNX_EPS) * lnx_g + lnx_b
    v = v_ref[...]
    bonus = _head_sum(r_ref[...] * k_ref[...] * r_k, seg) * v
    y_a = _bdot((yn + bonus) * g_ref[...], worw_ref[...])
    y_b = _bdot(cb_ref[...], woc_ref[...])
    gt = gt_ref[...].astype(f32)
    merged = jax.nn.sigmoid(gt[:, :D]) * y_a + jax.nn.sigmoid(gt[:, D:]) * y_b
    hres = x_ref[...] + _bdot(merged, wo_ref[...])
    h_out[...] = hres
    x2 = hres * lax.rsqrt(jnp.mean(hres * hres, axis=-1, keepdims=True) + NORM_EPS) * ln2_ref[...]
    x2_out[...] = x2
    xh, xm, _ = _split3(x2)
    wh, wm, _ = _split3(wr_ref[...])
    nt = functools.partial(lax.dot_general, dimension_numbers=(((1,), (1,)), ((), ())),
                           preferred_element_type=f32)
    lg_out[...] = nt(wh, xh) + nt(wh, xm) + nt(wm, xh) + br_ref[...]


def _merge(y, r, k, v, g, cb, gt, x2d, vec, ln2, seg, worw, woc, wo, wr_t, br, tm):
    t = y.shape[0]

    def rows(n):
        return pl.BlockSpec((tm, n), lambda i: (i, 0))

    def whole(shape):
        return pl.BlockSpec(shape, lambda i: (0,) * len(shape))

    return pl.pallas_call(
        _merge_kernel,
        name="merge",
        grid=(t // tm,),
        in_specs=[rows(R)] * 6 + [rows(2 * D), rows(D), whole((8, R)), whole((1, D)),
                                   whole((R, R)), whole((R, D)), whole((R, D)), whole((D, D)),
                                   whole((NE, D)), whole((NE, 1))],
        out_specs=[rows(D), rows(D), pl.BlockSpec((NE, tm), lambda i: (0, i))],
        out_shape=[jax.ShapeDtypeStruct((t, D), f32), jax.ShapeDtypeStruct((t, D), f32),
                   jax.ShapeDtypeStruct((NE, t), f32)],
        compiler_params=_cparams(("parallel",)),
    )(y, r, k, v, g, cb, gt, x2d, vec, ln2, seg, worw, woc, wo, wr_t, br)


def _route_kernel(lg_ref, e_out, w_out, rank_out, cnt_out, cnt_scr):
    i = pl.program_id(0)

    @pl.when(i == 0)
    def _():
        cnt_scr[...] = jnp.zeros_like(cnt_scr)

    lg = lg_ref[...]
    tr = lg.shape[1]
    erow = lax.broadcasted_iota(jnp.int32, lg.shape, 0)
    work = lg
    hits, vals, idxs = [], [], []
    for _ in range(TOPK):
        m = jnp.max(work, axis=0, keepdims=True)
        idx = jnp.min(jnp.where(work == m, erow, NE), axis=0, keepdims=True)
        hit = erow == idx
        hits.append(hit)
        vals.append(m)
        idxs.append(idx)
        work = jnp.where(hit, -jnp.inf, work)
    ex = [jnp.exp(vk - vals[0]) for vk in vals]
    den = ex[0] + ex[1] + ex[2] + ex[3]
    multi = jnp.where(hits[0] | hits[1] | hits[2] | hits[3], 1.0, 0.0)
    ti = lax.broadcasted_iota(jnp.int32, (tr, tr), 0)
    tj = lax.broadcasted_iota(jnp.int32, (tr, tr), 1)
    before = jnp.where(ti < tj, 1.0, 0.0).astype(bf16)
    base = cnt_scr[...]
    excl = jnp.dot(multi.astype(bf16), before, preferred_element_type=f32) + base
    for kk in range(TOPK):
        e_out[kk:kk + 1, :] = idxs[kk]
        w_out[kk:kk + 1, :] = ex[kk] / den
        rank_out[kk:kk + 1, :] = jnp.sum(jnp.where(hits[kk], excl, 0.0), axis=0,
                                         keepdims=True).astype(jnp.int32)
    total = base + jnp.sum(multi, axis=1, keepdims=True)
    cnt_scr[...] = total
    cnt_out[...] = total.astype(jnp.int32)


def _route(logits_t, tr):
    t = logits_t.shape[1]
    sel = pl.BlockSpec((TOPK, tr), lambda i: (0, i))
    return pl.pallas_call(
        _route_kernel,
        name="route",
        grid=(t // tr,),
        in_specs=[pl.BlockSpec((NE, tr), lambda i: (0, i))],
        out_specs=[sel, sel, sel, pl.BlockSpec((NE, 1), lambda i: (0, 0))],
        out_shape=[jax.ShapeDtypeStruct((TOPK, t), jnp.int32),
                   jax.ShapeDtypeStruct((TOPK, t), f32),
                   jax.ShapeDtypeStruct((TOPK, t), jnp.int32),
                   jax.ShapeDtypeStruct((NE, 1), jnp.int32)],
        scratch_shapes=[pltpu.VMEM((NE, 1), f32)],
        compiler_params=_cparams(("arbitrary",)),
    )(logits_t)


def _dispatch_kernel(dest_ref, x2_ref, xs_in_ref, xs_ref, sem):
    del xs_in_ref
    i = pl.program_id(0)
    tt = dest_ref.shape[1]

    def copy(rw, kk):
        return pltpu.make_async_copy(x2_ref.at[pl.ds(i * tt + rw, 1)],
                                     xs_ref.at[pl.ds(dest_ref[kk, rw], 1)], sem)

    def issue(rw, carry):
        for kk in range(TOPK):
            copy(rw, kk).start()
        return carry

    lax.fori_loop(0, tt, issue, 0)

    def drain(rw, carry):
        for kk in range(TOPK):
            copy(rw, kk).wait()
        return carry

    lax.fori_loop(0, tt, drain, 0)


def _dispatch(dest, x2, xs_zero, tt):
    t = x2.shape[0]
    return pl.pallas_call(
        _dispatch_kernel,
        name="dispatch",
        grid=(t // tt,),
        in_specs=[pl.BlockSpec((TOPK, tt), lambda i: (0, i), memory_space=pltpu.SMEM),
                  pl.BlockSpec(memory_space=pl.ANY),
                  pl.BlockSpec(memory_space=pl.ANY)],
        out_specs=pl.BlockSpec(memory_space=pl.ANY),
        out_shape=jax.ShapeDtypeStruct(xs_zero.shape, xs_zero.dtype),
        scratch_shapes=[pltpu.SemaphoreType.DMA(())],
        input_output_aliases={2: 0},
        compiler_params=_cparams(("arbitrary",)),
    )(dest, x2, xs_zero)


def _expert_kernel(be_ref, xs_ref, wg_ref, bg_ref, wu_ref, bu_ref, wd_ref, bd_ref, ys_ref):
    del be_ref
    xb = xs_ref[...].astype(bf16)
    gt = jnp.minimum(jnp.dot(xb, wg_ref[...], preferred_element_type=f32) + bg_ref[...], LIMIT)
    up = jnp.clip(jnp.dot(xb, wu_ref[...], preferred_element_type=f32) + bu_ref[...], -LIMIT, LIMIT)
    act = (up + 1.0) * (gt * jax.nn.sigmoid(gt * ALPHA))
    ys_ref[...] = jnp.dot(act.astype(bf16), wd_ref[...], preferred_element_type=f32) + bd_ref[...]


def _experts(blk_e, xs, wg, bg, wu, bu, wd, bd):
    p = xs.shape[0]
    wspec = pl.BlockSpec((None, D, D), lambda i, be: (be[i], 0, 0))
    bspec = pl.BlockSpec((None, 1, D), lambda i, be: (be[i], 0, 0))
    rows = pl.BlockSpec((EBLK, D), lambda i, be: (i, 0))
    return pl.pallas_call(
        _expert_kernel,
        name="experts",
        grid_spec=pltpu.PrefetchScalarGridSpec(
            num_scalar_prefetch=1, grid=(p // EBLK,),
            in_specs=[rows, wspec, bspec, wspec, bspec, wspec, bspec],
            out_specs=rows),
        out_shape=jax.ShapeDtypeStruct((p, D), f32),
        compiler_params=_cparams(("arbitrary",)),
    )(blk_e, xs, wg, bg, wu, bu, wd, bd)


def _combine_kernel(dest_ref, ys_ref, h_ref, w_ref, g_ref, o_ref, buf, sem):
    tt = h_ref.shape[0]

    def copy(rw, kk):
        return pltpu.make_async_copy(ys_ref.at[pl.ds(dest_ref[kk, rw], 1)],
                                     buf.at[kk, pl.ds(rw, 1)], sem)

    def issue(rw, carry):
        for kk in range(TOPK):
            copy(rw, kk).start()
        return carry

    lax.fori_loop(0, tt, issue, 0)

    def drain(rw, carry):
        for kk in range(TOPK):
            copy(rw, kk).wait()
        return carry

    lax.fori_loop(0, tt, drain, 0)

    w = w_ref[...]
    acc = h_ref[...]
    for kk in range(TOPK):
        acc = acc + w[:, kk:kk + 1] * buf[kk]
    o_ref[...] = acc * lax.rsqrt(jnp.mean(acc * acc, axis=-1, keepdims=True) + NORM_EPS) * g_ref[...]


def _combine(dest, ys, hres, gates_t, lnf, tt):
    t = hres.shape[0]
    return pl.pallas_call(
        _combine_kernel,
        name="combine",
        grid=(t // tt,),
        in_specs=[pl.BlockSpec((TOPK, tt), lambda i: (0, i), memory_space=pltpu.SMEM),
                  pl.BlockSpec(memory_space=pl.ANY),
                  pl.BlockSpec((tt, D), lambda i: (i, 0)),
                  pl.BlockSpec((tt, TOPK), lambda i: (i, 0)),
                  pl.BlockSpec((1, D), lambda i: (0, 0))],
        out_specs=pl.BlockSpec((tt, D), lambda i: (i, 0)),
        out_shape=jax.ShapeDtypeStruct((t, D), f32),
        scratch_shapes=[pltpu.VMEM((TOPK, tt, D), f32), pltpu.SemaphoreType.DMA(())],
        compiler_params=_cparams(("arbitrary",)),
    )(dest, ys, hres, gates_t, lnf)


def _row_tile(n, want):
    t = min(n, want)
    assert n % t == 0
    return t


def _mixer_inputs(x2d, seq_len, init, p):
    n = x2d.shape[0]
    rkv, cv, gt, lo = _inproj(x2d, p["ln1"], p["w_in"], _row_tile(n, 512))
    outs = _mix(rkv, cv, lo, init, p["vec_mix"], p["ww2"], p["wa2"], p["wg2"], p["cw"], p["seg"],
                seq_len, _row_tile(seq_len, 256))
    return (rkv, cv, lo), gt, outs


def kernel(x, meta_tokens, ln1_g, w_in, mu_r, mu_k, mu_v, mu_w, mu_a, mu_g, w0, w_w1, w_w2, a0, w_a1, w_a2, w_g1, w_g2, k_k, k_a, r_k, lnx_g, lnx_b, w_o_rwkv, conv_w, w_o_conv, w_o, ln2_g, w_router, b_router, w_e_gate, b_e_gate, w_e_up, b_e_up, w_e_down, b_e_down, lnf_g):
    nb, seq, _ = x.shape
    t = nb * seq
    assert ln1_g.shape[0] == 1, "single layer"

    muw, mua, mug = mu_w[0][:, None], mu_a[0][:, None], mu_g[0][:, None]
    lora_cur = jnp.concatenate([(1 - muw) * w_w1[0], (1 - mua) * w_a1[0], (1 - mug) * w_g1[0]], axis=1)
    lora_prev = jnp.concatenate([muw * w_w1[0], mua * w_a1[0], mug * w_g1[0]], axis=1)
    zrow = jnp.zeros((1, R), f32)
    p = {
        "ln1": ln1_g[0][None, :],
        "w_in": jnp.concatenate([w_in[0], lora_cur, lora_prev], axis=1).astype(bf16),
        "vec_mix": jnp.concatenate([mu_r, mu_k, mu_v, w0, a0, k_k, k_a, zrow], axis=0),
        "ww2": w_w2[0].astype(bf16), "wa2": w_a2[0].astype(bf16), "wg2": w_g2[0].astype(bf16),
        "cw": jnp.concatenate([conv_w[0], jnp.zeros((5, R), f32)], axis=0),
        "seg": (jnp.arange(R)[:, None] // HS == jnp.arange(R)[None, :] // HS).astype(bf16),
    }
    vec_merge = jnp.concatenate([lnx_g, lnx_b, r_k, jnp.zeros((5, R), f32)], axis=0)

    zero_init = (jnp.zeros((HALO, 3 * R), bf16), jnp.zeros((HALO, 3 * R), bf16),
                 jnp.zeros((HALO, 2 * LORA), bf16))
    meta_proj, _, meta_ops = _mixer_inputs(meta_tokens.astype(f32), NMETA, zero_init, p)
    pad = lambda z: jnp.pad(z, ((CHUNK - NMETA, 0), (0, 0)))[None]
    r_m, k_m, v_m, kk_m, b_m, lw_m = (pad(z) for z in meta_ops[:6])
    _, s_meta = _wkv(r_m, k_m, v_m, kk_m, b_m, lw_m, jnp.zeros((NH, HS, HS), f32))

    x2d = x.reshape(t, D)
    _, gt, ops = _mixer_inputs(x2d, seq, meta_proj, p)
    r, k, v, kk, b, lw, g, cb = ops
    as3 = lambda z: z.reshape(nb, seq, R)
    y, _ = _wkv(as3(r), as3(k), as3(v), as3(kk), as3(b), as3(lw), s_meta[0])
    hres, x2, logits_t = _merge(
        y.reshape(t, R), r, k, v, g, cb, gt, x2d, vec_merge, ln2_g[0][None, :], p["seg"],
        w_o_rwkv[0].astype(bf16), w_o_conv[0].astype(bf16), w_o[0].astype(bf16),
        w_router[0].T, b_router[0][:, None], _row_tile(t, 256))

    top_e, gates, rank, counts = _route(logits_t, _row_tile(t, 512))
    counts = counts[:, 0]
    padded = ((counts + EBLK - 1) // EBLK) * EBLK
    pend = jnp.cumsum(padded)
    pstart = pend - padded
    dest = pstart[top_e] + rank
    n_rows = t * TOPK + NE * EBLK
    n_blk = n_rows // EBLK
    blk_start = jnp.arange(n_blk, dtype=jnp.int32) * EBLK
    blk_e = jnp.minimum(jnp.searchsorted(pend, blk_start, side="right"), NE - 1).astype(jnp.int32)
    xs = _dispatch(dest, x2, jnp.zeros((n_rows, D), f32), _row_tile(t, 256))
    ys = _experts(blk_e, xs,
                  w_e_gate[0].astype(bf16), b_e_gate[0][:, None, :],
                  w_e_up[0].astype(bf16), b_e_up[0][:, None, :],
                  w_e_down[0].astype(bf16), b_e_down[0][:, None, :])
    out = _combine(dest, ys, hres, gates.T, lnf_g[None, :], _row_tile(t, 128))
    return out.reshape(nb, seq, D)
```

```python
import functools

import jax
import jax.numpy as jnp
from jax import lax
from jax.experimental import pallas as pl
from jax.experimental.pallas import tpu as pltpu

D = 1024
R = 512
NH = 8
HS = 64
NE = 32
TOPK = 4
NMETA = 16
CHUNK = 64
EBLK = 256
NORM_EPS = 1e-5
LNX_EPS = 64e-5
ALPHA = 1.702
LIMIT = 7.0
LORA_W, LORA_A, LORA_G = 64, 64, 128
LORA = LORA_W + LORA_A + LORA_G
NPROJ = 3 * R + 3 * R + 2 * D + 2 * LORA
VMEM_LIMIT = 56 * 1024 * 1024

f32 = jnp.float32
bf16 = jnp.bfloat16


def _bdot(a, b):
    return jnp.dot(a.astype(bf16), b.astype(bf16), preferred_element_type=f32)


def _bdot_nt(a, b):
    return lax.dot_general(a.astype(bf16), b.astype(bf16), (((1,), (1,)), ((), ())),
                           preferred_element_type=f32)


def _bdot_tn(a, b):
    return lax.dot_general(a.astype(bf16), b.astype(bf16), (((0,), (0,)), ((), ())),
                           preferred_element_type=f32)


def _split3(x):
    h = x.astype(bf16)
    r1 = x - h.astype(f32)
    m = r1.astype(bf16)
    l = (r1 - m.astype(f32)).astype(bf16)
    return h, m, l


def _head_sum(x, seg):
    h, m, l = _split3(x)
    dot = functools.partial(jnp.dot, preferred_element_type=f32)
    return dot(h, seg) + dot(m, seg) + dot(l, seg)


def _cparams(sem):
    return pltpu.CompilerParams(dimension_semantics=sem, vmem_limit_bytes=VMEM_LIMIT)


def _inproj_kernel(x_ref, g_ref, w_ref, rkv_ref, cv_ref, gt_ref, lo_ref):
    x = x_ref[...]
    xn = x * lax.rsqrt(jnp.mean(x * x, axis=-1, keepdims=True) + NORM_EPS) * g_ref[...]
    xb = xn.astype(bf16)
    col = 0
    for ref in (rkv_ref, cv_ref, gt_ref, lo_ref):
        width = ref.shape[-1]
        for c in range(0, width, 512):
            ref[:, c:c + 512] = jnp.dot(
                xb, w_ref[:, col + c:col + c + 512], preferred_element_type=f32
            ).astype(ref.dtype)
        col += width


def _inproj(x2d, g, w, tm):
    t = x2d.shape[0]
    widths = (3 * R, 3 * R, 2 * D, 2 * LORA)
    return pl.pallas_call(
        _inproj_kernel,
        name="inproj",
        grid=(t // tm,),
        in_specs=[pl.BlockSpec((tm, D), lambda i: (i, 0)),
                  pl.BlockSpec((1, D), lambda i: (0, 0)),
                  pl.BlockSpec((D, NPROJ), lambda i: (0, 0))],
        out_specs=[pl.BlockSpec((tm, n), lambda i: (i, 0)) for n in widths],
        out_shape=[jax.ShapeDtypeStruct((t, n), bf16) for n in widths],
        compiler_params=_cparams(("parallel",)),
    )(x2d, g, w)


HALO = 16


def _mix_kernel(tiles_per_seq, rkv_ref, cv_ref, lo_ref, rkv_h_ref, cv_h_ref, lo_h_ref,
                rkv_i_ref, cv_i_ref, lo_i_ref, vec_ref, ww2_ref, wa2_ref, wg2_ref, cw_ref,
                seg_ref, r_out, k_out, v_out, kk_out, b_out, lw_out, g_out, cb_out):
    first = (pl.program_id(0) % tiles_per_seq) == 0
    tm = rkv_ref.shape[0]
    row = lax.broadcasted_iota(jnp.int32, (tm, 1), 0)

    def history(h_ref, i_ref):
        return jnp.where(first, i_ref[...].astype(f32), h_ref[...].astype(f32))

    def shifted(cur, hist, n):
        out = pltpu.roll(cur, n, axis=0)
        for j in range(n):
            out = jnp.where(row == j, hist[HALO - n + j:HALO - n + j + 1, :], out)
        return out

    vec = vec_ref[...]
    mu_r, mu_k, mu_v = vec[0:1, :], vec[1:2, :], vec[2:3, :]
    w0, a0, k_k, k_a = vec[3:4, :], vec[4:5, :], vec[5:6, :], vec[6:7, :]

    rkv = rkv_ref[...].astype(f32)
    rkv_prev = shifted(rkv, history(rkv_h_ref, rkv_i_ref), 1)
    pr, pk, pv = rkv[:, :R], rkv[:, R:2 * R], rkv[:, 2 * R:]
    r = pr + (rkv_prev[:, :R] - pr) * mu_r
    k = pk + (rkv_prev[:, R:2 * R] - pk) * mu_k
    v = pv + (rkv_prev[:, 2 * R:] - pv) * mu_v

    lo = lo_ref[...].astype(f32)
    lo_prev = shifted(lo, history(lo_h_ref, lo_i_ref), 1)
    mixed = lo[:, :LORA] + lo_prev[:, LORA:]
    hw = jnp.tanh(mixed[:, :LORA_W])
    ha = mixed[:, LORA_W:LORA_W + LORA_A]
    hg = jax.nn.sigmoid(mixed[:, LORA_W + LORA_A:])
    w_log = -jax.nn.softplus(-(w0 + _bdot(hw, ww2_ref[...]))) - 0.5
    a = jax.nn.sigmoid(a0 + _bdot(ha, wa2_ref[...]))
    g = _bdot(hg, wg2_ref[...])

    kk = k * k_k
    norm = jnp.sqrt(_head_sum(kk * kk, seg_ref[...]))
    kk = kk / jnp.maximum(norm, 1e-12)
    k = k * (1.0 + (a - 1.0) * k_a)

    r_out[...] = r
    k_out[...] = k
    v_out[...] = v
    kk_out[...] = kk
    b_out[...] = kk * a
    lw_out[...] = -jnp.exp(w_log)
    g_out[...] = g

    cv = cv_ref[...].astype(f32)
    hist = history(cv_h_ref, cv_i_ref)
    u = cv[:, R:2 * R] * cv[:, 2 * R:]
    uh = hist[:, R:2 * R] * hist[:, 2 * R:]
    cw = cw_ref[...]
    conv = cw[2:3, :] * u + cw[1:2, :] * shifted(u, uh, 1) + cw[0:1, :] * shifted(u, uh, 2)
    cb_out[...] = cv[:, :R] * conv


def _mix(rkv, cv, lo, init, vec, ww2, wa2, wg2, cw, seg, seq_len, tm):
    t = rkv.shape[0]
    tiles_per_seq = seq_len // tm
    hb = tm // HALO

    def cur(n):
        return pl.BlockSpec((tm, n), lambda i: (i, 0))

    def halo(n):
        return pl.BlockSpec((HALO, n), lambda i: (jnp.maximum(i * hb - 1, 0), 0))

    def whole(shape):
        return pl.BlockSpec(shape, lambda i: (0,) * len(shape))

    out_spec = pl.BlockSpec((tm, R), lambda i: (i, 0))
    return pl.pallas_call(
        functools.partial(_mix_kernel, tiles_per_seq),
        name="mix",
        grid=(t // tm,),
        in_specs=[cur(3 * R), cur(3 * R), cur(2 * LORA),
                  halo(3 * R), halo(3 * R), halo(2 * LORA),
                  whole((HALO, 3 * R)), whole((HALO, 3 * R)), whole((HALO, 2 * LORA)),
                  whole((8, R)), whole((LORA_W, R)), whole((LORA_A, R)), whole((LORA_G, R)),
                  whole((8, R)), whole((R, R))],
        out_specs=[out_spec] * 8,
        out_shape=[jax.ShapeDtypeStruct((t, R), f32)] * 8,
        compiler_params=_cparams(("parallel",)),
    )(rkv, cv, lo, rkv, cv, lo, *init, vec, ww2, wa2, wg2, cw, seg)


def _wkv_kernel(r_ref, k_ref, v_ref, kk_ref, b_ref, lw_ref, s0_ref, y_ref, sT_ref, s_scr):
    c = pl.program_id(1)

    @pl.when(c == 0)
    def _():
        s_scr[...] = s0_ref[...]

    C = CHUNK
    ri = lax.broadcasted_iota(jnp.int32, (C, C), 0)
    ci = lax.broadcasted_iota(jnp.int32, (C, C), 1)
    incl = ri >= ci
    strict = ri > ci
    ri2 = lax.broadcasted_iota(jnp.int32, (C, 2 * C), 0)
    ci2 = lax.broadcasted_iota(jnp.int32, (C, 2 * C), 1)
    incl2 = ri2 >= jnp.where(ci2 >= C, ci2 - C, ci2)
    tri =jnp.where(incl, 1.0, 0.0).astype(bf16)
    eye = jnp.where(ri == ci, 1.0, 0.0).astype(f32)

    lw = lw_ref[...]
    h, m, l = _split3(lw)
    dot = functools.partial(jnp.dot, preferred_element_type=f32)
    cum = dot(tri, h) + dot(tri, m) + dot(tri, l)
    tot = cum[C - 1:C, :]
    r = r_ref[...]
    k = k_ref[...]
    v = v_ref[...]
    b = b_ref[...]
    g_inv = jnp.exp(-cum)
    rt = r * jnp.exp(cum)
    kt = k * g_inv
    bt = b * g_inv
    at = -kk_ref[...] * jnp.exp(cum - lw)
    tail = jnp.exp(tot - cum)
    bh = b * tail
    kh = k * tail
    g_tot = jnp.exp(tot)

    hs = range(NH)
    sls = [slice(hd * HS, (hd + 1) * HS) for hd in hs]
    s_all = s_scr[...]
    lhs = [jnp.concatenate([at[:, sl], rt[:, sl]], axis=0).astype(bf16) for sl in sls]
    rhs = [jnp.concatenate([bt[:, sl], kt[:, sl]], axis=0).astype(bf16) for sl in sls]
    amat = [_bdot_nt(lhs[hd], rhs[hd]) for hd in hs]
    a_ab = [jnp.where(strict, amat[hd][:C, :C], 0.0) for hd in hs]
    a_ak = [jnp.where(strict, amat[hd][:C, C:], 0.0) for hd in hs]
    a_r = [jnp.where(incl2, amat[hd][C:, :], 0.0) for hd in hs]
    x = [eye + a_ab[hd] for hd in hs]
    p = a_ab
    for _ in range(5):
        pb = [p[hd].astype(bf16) for hd in hs]
        p = [dot(pb[hd], pb[hd]) for hd in hs]
        x = [x[hd] + _bdot(x[hd], p[hd]) for hd in hs]
    vh = [v[:, sl] for sl in sls]
    sh = [_bdot_nt(lhs[hd], s_all[hd]) for hd in hs]
    av = [_bdot(a_ak[hd], vh[hd]) for hd in hs]
    u = [_bdot(x[hd], sh[hd][:C] + av[hd]) for hd in hs]
    uv = [jnp.concatenate([u[hd], vh[hd]], axis=0).astype(bf16) for hd in hs]
    ys = [sh[hd][C:] + _bdot(a_r[hd], uv[hd]) for hd in hs]
    bk = [jnp.concatenate([bh[:, sl], kh[:, sl]], axis=0) for sl in sls]
    s_new = [s_all[hd] * g_tot[:, sls[hd]] + _bdot_tn(uv[hd], bk[hd]) for hd in hs]
    s_scr[...] = jnp.stack(s_new, axis=0)
    y_ref[...] = jnp.concatenate(ys, axis=1)

    @pl.when(c == pl.num_programs(1) - 1)
    def _():
        sT_ref[...] = s_scr[...]


def _wkv(r, k, v, kk, b, lw, s0):
    nb, seq, _ = r.shape
    blk = pl.BlockSpec((None, CHUNK, R), lambda i, c: (i, c, 0))
    return pl.pallas_call(
        _wkv_kernel,
        name="wkv",
        grid=(nb, seq // CHUNK),
        in_specs=[blk] * 6 + [pl.BlockSpec((NH, HS, HS), lambda i, c: (0, 0, 0))],
        out_specs=[blk, pl.BlockSpec((None, NH, HS, HS), lambda i, c: (i, 0, 0, 0))],
        out_shape=[jax.ShapeDtypeStruct((nb, seq, R), f32),
                   jax.ShapeDtypeStruct((nb, NH, HS, HS), f32)],
        scratch_shapes=[pltpu.VMEM((NH, HS, HS), f32)],
        compiler_params=_cparams(("parallel", "arbitrary")),
    )(r, k, v, kk, b, lw, s0)


def _merge_kernel(y_ref, r_ref, k_ref, v_ref, g_ref, cb_ref, gt_ref, x_ref, vec_ref, ln2_ref,
                  seg_ref, worw_ref, woc_ref, wo_ref, wr_ref, br_ref,
                  h_out, x2_out, lg_out):
    seg = seg_ref[...]
    vec = vec_ref[...]
    lnx_g, lnx_b, r_k = vec[0:1, :], vec[1:2, :], vec[2:3, :]
    y = y_ref[...]
    mean = _head_sum(y, seg) * (1.0 / HS)
    yc = y - mean
    var = _head_sum(yc * yc, seg) * (1.0 / HS)
    yn = yc * lax.rsqrt(var + LNX_EPS) * lnx_g + lnx_b
    v = v_ref[...]
    bonus = _head_sum(r_ref[...] * k_ref[...] * r_k, seg) * v
    y_a = _bdot((yn + bonus) * g_ref[...], worw_ref[...])
    y_b = _bdot(cb_ref[...], woc_ref[...])
    gt = gt_ref[...].astype(f32)
    merged = jax.nn.sigmoid(gt[:, :D]) * y_a + jax.nn.sigmoid(gt[:, D:]) * y_b
    hres = x_ref[...] + _bdot(merged, wo_ref[...])
    h_out[...] = hres
    x2 = hres * lax.rsqrt(jnp.mean(hres * hres, axis=-1, keepdims=True) + NORM_EPS) * ln2_ref[...]
    x2_out[...] = x2
    xh, xm, _ = _split3(x2)
    wh, wm, _ = _split3(wr_ref[...])
    nt = functools.partial(lax.dot_general, dimension_numbers=(((1,), (1,)), ((), ())),
                           preferred_element_type=f32)
    lg_out[...] = nt(wh, xh) + nt(wh, xm) + nt(wm, xh) + br_ref[...]


def _merge(y, r, k, v, g, cb, gt, x2d, vec, ln2, seg, worw, woc, wo, wr_t, br, tm):
    t = y.shape[0]

    def rows(n):
        return pl.BlockSpec((tm, n), lambda i: (i, 0))

    def whole(shape):
        return pl.BlockSpec(shape, lambda i: (0,) * len(shape))

    return pl.pallas_call(
        _merge_kernel,
        name="merge",
        grid=(t // tm,),
        in_specs=[rows(R)] * 6 + [rows(2 * D), rows(D), whole((8, R)), whole((1, D)),
                                   whole((R, R)), whole((R, D)), whole((R, D)), whole((D, D)),
                                   whole((NE, D)), whole((NE, 1))],
        out_specs=[rows(D), rows(D), pl.BlockSpec((NE, tm), lambda i: (0, i))],
        out_shape=[jax.ShapeDtypeStruct((t, D), f32), jax.ShapeDtypeStruct((t, D), f32),
                   jax.ShapeDtypeStruct((NE, t), f32)],
        compiler_params=_cparams(("parallel",)),
    )(y, r, k, v, g, cb, gt, x2d, vec, ln2, seg, worw, woc, wo, wr_t, br)


def _route_kernel(lg_ref, e_out, w_out, rank_out, cnt_out, cnt_scr):
    i = pl.program_id(0)

    @pl.when(i == 0)
    def _():
        cnt_scr[...] = jnp.zeros_like(cnt_scr)

    lg = lg_ref[...]
    tr = lg.shape[1]
    erow = lax.broadcasted_iota(jnp.int32, lg.shape, 0)
    work = lg
    hits, vals, idxs = [], [], []
    for _ in range(TOPK):
        m = jnp.max(work, axis=0, keepdims=True)
        idx = jnp.min(jnp.where(work == m, erow, NE), axis=0, keepdims=True)
        hit = erow == idx
        hits.append(hit)
        vals.append(m)
        idxs.append(idx)
        work = jnp.where(hit, -jnp.inf, work)
    ex = [jnp.exp(vk - vals[0]) for vk in vals]
    den = ex[0] + ex[1] + ex[2] + ex[3]
    multi = jnp.where(hits[0] | hits[1] | hits[2] | hits[3], 1.0, 0.0)
    ti = lax.broadcasted_iota(jnp.int32, (tr, tr), 0)
    tj = lax.broadcasted_iota(jnp.int32, (tr, tr), 1)
    before = jnp.where(ti < tj, 1.0, 0.0).astype(bf16)
    base = cnt_scr[...]
    excl = jnp.dot(multi.astype(bf16), before, preferred_element_type=f32) + base
    for kk in range(TOPK):
        e_out[kk:kk + 1, :] = idxs[kk]
        w_out[kk:kk + 1, :] = ex[kk] / den
        rank_out[kk:kk + 1, :] = jnp.sum(jnp.where(hits[kk], excl, 0.0), axis=0,
                                         keepdims=True).astype(jnp.int32)
    total = base + jnp.sum(multi, axis=1, keepdims=True)
    cnt_scr[...] = total
    cnt_out[...] = total.astype(jnp.int32)


def _route(logits_t, tr):
    t = logits_t.shape[1]
    sel = pl.BlockSpec((TOPK, tr), lambda i: (0, i))
    return pl.pallas_call(
        _route_kernel,
        name="route",
        grid=(t // tr,),
        in_specs=[pl.BlockSpec((NE, tr), lambda i: (0, i))],
        out_specs=[sel, sel, sel, pl.BlockSpec((NE, 1), lambda i: (0, 0))],
        out_shape=[jax.ShapeDtypeStruct((TOPK, t), jnp.int32),
                   jax.ShapeDtypeStruct((TOPK, t), f32),
                   jax.ShapeDtypeStruct((TOPK, t), jnp.int32),
                   jax.ShapeDtypeStruct((NE, 1), jnp.int32)],
        scratch_shapes=[pltpu.VMEM((NE, 1), f32)],
        compiler_params=_cparams(("arbitrary",)),
    )(logits_t)


def _dispatch_kernel(dest_ref, tail_ref, x2_ref, xs_ref, zbuf, sem, zsem):
    tt = x2_ref.shape[0]

    @pl.when(pl.program_id(0) == 0)
    def _():
        zbuf[...] = jnp.zeros_like(zbuf)

        def zero_copy(e):
            start = pl.multiple_of(tail_ref[e], EBLK)
            return pltpu.make_async_copy(zbuf, xs_ref.at[pl.ds(start, EBLK)], zsem)

        for e in range(NE):
            @pl.when(tail_ref[e] >= 0)
            def _():
                zero_copy(e).start()
        for e in range(NE):
            @pl.when(tail_ref[e] >= 0)
            def _():
                zero_copy(e).wait()

        def unused_copy(blk):
            start = pl.multiple_of(blk * EBLK, EBLK)
            return pltpu.make_async_copy(zbuf, xs_ref.at[pl.ds(start, EBLK)], zsem)

        n_blk = xs_ref.shape[0] // EBLK
        lax.fori_loop(tail_ref[NE], n_blk, lambda blk, c: (unused_copy(blk).start(), c)[1], 0)
        lax.fori_loop(tail_ref[NE], n_blk, lambda blk, c: (unused_copy(blk).wait(), c)[1], 0)

    def issue(rw, carry):
        for kk in range(TOPK):
            pltpu.make_async_copy(x2_ref.at[pl.ds(rw, 1)],
                                  xs_ref.at[pl.ds(dest_ref[kk, rw], 1)], sem).start()
        return carry

    lax.fori_loop(0, tt, issue, 0, unroll=8)
    for kk in range(TOPK):
        pltpu.make_async_copy(x2_ref, xs_ref.at[pl.ds(0, tt)], sem).wait()


def _dispatch(dest, tail, x2, n_rows, tt):
    t = x2.shape[0]
    return pl.pallas_call(
        _dispatch_kernel,
        name="dispatch",
        grid=(t // tt,),
        in_specs=[pl.BlockSpec((TOPK, tt), lambda i: (0, i), memory_space=pltpu.SMEM),
                  pl.BlockSpec(memory_space=pltpu.SMEM),
                  pl.BlockSpec((tt, D), lambda i: (i, 0))],
        out_specs=pl.BlockSpec(memory_space=pl.ANY),
        out_shape=jax.ShapeDtypeStruct((n_rows, D), f32),
        scratch_shapes=[pltpu.VMEM((EBLK, D), f32), pltpu.SemaphoreType.DMA(()),
                        pltpu.SemaphoreType.DMA(())],
        compiler_params=_cparams(("arbitrary",)),
    )(dest, tail, x2)


def _expert_kernel(be_ref, nu_ref, xs_ref, wg_ref, bg_ref, wu_ref, bu_ref, wd_ref, bd_ref, ys_ref):
    del be_ref

    @pl.when(pl.program_id(0) < nu_ref[0])
    def _():
        xb = xs_ref[...].astype(bf16)
        gt = jnp.minimum(jnp.dot(xb, wg_ref[...], preferred_element_type=f32) + bg_ref[...], LIMIT)
        up = jnp.clip(jnp.dot(xb, wu_ref[...], preferred_element_type=f32) + bu_ref[...],
                      -LIMIT, LIMIT)
        act = (up + 1.0) * (gt * jax.nn.sigmoid(gt * ALPHA))
        ys_ref[...] = jnp.dot(act.astype(bf16), wd_ref[...], preferred_element_type=f32) + bd_ref[...]

    @pl.when(pl.program_id(0) >= nu_ref[0])
    def _():
        ys_ref[...] = jnp.zeros_like(ys_ref)


def _experts(blk_e, n_used, xs, wg, bg, wu, bu, wd, bd):
    p = xs.shape[0]
    wspec = pl.BlockSpec((None, D, D), lambda i, be, nu: (be[i], 0, 0))
    bspec = pl.BlockSpec((None, 1, D), lambda i, be, nu: (be[i], 0, 0))
    xrows = pl.BlockSpec((EBLK, D), lambda i, be, nu: (jnp.where(i < nu[0], i, 0), 0))
    yrows = pl.BlockSpec((EBLK, D), lambda i, be, nu: (i, 0))
    return pl.pallas_call(
        _expert_kernel,
        name="experts",
        grid_spec=pltpu.PrefetchScalarGridSpec(
            num_scalar_prefetch=2, grid=(p // EBLK,),
            in_specs=[xrows, wspec, bspec, wspec, bspec, wspec, bspec],
            out_specs=yrows),
        out_shape=jax.ShapeDtypeStruct((p, D), f32),
        compiler_params=_cparams(("arbitrary",)),
    )(blk_e, n_used, xs, wg, bg, wu, bu, wd, bd)


def _combine_kernel(dest_ref, ys_ref, h_ref, w_ref, g_ref, o_ref, buf, sem):
    tt = h_ref.shape[0]

    def issue(rw, carry):
        for kk in range(TOPK):
            pltpu.make_async_copy(ys_ref.at[pl.ds(dest_ref[kk, rw], 1)],
                                  buf.at[kk, pl.ds(rw, 1)], sem).start()
        return carry

    lax.fori_loop(0, tt, issue, 0, unroll=8)
    for kk in range(TOPK):
        pltpu.make_async_copy(ys_ref.at[pl.ds(0, tt)], buf.at[kk], sem).wait()

    w = w_ref[...]
    acc = h_ref[...]
    for kk in range(TOPK):
        acc = acc + w[:, kk:kk + 1] * buf[kk]
    o_ref[...] = acc * lax.rsqrt(jnp.mean(acc * acc, axis=-1, keepdims=True) + NORM_EPS) * g_ref[...]


def _combine(dest, ys, hres, gates_t, lnf, tt):
    t = hres.shape[0]
    return pl.pallas_call(
        _combine_kernel,
        name="combine",
        grid=(t // tt,),
        in_specs=[pl.BlockSpec((TOPK, tt), lambda i: (0, i), memory_space=pltpu.SMEM),
                  pl.BlockSpec(memory_space=pl.ANY),
                  pl.BlockSpec((tt, D), lambda i: (i, 0)),
                  pl.BlockSpec((tt, TOPK), lambda i: (i, 0)),
                  pl.BlockSpec((1, D), lambda i: (0, 0))],
        out_specs=pl.BlockSpec((tt, D), lambda i: (i, 0)),
        out_shape=jax.ShapeDtypeStruct((t, D), f32),
        scratch_shapes=[pltpu.VMEM((TOPK, tt, D), f32), pltpu.SemaphoreType.DMA(())],
        compiler_params=_cparams(("arbitrary",)),
    )(dest, ys, hres, gates_t, lnf)


def _row_tile(n, want):
    t = min(n, want)
    assert n % t == 0
    return t


def _mixer_inputs(x2d, seq_len, init, p):
    n = x2d.shape[0]
    rkv, cv, gt, lo = _inproj(x2d, p["ln1"], p["w_in"], _row_tile(n, 512))
    outs = _mix(rkv, cv, lo, init, p["vec_mix"], p["ww2"], p["wa2"], p["wg2"], p["cw"], p["seg"],
                seq_len, _row_tile(seq_len, 256))
    return (rkv, cv, lo), gt, outs


def kernel(x, meta_tokens, ln1_g, w_in, mu_r, mu_k, mu_v, mu_w, mu_a, mu_g, w0, w_w1, w_w2, a0, w_a1, w_a2, w_g1, w_g2, k_k, k_a, r_k, lnx_g, lnx_b, w_o_rwkv, conv_w, w_o_conv, w_o, ln2_g, w_router, b_router, w_e_gate, b_e_gate, w_e_up, b_e_up, w_e_down, b_e_down, lnf_g):
    nb, seq, _ = x.shape
    t = nb * seq
    assert ln1_g.shape[0] == 1, "single layer"

    muw, mua, mug = mu_w[0][:, None], mu_a[0][:, None], mu_g[0][:, None]
    lora_cur = jnp.concatenate([(1 - muw) * w_w1[0], (1 - mua) * w_a1[0], (1 - mug) * w_g1[0]], axis=1)
    lora_prev = jnp.concatenate([muw * w_w1[0], mua * w_a1[0], mug * w_g1[0]], axis=1)
    zrow = jnp.zeros((1, R), f32)
    p = {
        "ln1": ln1_g[0][None, :],
        "w_in": jnp.concatenate([w_in[0], lora_cur, lora_prev], axis=1).astype(bf16),
        "vec_mix": jnp.concatenate([mu_r, mu_k, mu_v, w0, a0, k_k, k_a, zrow], axis=0),
        "ww2": w_w2[0].astype(bf16), "wa2": w_a2[0].astype(bf16), "wg2": w_g2[0].astype(bf16),
        "cw": jnp.concatenate([conv_w[0], jnp.zeros((5, R), f32)], axis=0),
        "seg": (jnp.arange(R)[:, None] // HS == jnp.arange(R)[None, :] // HS).astype(bf16),
    }
    vec_merge = jnp.concatenate([lnx_g, lnx_b, r_k, jnp.zeros((5, R), f32)], axis=0)

    zero_init = (jnp.zeros((HALO, 3 * R), bf16), jnp.zeros((HALO, 3 * R), bf16),
                 jnp.zeros((HALO, 2 * LORA), bf16))
    meta_proj, _, meta_ops = _mixer_inputs(meta_tokens.astype(f32), NMETA, zero_init, p)
    pad = lambda z: jnp.pad(z, ((CHUNK - NMETA, 0), (0, 0)))[None]
    r_m, k_m, v_m, kk_m, b_m, lw_m = (pad(z) for z in meta_ops[:6])
    _, s_meta = _wkv(r_m, k_m, v_m, kk_m, b_m, lw_m, jnp.zeros((NH, HS, HS), f32))

    x2d = x.reshape(t, D)
    _, gt, ops = _mixer_inputs(x2d, seq, meta_proj, p)
    r, k, v, kk, b, lw, g, cb = ops
    as3 = lambda z: z.reshape(nb, seq, R)
    y, _ = _wkv(as3(r), as3(k), as3(v), as3(kk), as3(b), as3(lw), s_meta[0])
    hres, x2, logits_t = _merge(
        y.reshape(t, R), r, k, v, g, cb, gt, x2d, vec_merge, ln2_g[0][None, :], p["seg"],
        w_o_rwkv[0].astype(bf16), w_o_conv[0].astype(bf16), w_o[0].astype(bf16),
        w_router[0].T, b_router[0][:, None], _row_tile(t, 256))

    top_e, gates, rank, counts = _route(logits_t, _row_tile(t, 512))
    counts = counts[:, 0]
    padded = ((counts + EBLK - 1) // EBLK) * EBLK
    pend = jnp.cumsum(padded)
    pstart = pend - padded
    eids = jnp.arange(NE, dtype=jnp.int32)[:, None, None]
    dest = rank + jnp.sum(jnp.where(top_e[None] == eids, pstart[:, None, None], 0), axis=0)
    n_rows = t * TOPK + NE * EBLK
    n_blk = n_rows // EBLK
    blk_start = jnp.arange(n_blk, dtype=jnp.int32) * EBLK
    blk_e = jnp.minimum(jnp.sum(pend[None, :] <= blk_start[:, None], axis=1), NE - 1).astype(jnp.int32)
    n_used = (pend[NE - 1:] // EBLK).astype(jnp.int32)
    tail = jnp.concatenate([jnp.where(padded > 0, pend - EBLK, -1), n_used]).astype(jnp.int32)
    xs = _dispatch(dest, tail, x2, n_rows, _row_tile(t, 256))
    ys = _experts(blk_e, n_used, xs,
                  w_e_gate[0].astype(bf16), b_e_gate[0][:, None, :],
                  w_e_up[0].astype(bf16), b_e_up[0][:, None, :],
                  w_e_down[0].astype(bf16), b_e_down[0][:, None, :])
    out = _combine(dest, ys, hres, gates.T, lnf_g[None, :], _row_tile(t, 128))
    return out.reshape(nb, seq, D)
```

```python
import functools

import jax
import jax.numpy as jnp
from jax import lax
from jax.experimental import pallas as pl
from jax.experimental.pallas import tpu as pltpu

D = 1024
R = 512
NH = 8
HS = 64
NE = 32
TOPK = 4
NMETA = 16
CHUNK = 64
EBLK = 256
NORM_EPS = 1e-5
LNX_EPS = 64e-5
ALPHA = 1.702
LIMIT = 7.0
LORA_W, LORA_A, LORA_G = 64, 64, 128
LORA = LORA_W + LORA_A + LORA_G
NPROJ = 3 * R + 3 * R + 2 * D + 2 * LORA
VMEM_LIMIT = 56 * 1024 * 1024

f32 = jnp.float32
bf16 = jnp.bfloat16


def _bdot(a, b):
    return jnp.dot(a.astype(bf16), b.astype(bf16), preferred_element_type=f32)


def _bdot_nt(a, b):
    return lax.dot_general(a.astype(bf16), b.astype(bf16), (((1,), (1,)), ((), ())),
                           preferred_element_type=f32)


def _bdot_tn(a, b):
    return lax.dot_general(a.astype(bf16), b.astype(bf16), (((0,), (0,)), ((), ())),
                           preferred_element_type=f32)


def _split3(x):
    h = x.astype(bf16)
    r1 = x - h.astype(f32)
    m = r1.astype(bf16)
    l = (r1 - m.astype(f32)).astype(bf16)
    return h, m, l


def _head_sum(x, seg):
    h, m, l = _split3(x)
    dot = functools.partial(jnp.dot, preferred_element_type=f32)
    return dot(h, seg) + dot(m, seg) + dot(l, seg)


def _cparams(sem):
    return pltpu.CompilerParams(dimension_semantics=sem, vmem_limit_bytes=VMEM_LIMIT)


def _inproj_kernel(x_ref, g_ref, w_ref, rkv_ref, cv_ref, gt_ref, lo_ref):
    x = x_ref[...]
    xn = x * lax.rsqrt(jnp.mean(x * x, axis=-1, keepdims=True) + NORM_EPS) * g_ref[...]
    xb = xn.astype(bf16)
    col = 0
    for ref in (rkv_ref, cv_ref, gt_ref, lo_ref):
        width = ref.shape[-1]
        for c in range(0, width, 512):
            ref[:, c:c + 512] = jnp.dot(
                xb, w_ref[:, col + c:col + c + 512], preferred_element_type=f32
            ).astype(ref.dtype)
        col += width


def _inproj(x2d, g, w, tm):
    t = x2d.shape[0]
    widths = (3 * R, 3 * R, 2 * D, 2 * LORA)
    return pl.pallas_call(
        _inproj_kernel,
        name="inproj",
        grid=(t // tm,),
        in_specs=[pl.BlockSpec((tm, D), lambda i: (i, 0)),
                  pl.BlockSpec((1, D), lambda i: (0, 0)),
                  pl.BlockSpec((D, NPROJ), lambda i: (0, 0))],
        out_specs=[pl.BlockSpec((tm, n), lambda i: (i, 0)) for n in widths],
        out_shape=[jax.ShapeDtypeStruct((t, n), bf16) for n in widths],
        compiler_params=_cparams(("parallel",)),
    )(x2d, g, w)


HALO = 16


def _mix_kernel(tiles_per_seq, rkv_ref, cv_ref, lo_ref, rkv_h_ref, cv_h_ref, lo_h_ref,
                rkv_i_ref, cv_i_ref, lo_i_ref, vec_ref, ww2_ref, wa2_ref, wg2_ref, cw_ref,
                seg_ref, r_out, k_out, v_out, kk_out, b_out, lw_out, g_out, cb_out):
    first = (pl.program_id(0) % tiles_per_seq) == 0
    tm = rkv_ref.shape[0]
    row = lax.broadcasted_iota(jnp.int32, (tm, 1), 0)

    def history(h_ref, i_ref):
        return jnp.where(first, i_ref[...].astype(f32), h_ref[...].astype(f32))

    def shifted(cur, hist, n):
        out = pltpu.roll(cur, n, axis=0)
        for j in range(n):
            out = jnp.where(row == j, hist[HALO - n + j:HALO - n + j + 1, :], out)
        return out

    vec = vec_ref[...]
    mu_r, mu_k, mu_v = vec[0:1, :], vec[1:2, :], vec[2:3, :]
    w0, a0, k_k, k_a = vec[3:4, :], vec[4:5, :], vec[5:6, :], vec[6:7, :]

    rkv = rkv_ref[...].astype(f32)
    rkv_prev = shifted(rkv, history(rkv_h_ref, rkv_i_ref), 1)
    pr, pk, pv = rkv[:, :R], rkv[:, R:2 * R], rkv[:, 2 * R:]
    r = pr + (rkv_prev[:, :R] - pr) * mu_r
    k = pk + (rkv_prev[:, R:2 * R] - pk) * mu_k
    v = pv + (rkv_prev[:, 2 * R:] - pv) * mu_v

    lo = lo_ref[...].astype(f32)
    lo_prev = shifted(lo, history(lo_h_ref, lo_i_ref), 1)
    mixed = lo[:, :LORA] + lo_prev[:, LORA:]
    hw = jnp.tanh(mixed[:, :LORA_W])
    ha = mixed[:, LORA_W:LORA_W + LORA_A]
    hg = jax.nn.sigmoid(mixed[:, LORA_W + LORA_A:])
    w_log = -jax.nn.softplus(-(w0 + _bdot(hw, ww2_ref[...]))) - 0.5
    a = jax.nn.sigmoid(a0 + _bdot(ha, wa2_ref[...]))
    g = _bdot(hg, wg2_ref[...])

    kk = k * k_k
    norm = jnp.sqrt(_head_sum(kk * kk, seg_ref[...]))
    kk = kk / jnp.maximum(norm, 1e-12)
    k = k * (1.0 + (a - 1.0) * k_a)

    r_out[...] = r
    k_out[...] = k
    v_out[...] = v
    kk_out[...] = kk
    b_out[...] = kk * a
    lw_out[...] = -jnp.exp(w_log)
    g_out[...] = g

    cv = cv_ref[...].astype(f32)
    hist = history(cv_h_ref, cv_i_ref)
    u = cv[:, R:2 * R] * cv[:, 2 * R:]
    uh = hist[:, R:2 * R] * hist[:, 2 * R:]
    cw = cw_ref[...]
    conv = cw[2:3, :] * u + cw[1:2, :] * shifted(u, uh, 1) + cw[0:1, :] * shifted(u, uh, 2)
    cb_out[...] = cv[:, :R] * conv


def _mix(rkv, cv, lo, init, vec, ww2, wa2, wg2, cw, seg, seq_len, tm):
    t = rkv.shape[0]
    tiles_per_seq = seq_len // tm
    hb = tm // HALO

    def cur(n):
        return pl.BlockSpec((tm, n), lambda i: (i, 0))

    def halo(n):
        return pl.BlockSpec((HALO, n), lambda i: (jnp.maximum(i * hb - 1, 0), 0))

    def whole(shape):
        return pl.BlockSpec(shape, lambda i: (0,) * len(shape))

    out_spec = pl.BlockSpec((tm, R), lambda i: (i, 0))
    return pl.pallas_call(
        functools.partial(_mix_kernel, tiles_per_seq),
        name="mix",
        grid=(t // tm,),
        in_specs=[cur(3 * R), cur(3 * R), cur(2 * LORA),
                  halo(3 * R), halo(3 * R), halo(2 * LORA),
                  whole((HALO, 3 * R)), whole((HALO, 3 * R)), whole((HALO, 2 * LORA)),
                  whole((8, R)), whole((LORA_W, R)), whole((LORA_A, R)), whole((LORA_G, R)),
                  whole((8, R)), whole((R, R))],
        out_specs=[out_spec] * 8,
        out_shape=[jax.ShapeDtypeStruct((t, R), f32)] * 8,
        compiler_params=_cparams(("parallel",)),
    )(rkv, cv, lo, rkv, cv, lo, *init, vec, ww2, wa2, wg2, cw, seg)


def _wkv_kernel(r_ref, k_ref, v_ref, kk_ref, b_ref, lw_ref, s0_ref, y_ref, sT_ref, s_scr):
    c = pl.program_id(1)

    @pl.when(c == 0)
    def _():
        s_scr[...] = s0_ref[...]

    C = CHUNK
    ri = lax.broadcasted_iota(jnp.int32, (C, C), 0)
    ci = lax.broadcasted_iota(jnp.int32, (C, C), 1)
    incl = ri >= ci
    strict = ri > ci
    ri2 = lax.broadcasted_iota(jnp.int32, (C, 2 * C), 0)
    ci2 = lax.broadcasted_iota(jnp.int32, (C, 2 * C), 1)
    incl2 = ri2 >= jnp.where(ci2 >= C, ci2 - C, ci2)
    tri =jnp.where(incl, 1.0, 0.0).astype(bf16)
    eye = jnp.where(ri == ci, 1.0, 0.0).astype(f32)

    lw = lw_ref[...]
    h, m, l = _split3(lw)
    dot = functools.partial(jnp.dot, preferred_element_type=f32)
    cum = dot(tri, h) + dot(tri, m) + dot(tri, l)
    tot = cum[C - 1:C, :]
    r = r_ref[...]
    k = k_ref[...]
    v = v_ref[...]
    b = b_ref[...]
    g_inv = jnp.exp(-cum)
    rt = r * jnp.exp(cum)
    kt = k * g_inv
    bt = b * g_inv
    at = -kk_ref[...] * jnp.exp(cum - lw)
    tail = jnp.exp(tot - cum)
    bh = b * tail
    kh = k * tail
    g_tot = jnp.exp(tot)

    hs = range(NH)
    sls = [slice(hd * HS, (hd + 1) * HS) for hd in hs]
    s_all = s_scr[...]
    lhs = [jnp.concatenate([at[:, sl], rt[:, sl]], axis=0).astype(bf16) for sl in sls]
    rhs = [jnp.concatenate([bt[:, sl], kt[:, sl]], axis=0).astype(bf16) for sl in sls]
    amat = [_bdot_nt(lhs[hd], rhs[hd]) for hd in hs]
    a_ab = [jnp.where(strict, amat[hd][:C, :C], 0.0) for hd in hs]
    a_ak = [jnp.where(strict, amat[hd][:C, C:], 0.0) for hd in hs]
    a_r = [jnp.where(incl2, amat[hd][C:, :], 0.0) for hd in hs]
    x = [eye + a_ab[hd] for hd in hs]
    p = a_ab
    for _ in range(5):
        pb = [p[hd].astype(bf16) for hd in hs]
        p = [dot(pb[hd], pb[hd]) for hd in hs]
        x = [x[hd] + _bdot(x[hd], p[hd]) for hd in hs]
    vh = [v[:, sl] for sl in sls]
    sh = [_bdot_nt(lhs[hd], s_all[hd]) for hd in hs]
    av = [_bdot(a_ak[hd], vh[hd]) for hd in hs]
    u = [_bdot(x[hd], sh[hd][:C] + av[hd]) for hd in hs]
    uv = [jnp.concatenate([u[hd], vh[hd]], axis=0).astype(bf16) for hd in hs]
    ys = [sh[hd][C:] + _bdot(a_r[hd], uv[hd]) for hd in hs]
    bk = [jnp.concatenate([bh[:, sl], kh[:, sl]], axis=0) for sl in sls]
    s_new = [s_all[hd] * g_tot[:, sls[hd]] + _bdot_tn(uv[hd], bk[hd]) for hd in hs]
    s_scr[...] = jnp.stack(s_new, axis=0)
    y_ref[...] = jnp.concatenate(ys, axis=1)

    @pl.when(c == pl.num_programs(1) - 1)
    def _():
        sT_ref[...] = s_scr[...]


def _wkv(r, k, v, kk, b, lw, s0):
    nb, seq, _ = r.shape
    blk = pl.BlockSpec((None, CHUNK, R), lambda i, c: (i, c, 0))
    return pl.pallas_call(
        _wkv_kernel,
        name="wkv",
        grid=(nb, seq // CHUNK),
        in_specs=[blk] * 6 + [pl.BlockSpec((NH, HS, HS), lambda i, c: (0, 0, 0))],
        out_specs=[blk, pl.BlockSpec((None, NH, HS, HS), lambda i, c: (i, 0, 0, 0))],
        out_shape=[jax.ShapeDtypeStruct((nb, seq, R), f32),
                   jax.ShapeDtypeStruct((nb, NH, HS, HS), f32)],
        scratch_shapes=[pltpu.VMEM((NH, HS, HS), f32)],
        compiler_params=_cparams(("parallel", "arbitrary")),
    )(r, k, v, kk, b, lw, s0)


def _merge_kernel(y_ref, r_ref, k_ref, v_ref, g_ref, cb_ref, gt_ref, x_ref, vec_ref, ln2_ref,
                  seg_ref, worw_ref, woc_ref, wo_ref, wr_ref, br_ref,
                  h_out, x2_out, lg_out):
    seg = seg_ref[...]
    vec = vec_ref[...]
    lnx_g, lnx_b, r_k = vec[0:1, :], vec[1:2, :], vec[2:3, :]
    y = y_ref[...]
    mean = _head_sum(y, seg) * (1.0 / HS)
    yc = y - mean
    var = _head_sum(yc * yc, seg) * (1.0 / HS)
    yn = yc * lax.rsqrt(var + LNX_EPS) * lnx_g + lnx_b
    v = v_ref[...]
    bonus = _head_sum(r_ref[...] * k_ref[...] * r_k, seg) * v
    y_a = _bdot((yn + bonus) * g_ref[...], worw_ref[...])
    y_b = _bdot(cb_ref[...], woc_ref[...])
    gt = gt_ref[...].astype(f32)
    merged = jax.nn.sigmoid(gt[:, :D]) * y_a + jax.nn.sigmoid(gt[:, D:]) * y_b
    hres = x_ref[...] + _bdot(merged, wo_ref[...])
    h_out[...] = hres
    x2 = hres * lax.rsqrt(jnp.mean(hres * hres, axis=-1, keepdims=True) + NORM_EPS) * ln2_ref[...]
    x2_out[...] = x2
    xh, xm, _ = _split3(x2)
    wh, wm, _ = _split3(wr_ref[...])
    nt = functools.partial(lax.dot_general, dimension_numbers=(((1,), (1,)), ((), ())),
                           preferred_element_type=f32)
    lg_out[...] = nt(wh, xh) + nt(wh, xm) + nt(wm, xh) + br_ref[...]


def _merge(y, r, k, v, g, cb, gt, x2d, vec, ln2, seg, worw, woc, wo, wr_t, br, tm):
    t = y.shape[0]

    def rows(n):
        return pl.BlockSpec((tm, n), lambda i: (i, 0))

    def whole(shape):
        return pl.BlockSpec(shape, lambda i: (0,) * len(shape))

    return pl.pallas_call(
        _merge_kernel,
        name="merge",
        grid=(t // tm,),
        in_specs=[rows(R)] * 6 + [rows(2 * D), rows(D), whole((8, R)), whole((1, D)),
                                   whole((R, R)), whole((R, D)), whole((R, D)), whole((D, D)),
                                   whole((NE, D)), whole((NE, 1))],
        out_specs=[rows(D), rows(D), pl.BlockSpec((NE, tm), lambda i: (0, i))],
        out_shape=[jax.ShapeDtypeStruct((t, D), f32), jax.ShapeDtypeStruct((t, D), f32),
                   jax.ShapeDtypeStruct((NE, t), f32)],
        compiler_params=_cparams(("parallel",)),
    )(y, r, k, v, g, cb, gt, x2d, vec, ln2, seg, worw, woc, wo, wr_t, br)


def _route_kernel(lg_ref, e_out, w_out, rank_out, cnt_out, cnt_scr):
    i = pl.program_id(0)

    @pl.when(i == 0)
    def _():
        cnt_scr[...] = jnp.zeros_like(cnt_scr)

    lg = lg_ref[...]
    tr = lg.shape[1]
    erow = lax.broadcasted_iota(jnp.int32, lg.shape, 0)
    work = lg
    hits, vals, idxs = [], [], []
    for _ in range(TOPK):
        m = jnp.max(work, axis=0, keepdims=True)
        idx = jnp.min(jnp.where(work == m, erow, NE), axis=0, keepdims=True)
        hit = erow == idx
        hits.append(hit)
        vals.append(m)
        idxs.append(idx)
        work = jnp.where(hit, -jnp.inf, work)
    ex = [jnp.exp(vk - vals[0]) for vk in vals]
    den = ex[0] + ex[1] + ex[2] + ex[3]
    multi = jnp.where(hits[0] | hits[1] | hits[2] | hits[3], 1.0, 0.0)
    ti = lax.broadcasted_iota(jnp.int32, (tr, tr), 0)
    tj = lax.broadcasted_iota(jnp.int32, (tr, tr), 1)
    before = jnp.where(ti < tj, 1.0, 0.0).astype(bf16)
    base = cnt_scr[...]
    excl = jnp.dot(multi.astype(bf16), before, preferred_element_type=f32) + base
    for kk in range(TOPK):
        e_out[kk:kk + 1, :] = idxs[kk]
        w_out[kk:kk + 1, :] = ex[kk] / den
        rank_out[kk:kk + 1, :] = jnp.sum(jnp.where(hits[kk], excl, 0.0), axis=0,
                                         keepdims=True).astype(jnp.int32)
    total = base + jnp.sum(multi, axis=1, keepdims=True)
    cnt_scr[...] = total
    cnt_out[...] = total.astype(jnp.int32)


def _route(logits_t, tr):
    t = logits_t.shape[1]
    sel = pl.BlockSpec((TOPK, tr), lambda i: (0, i))
    return pl.pallas_call(
        _route_kernel,
        name="route",
        grid=(t // tr,),
        in_specs=[pl.BlockSpec((NE, tr), lambda i: (0, i))],
        out_specs=[sel, sel, sel, pl.BlockSpec((NE, 1), lambda i: (0, 0))],
        out_shape=[jax.ShapeDtypeStruct((TOPK, t), jnp.int32),
                   jax.ShapeDtypeStruct((TOPK, t), f32),
                   jax.ShapeDtypeStruct((TOPK, t), jnp.int32),
                   jax.ShapeDtypeStruct((NE, 1), jnp.int32)],
        scratch_shapes=[pltpu.VMEM((NE, 1), f32)],
        compiler_params=_cparams(("arbitrary",)),
    )(logits_t)


def _expert_kernel(be_ref, nv_ref, tokc_ref, tokn_ref, slot_ref, x2_ref,
                   wg_ref, bg_ref, wu_ref, bu_ref, wd_ref, bd_ref, ys_ref,
                   xbuf, ybuf, gsem, ssem):
    del be_ref
    i = pl.program_id(0)
    nblk = pl.num_programs(0)
    cur = i % 2
    nxt = 1 - cur

    def gather(tok_ref, buf_idx):
        def body(rw, carry):
            pltpu.make_async_copy(x2_ref.at[pl.ds(tok_ref[0, rw], 1)],
                                  xbuf.at[buf_idx, pl.ds(rw, 1)], gsem.at[buf_idx]).start()
            return carry
        lax.fori_loop(0, EBLK, body, 0, unroll=8)

    def scatter_copy(buf_idx, rw, row):
        return pltpu.make_async_copy(ybuf.at[buf_idx, pl.ds(rw, 1)],
                                     ys_ref.at[pl.ds(row, 1)], ssem.at[buf_idx])

    def drain(blk, buf_idx):
        def body(rw, carry):
            scatter_copy(buf_idx, 0, 0).wait()
            return carry
        lax.fori_loop(0, nv_ref[blk], body, 0)

    @pl.when(i == 0)
    def _():
        gather(tokc_ref, 0)

    @pl.when(i + 1 < nblk)
    def _():
        @pl.when(nv_ref[i + 1] > 0)
        def _():
            gather(tokn_ref, nxt)

    @pl.when(i >= 2)
    def _():
        drain(i - 2, cur)

    nv = nv_ref[i]

    @pl.when(nv > 0)
    def _():
        pltpu.make_async_copy(x2_ref.at[pl.ds(0, EBLK)], xbuf.at[cur], gsem.at[cur]).wait()
        xb = xbuf[cur].astype(bf16)
        gt = jnp.minimum(jnp.dot(xb, wg_ref[...], preferred_element_type=f32) + bg_ref[...], LIMIT)
        up = jnp.clip(jnp.dot(xb, wu_ref[...], preferred_element_type=f32) + bu_ref[...],
                      -LIMIT, LIMIT)
        act = (up + 1.0) * (gt * jax.nn.sigmoid(gt * ALPHA))
        ybuf[cur] = jnp.dot(act.astype(bf16), wd_ref[...], preferred_element_type=f32) + bd_ref[...]

        def body(rw, carry):
            scatter_copy(cur, rw, slot_ref[0, rw]).start()
            return carry
        lax.fori_loop(0, nv, body, 0)

    @pl.when(i == nblk - 1)
    def _():
        drain(i - 1, nxt)
        drain(i, cur)


def _experts(blk_e, nvalid, tok_rows, slot_rows, x2, n_slots, wg, bg, wu, bu, wd, bd):
    n_blk = blk_e.shape[0]
    assert n_blk >= 2
    wspec = pl.BlockSpec((None, D, D), lambda i, be, nv: (be[i], 0, 0))
    bspec = pl.BlockSpec((None, 1, D), lambda i, be, nv: (be[i], 0, 0))
    idx_cur = pl.BlockSpec((None, 1, EBLK), lambda i, be, nv: (i, 0, 0), memory_space=pltpu.SMEM)
    idx_nxt = pl.BlockSpec((None, 1, EBLK), lambda i, be, nv: (jnp.minimum(i + 1, n_blk - 1), 0, 0),
                           memory_space=pltpu.SMEM)
    hbm = pl.BlockSpec(memory_space=pl.ANY)
    return pl.pallas_call(
        _expert_kernel,
        name="experts",
        grid_spec=pltpu.PrefetchScalarGridSpec(
            num_scalar_prefetch=2, grid=(n_blk,),
            in_specs=[idx_cur, idx_nxt, idx_cur, hbm, wspec, bspec, wspec, bspec, wspec, bspec],
            out_specs=hbm,
            scratch_shapes=[pltpu.VMEM((2, EBLK, D), f32), pltpu.VMEM((2, EBLK, D), f32),
                            pltpu.SemaphoreType.DMA((2,)), pltpu.SemaphoreType.DMA((2,))]),
        out_shape=jax.ShapeDtypeStruct((n_slots, D), f32),
        compiler_params=_cparams(("arbitrary",)),
    )(blk_e, nvalid, tok_rows, tok_rows, slot_rows, x2, wg, bg, wu, bu, wd, bd)


def _combine_kernel(ys_ref, h_ref, w_ref, g_ref, o_ref):
    w = w_ref[...]
    acc = h_ref[...]
    for kk in range(TOPK):
        acc = acc + w[:, kk:kk + 1] * ys_ref[kk]
    o_ref[...] = acc * lax.rsqrt(jnp.mean(acc * acc, axis=-1, keepdims=True) + NORM_EPS) * g_ref[...]


def _combine(ys, hres, gates_t, lnf, tt):
    t = hres.shape[0]
    return pl.pallas_call(
        _combine_kernel,
        name="combine",
        grid=(t // tt,),
        in_specs=[pl.BlockSpec((TOPK, tt, D), lambda i: (0, i, 0)),
                  pl.BlockSpec((tt, D), lambda i: (i, 0)),
                  pl.BlockSpec((tt, TOPK), lambda i: (i, 0)),
                  pl.BlockSpec((1, D), lambda i: (0, 0))],
        out_specs=pl.BlockSpec((tt, D), lambda i: (i, 0)),
        out_shape=jax.ShapeDtypeStruct((t, D), f32),
        compiler_params=_cparams(("parallel",)),
    )(ys, hres, gates_t, lnf)


def _row_tile(n, want):
    t = min(n, want)
    assert n % t == 0
    return t


def _mixer_inputs(x2d, seq_len, init, p):
    n = x2d.shape[0]
    rkv, cv, gt, lo = _inproj(x2d, p["ln1"], p["w_in"], _row_tile(n, 512))
    outs = _mix(rkv, cv, lo, init, p["vec_mix"], p["ww2"], p["wa2"], p["wg2"], p["cw"], p["seg"],
                seq_len, _row_tile(seq_len, 256))
    return (rkv, cv, lo), gt, outs


def kernel(x, meta_tokens, ln1_g, w_in, mu_r, mu_k, mu_v, mu_w, mu_a, mu_g, w0, w_w1, w_w2, a0, w_a1, w_a2, w_g1, w_g2, k_k, k_a, r_k, lnx_g, lnx_b, w_o_rwkv, conv_w, w_o_conv, w_o, ln2_g, w_router, b_router, w_e_gate, b_e_gate, w_e_up, b_e_up, w_e_down, b_e_down, lnf_g):
    nb, seq, _ = x.shape
    t = nb * seq
    assert ln1_g.shape[0] == 1, "single layer"

    muw, mua, mug = mu_w[0][:, None], mu_a[0][:, None], mu_g[0][:, None]
    lora_cur = jnp.concatenate([(1 - muw) * w_w1[0], (1 - mua) * w_a1[0], (1 - mug) * w_g1[0]], axis=1)
    lora_prev = jnp.concatenate([muw * w_w1[0], mua * w_a1[0], mug * w_g1[0]], axis=1)
    zrow = jnp.zeros((1, R), f32)
    p = {
        "ln1": ln1_g[0][None, :],
        "w_in": jnp.concatenate([w_in[0], lora_cur, lora_prev], axis=1).astype(bf16),
        "vec_mix": jnp.concatenate([mu_r, mu_k, mu_v, w0, a0, k_k, k_a, zrow], axis=0),
        "ww2": w_w2[0].astype(bf16), "wa2": w_a2[0].astype(bf16), "wg2": w_g2[0].astype(bf16),
        "cw": jnp.concatenate([conv_w[0], jnp.zeros((5, R), f32)], axis=0),
        "seg": (jnp.arange(R)[:, None] // HS == jnp.arange(R)[None, :] // HS).astype(bf16),
    }
    vec_merge = jnp.concatenate([lnx_g, lnx_b, r_k, jnp.zeros((5, R), f32)], axis=0)

    zero_init = (jnp.zeros((HALO, 3 * R), bf16), jnp.zeros((HALO, 3 * R), bf16),
                 jnp.zeros((HALO, 2 * LORA), bf16))
    meta_proj, _, meta_ops = _mixer_inputs(meta_tokens.astype(f32), NMETA, zero_init, p)
    pad = lambda z: jnp.pad(z, ((CHUNK - NMETA, 0), (0, 0)))[None]
    r_m, k_m, v_m, kk_m, b_m, lw_m = (pad(z) for z in meta_ops[:6])
    _, s_meta = _wkv(r_m, k_m, v_m, kk_m, b_m, lw_m, jnp.zeros((NH, HS, HS), f32))

    x2d = x.reshape(t, D)
    _, gt, ops = _mixer_inputs(x2d, seq, meta_proj, p)
    r, k, v, kk, b, lw, g, cb = ops
    as3 = lambda z: z.reshape(nb, seq, R)
    y, _ = _wkv(as3(r), as3(k), as3(v), as3(kk), as3(b), as3(lw), s_meta[0])
    hres, x2, logits_t = _merge(
        y.reshape(t, R), r, k, v, g, cb, gt, x2d, vec_merge, ln2_g[0][None, :], p["seg"],
        w_o_rwkv[0].astype(bf16), w_o_conv[0].astype(bf16), w_o[0].astype(bf16),
        w_router[0].T, b_router[0][:, None], _row_tile(t, 256))

    top_e, gates, rank, counts = _route(logits_t, _row_tile(t, 512))
    counts = counts[:, 0]
    padded = ((counts + EBLK - 1) // EBLK) * EBLK
    pend = jnp.cumsum(padded)
    pstart = pend - padded
    eids = jnp.arange(NE, dtype=jnp.int32)[:, None, None]
    dest = rank + jnp.sum(jnp.where(top_e[None] == eids, pstart[:, None, None], 0), axis=0)
    n_rows = t * TOPK + NE * EBLK
    n_blk = n_rows // EBLK
    blk_id = jnp.arange(n_blk, dtype=jnp.int32)
    blk_e = jnp.minimum(jnp.sum(pend[None, :] <= blk_id[:, None] * EBLK, axis=1), NE - 1).astype(jnp.int32)
    nvalid = jnp.clip(counts[blk_e] - (blk_id * EBLK - pstart[blk_e]), 0, EBLK)
    nvalid = jnp.where(blk_id * EBLK < pend[NE - 1], nvalid, 0).astype(jnp.int32)
    slot_rows = jnp.zeros((n_rows,), jnp.int32).at[dest.reshape(-1)].set(
        jnp.arange(t * TOPK, dtype=jnp.int32), unique_indices=True)
    tok_rows = slot_rows % t
    ys = _experts(blk_e, nvalid, tok_rows.reshape(n_blk, 1, EBLK), slot_rows.reshape(n_blk, 1, EBLK),
                  x2, t * TOPK,
                  w_e_gate[0].astype(bf16), b_e_gate[0][:, None, :],
                  w_e_up[0].astype(bf16), b_e_up[0][:, None, :],
                  w_e_down[0].astype(bf16), b_e_down[0][:, None, :])
    out = _combine(ys.reshape(TOPK, t, D), hres, gates.T, lnf_g[None, :], _row_tile(t, 256))
    return out.reshape(nb, seq, D)
```

```python
import functools

import jax
import jax.numpy as jnp
from jax import lax
from jax.experimental import pallas as pl
from jax.experimental.pallas import tpu as pltpu

D = 1024
R = 512
NH = 8
HS = 64
NE = 32
TOPK = 4
NMETA = 16
CHUNK = 64
WKV_SEQS = 4
EBLK = 512
NORM_EPS = 1e-5
LNX_EPS = 64e-5
ALPHA = 1.702
LIMIT = 7.0
LORA_W, LORA_A, LORA_G = 64, 64, 128
LORA = LORA_W + LORA_A + LORA_G
NPROJ = 3 * R + 3 * R + 2 * D + 2 * LORA
VMEM_LIMIT = 56 * 1024 * 1024

f32 = jnp.float32
bf16 = jnp.bfloat16


def _bdot(a, b):
    return jnp.dot(a.astype(bf16), b.astype(bf16), preferred_element_type=f32)


def _bdot_nt(a, b):
    return lax.dot_general(a.astype(bf16), b.astype(bf16), (((1,), (1,)), ((), ())),
                           preferred_element_type=f32)


def _bdot_tn(a, b):
    return lax.dot_general(a.astype(bf16), b.astype(bf16), (((0,), (0,)), ((), ())),
                           preferred_element_type=f32)


def _split3(x):
    h = x.astype(bf16)
    r1 = x - h.astype(f32)
    m = r1.astype(bf16)
    l = (r1 - m.astype(f32)).astype(bf16)
    return h, m, l


def _head_sum(x, seg):
    h = x.astype(bf16)
    l = (x - h.astype(f32)).astype(bf16)
    dot = functools.partial(jnp.dot, preferred_element_type=f32)
    return dot(h, seg) + dot(l, seg)


def _cparams(sem):
    return pltpu.CompilerParams(dimension_semantics=sem, vmem_limit_bytes=VMEM_LIMIT)


def _inproj_kernel(x_ref, g_ref, w_ref, rkv_ref, cv_ref, gt_ref, lo_ref):
    x = x_ref[...]
    xn = x * lax.rsqrt(jnp.mean(x * x, axis=-1, keepdims=True) + NORM_EPS) * g_ref[...]
    xb = xn.astype(bf16)
    col = 0
    for ref in (rkv_ref, cv_ref, gt_ref, lo_ref):
        width = ref.shape[-1]
        for c in range(0, width, 512):
            ref[:, c:c + 512] = jnp.dot(
                xb, w_ref[:, col + c:col + c + 512], preferred_element_type=f32
            ).astype(ref.dtype)
        col += width


def _inproj(x2d, g, w, tm):
    t = x2d.shape[0]
    widths = (3 * R, 3 * R, 2 * D, 2 * LORA)
    return pl.pallas_call(
        _inproj_kernel,
        name="inproj",
        grid=(t // tm,),
        in_specs=[pl.BlockSpec((tm, D), lambda i: (i, 0)),
                  pl.BlockSpec((1, D), lambda i: (0, 0)),
                  pl.BlockSpec((D, NPROJ), lambda i: (0, 0))],
        out_specs=[pl.BlockSpec((tm, n), lambda i: (i, 0)) for n in widths],
        out_shape=[jax.ShapeDtypeStruct((t, n), bf16) for n in widths],
        compiler_params=_cparams(("parallel",)),
    )(x2d, g, w)


HALO = 16


def _mix_kernel(tiles_per_seq, rkv_ref, cv_ref, lo_ref, rkv_h_ref, cv_h_ref, lo_h_ref,
                rkv_i_ref, cv_i_ref, lo_i_ref, vec_ref, ww2_ref, wa2_ref, wg2_ref, cw_ref,
                seg_ref, r_out, k_out, v_out, kk_out, b_out, lw_out, g_out, cb_out):
    first = (pl.program_id(0) % tiles_per_seq) == 0
    tm = rkv_ref.shape[0]
    row = lax.broadcasted_iota(jnp.int32, (tm, 1), 0)

    def history(h_ref, i_ref):
        return jnp.where(first, i_ref[...].astype(f32), h_ref[...].astype(f32))

    def shifted(cur, hist, n):
        out = pltpu.roll(cur, n, axis=0)
        for j in range(n):
            out = jnp.where(row == j, hist[HALO - n + j:HALO - n + j + 1, :], out)
        return out

    vec = vec_ref[...]
    mu_r, mu_k, mu_v = vec[0:1, :], vec[1:2, :], vec[2:3, :]
    w0, a0, k_k, k_a = vec[3:4, :], vec[4:5, :], vec[5:6, :], vec[6:7, :]

    rkv = rkv_ref[...].astype(f32)
    rkv_prev = shifted(rkv, history(rkv_h_ref, rkv_i_ref), 1)
    pr, pk, pv = rkv[:, :R], rkv[:, R:2 * R], rkv[:, 2 * R:]
    r = pr + (rkv_prev[:, :R] - pr) * mu_r
    k = pk + (rkv_prev[:, R:2 * R] - pk) * mu_k
    v = pv + (rkv_prev[:, 2 * R:] - pv) * mu_v

    lo = lo_ref[...].astype(f32)
    lo_prev = shifted(lo, history(lo_h_ref, lo_i_ref), 1)
    mixed = lo[:, :LORA] + lo_prev[:, LORA:]
    hw = jnp.tanh(mixed[:, :LORA_W])
    ha = mixed[:, LORA_W:LORA_W + LORA_A]
    hg = jax.nn.sigmoid(mixed[:, LORA_W + LORA_A:])
    w_log = -jax.nn.softplus(-(w0 + _bdot(hw, ww2_ref[...]))) - 0.5
    a = jax.nn.sigmoid(a0 + _bdot(ha, wa2_ref[...]))
    g = _bdot(hg, wg2_ref[...])

    kk = k * k_k
    norm = jnp.sqrt(_head_sum(kk * kk, seg_ref[...]))
    kk = kk / jnp.maximum(norm, 1e-12)
    k = k * (1.0 + (a - 1.0) * k_a)

    r_out[...] = r.astype(r_out.dtype)
    k_out[...] = k.astype(k_out.dtype)
    v_out[...] = v.astype(v_out.dtype)
    kk_out[...] = kk.astype(kk_out.dtype)
    b_out[...] = (kk * a).astype(b_out.dtype)
    lw_out[...] = -jnp.exp(w_log)
    g_out[...] = g.astype(g_out.dtype)

    cv = cv_ref[...].astype(f32)
    hist = history(cv_h_ref, cv_i_ref)
    u = cv[:, R:2 * R] * cv[:, 2 * R:]
    uh = hist[:, R:2 * R] * hist[:, 2 * R:]
    cw = cw_ref[...]
    conv = cw[2:3, :] * u + cw[1:2, :] * shifted(u, uh, 1) + cw[0:1, :] * shifted(u, uh, 2)
    cb_out[...] = (cv[:, :R] * conv).astype(cb_out.dtype)


def _mix(rkv, cv, lo, init, vec, ww2, wa2, wg2, cw, seg, seq_len, tm):
    t = rkv.shape[0]
    tiles_per_seq = seq_len // tm
    hb = tm // HALO

    def cur(n):
        return pl.BlockSpec((tm, n), lambda i: (i, 0))

    def halo(n):
        return pl.BlockSpec((HALO, n), lambda i: (jnp.maximum(i * hb - 1, 0), 0))

    def whole(shape):
        return pl.BlockSpec(shape, lambda i: (0,) * len(shape))

    out_spec = pl.BlockSpec((tm, R), lambda i: (i, 0))
    return pl.pallas_call(
        functools.partial(_mix_kernel, tiles_per_seq),
        name="mix",
        grid=(t // tm,),
        in_specs=[cur(3 * R), cur(3 * R), cur(2 * LORA),
                  halo(3 * R), halo(3 * R), halo(2 * LORA),
                  whole((HALO, 3 * R)), whole((HALO, 3 * R)), whole((HALO, 2 * LORA)),
                  whole((8, R)), whole((LORA_W, R)), whole((LORA_A, R)), whole((LORA_G, R)),
                  whole((8, R)), whole((R, R))],
        out_specs=[out_spec] * 8,
        out_shape=[jax.ShapeDtypeStruct((t, R), f32 if n == 5 else bf16) for n in range(8)],
        compiler_params=_cparams(("parallel",)),
    )(rkv, cv, lo, rkv, cv, lo, *init, vec, ww2, wa2, wg2, cw, seg)


def _wkv_kernel(r_ref, k_ref, v_ref, kk_ref, b_ref, lw_ref, s0_ref, y_ref, sT_ref, s_scr):
    c = pl.program_id(1)
    nbw = r_ref.shape[0]

    @pl.when(c == 0)
    def _():
        for j in range(nbw):
            s_scr[j] = s0_ref[...]

    C = CHUNK
    ri = lax.broadcasted_iota(jnp.int32, (C, C), 0)
    ci = lax.broadcasted_iota(jnp.int32, (C, C), 1)
    incl = ri >= ci
    strict = ri > ci
    ri2 = lax.broadcasted_iota(jnp.int32, (C, 2 * C), 0)
    ci2 = lax.broadcasted_iota(jnp.int32, (C, 2 * C), 1)
    incl2 = ri2 >= jnp.where(ci2 >= C, ci2 - C, ci2)
    tri = jnp.where(incl, 1.0, 0.0).astype(bf16)
    eye = jnp.where(ri == ci, 1.0, 0.0).astype(f32)
    dot = functools.partial(jnp.dot, preferred_element_type=f32)

    at, rt, bt, kt, bh, kh, vv, g_tot = [], [], [], [], [], [], [], []
    for j in range(nbw):
        lw = lw_ref[j]
        h, m, l = _split3(lw)
        cum = dot(tri, h) + dot(tri, m) + dot(tri, l)
        tot = cum[C - 1:C, :]
        k = k_ref[j].astype(f32)
        b = b_ref[j].astype(f32)
        g_inv = jnp.exp(-cum)
        tail = jnp.exp(tot - cum)
        rt.append(r_ref[j].astype(f32) * jnp.exp(cum))
        kt.append(k * g_inv)
        bt.append(b * g_inv)
        at.append(-kk_ref[j].astype(f32) * jnp.exp(cum - lw))
        bh.append(b * tail)
        kh.append(k * tail)
        vv.append(v_ref[j].astype(f32))
        g_tot.append(jnp.exp(tot))

    chains = [(j, hd) for j in range(nbw) for hd in range(NH)]
    cs = range(len(chains))
    sl = lambda hd: slice(hd * HS, (hd + 1) * HS)
    s_old = [s_scr[j, hd] for j, hd in chains]
    lhs = [jnp.concatenate([at[j][:, sl(hd)], rt[j][:, sl(hd)]], axis=0).astype(bf16)
           for j, hd in chains]
    rhs = [jnp.concatenate([bt[j][:, sl(hd)], kt[j][:, sl(hd)]], axis=0).astype(bf16)
           for j, hd in chains]
    amat = [_bdot_nt(lhs[n], rhs[n]) for n in cs]
    a_ab = [jnp.where(strict, amat[n][:C, :C], 0.0) for n in cs]
    a_ak = [jnp.where(strict, amat[n][:C, C:], 0.0) for n in cs]
    a_r = [jnp.where(incl2, amat[n][C:, :], 0.0) for n in cs]
    x = [eye + a_ab[n] for n in cs]
    p = a_ab
    for _ in range(5):
        pb = [p[n].astype(bf16) for n in cs]
        p = [dot(pb[n], pb[n]) for n in cs]
        x = [x[n] + _bdot(x[n], p[n]) for n in cs]
    vh = [vv[j][:, sl(hd)] for j, hd in chains]
    sh = [_bdot_nt(lhs[n], s_old[n]) for n in cs]
    av = [_bdot(a_ak[n], vh[n]) for n in cs]
    u = [_bdot(x[n], sh[n][:C] + av[n]) for n in cs]
    uv = [jnp.concatenate([u[n], vh[n]], axis=0).astype(bf16) for n in cs]
    ys = [sh[n][C:] + _bdot(a_r[n], uv[n]) for n in cs]
    bk = [jnp.concatenate([bh[j][:, sl(hd)], kh[j][:, sl(hd)]], axis=0) for j, hd in chains]
    s_new = [s_old[n] * g_tot[j][:, sl(hd)] + _bdot_tn(uv[n], bk[n])
             for n, (j, hd) in enumerate(chains)]
    for n, (j, hd) in enumerate(chains):
        s_scr[j, hd] = s_new[n]
    for j in range(nbw):
        y_ref[j] = jnp.concatenate(ys[j * NH:(j + 1) * NH], axis=1)

    @pl.when(c == pl.num_programs(1) - 1)
    def _():
        sT_ref[...] = s_scr[...]


def _wkv(r, k, v, kk, b, lw, s0, nbw):
    nb, seq, _ = r.shape
    assert nb % nbw == 0 and seq % CHUNK == 0
    blk = pl.BlockSpec((nbw, CHUNK, R), lambda i, c: (i, c, 0))
    return pl.pallas_call(
        _wkv_kernel,
        name="wkv",
        grid=(nb // nbw, seq // CHUNK),
        in_specs=[blk] * 6 + [pl.BlockSpec((NH, HS, HS), lambda i, c: (0, 0, 0))],
        out_specs=[blk, pl.BlockSpec((nbw, NH, HS, HS), lambda i, c: (i, 0, 0, 0))],
        out_shape=[jax.ShapeDtypeStruct((nb, seq, R), f32),
                   jax.ShapeDtypeStruct((nb, NH, HS, HS), f32)],
        scratch_shapes=[pltpu.VMEM((nbw, NH, HS, HS), f32)],
        compiler_params=_cparams(("parallel", "arbitrary")),
    )(r, k, v, kk, b, lw, s0)


def _merge_kernel(y_ref, r_ref, k_ref, v_ref, g_ref, cb_ref, gt_ref, x_ref, vec_ref, ln2_ref,
                  seg_ref, worw_ref, woc_ref, wo_ref, wr_ref, br_ref,
                  h_out, x2_out, lg_out):
    seg = seg_ref[...]
    vec = vec_ref[...]
    lnx_g, lnx_b, r_k = vec[0:1, :], vec[1:2, :], vec[2:3, :]
    y = y_ref[...]
    mean = _head_sum(y, seg) * (1.0 / HS)
    yc = y - mean
    var = _head_sum(yc * yc, seg) * (1.0 / HS)
    yn = yc * lax.rsqrt(var + LNX_EPS) * lnx_g + lnx_b
    v = v_ref[...].astype(f32)
    bonus = _head_sum(r_ref[...].astype(f32) * k_ref[...].astype(f32) * r_k, seg) * v
    y_a = _bdot((yn + bonus) * g_ref[...].astype(f32), worw_ref[...])
    y_b = _bdot(cb_ref[...], woc_ref[...])
    gt = gt_ref[...].astype(f32)
    merged = jax.nn.sigmoid(gt[:, :D]) * y_a + jax.nn.sigmoid(gt[:, D:]) * y_b
    hres = x_ref[...] + _bdot(merged, wo_ref[...])
    h_out[...] = hres
    x2 = hres * lax.rsqrt(jnp.mean(hres * hres, axis=-1, keepdims=True) + NORM_EPS) * ln2_ref[...]
    x2_out[...] = x2
    xh, xm, _ = _split3(x2)
    wh, wm, _ = _split3(wr_ref[...])
    nt = functools.partial(lax.dot_general, dimension_numbers=(((1,), (1,)), ((), ())),
                           preferred_element_type=f32)
    lg_out[...] = nt(wh, xh) + nt(wh, xm) + nt(wm, xh) + br_ref[...]


def _merge(y, r, k, v, g, cb, gt, x2d, vec, ln2, seg, worw, woc, wo, wr_t, br, tm):
    t = y.shape[0]

    def rows(n):
        return pl.BlockSpec((tm, n), lambda i: (i, 0))

    def whole(shape):
        return pl.BlockSpec(shape, lambda i: (0,) * len(shape))

    return pl.pallas_call(
        _merge_kernel,
        name="merge",
        grid=(t // tm,),
        in_specs=[rows(R)] * 6 + [rows(2 * D), rows(D), whole((8, R)), whole((1, D)),
                                   whole((R, R)), whole((R, D)), whole((R, D)), whole((D, D)),
                                   whole((NE, D)), whole((NE, 1))],
        out_specs=[rows(D), rows(D), pl.BlockSpec((NE, tm), lambda i: (0, i))],
        out_shape=[jax.ShapeDtypeStruct((t, D), f32), jax.ShapeDtypeStruct((t, D), f32),
                   jax.ShapeDtypeStruct((NE, t), f32)],
        compiler_params=_cparams(("parallel",)),
    )(y, r, k, v, g, cb, gt, x2d, vec, ln2, seg, worw, woc, wo, wr_t, br)


def _route_kernel(lg_ref, e_out, w_out, rank_out, cnt_out, cnt_scr):
    i = pl.program_id(0)

    @pl.when(i == 0)
    def _():
        cnt_scr[...] = jnp.zeros_like(cnt_scr)

    lg = lg_ref[...]
    tr = lg.shape[1]
    erow = lax.broadcasted_iota(jnp.int32, lg.shape, 0)
    work = lg
    hits, vals, idxs = [], [], []
    for _ in range(TOPK):
        m = jnp.max(work, axis=0, keepdims=True)
        idx = jnp.min(jnp.where(work == m, erow, NE), axis=0, keepdims=True)
        hit = erow == idx
        hits.append(hit)
        vals.append(m)
        idxs.append(idx)
        work = jnp.where(hit, -jnp.inf, work)
    ex = [jnp.exp(vk - vals[0]) for vk in vals]
    den = ex[0] + ex[1] + ex[2] + ex[3]
    multi = jnp.where(hits[0] | hits[1] | hits[2] | hits[3], 1.0, 0.0)
    ti = lax.broadcasted_iota(jnp.int32, (tr, tr), 0)
    tj = lax.broadcasted_iota(jnp.int32, (tr, tr), 1)
    before = jnp.where(ti < tj, 1.0, 0.0).astype(bf16)
    base = cnt_scr[...]
    excl = jnp.dot(multi.astype(bf16), before, preferred_element_type=f32) + base
    for kk in range(TOPK):
        e_out[kk:kk + 1, :] = idxs[kk]
        w_out[kk:kk + 1, :] = ex[kk] / den
        rank_out[kk:kk + 1, :] = jnp.sum(jnp.where(hits[kk], excl, 0.0), axis=0,
                                         keepdims=True).astype(jnp.int32)
    total = base + jnp.sum(multi, axis=1, keepdims=True)
    cnt_scr[...] = total
    cnt_out[...] = total.astype(jnp.int32)


def _route(logits_t, tr):
    t = logits_t.shape[1]
    sel = pl.BlockSpec((TOPK, tr), lambda i: (0, i))
    return pl.pallas_call(
        _route_kernel,
        name="route",
        grid=(t // tr,),
        in_specs=[pl.BlockSpec((NE, tr), lambda i: (0, i))],
        out_specs=[sel, sel, sel, pl.BlockSpec((NE, 1), lambda i: (0, 0))],
        out_shape=[jax.ShapeDtypeStruct((TOPK, t), jnp.int32),
                   jax.ShapeDtypeStruct((TOPK, t), f32),
                   jax.ShapeDtypeStruct((TOPK, t), jnp.int32),
                   jax.ShapeDtypeStruct((NE, 1), jnp.int32)],
        scratch_shapes=[pltpu.VMEM((NE, 1), f32)],
        compiler_params=_cparams(("arbitrary",)),
    )(logits_t)


def _dispatch_kernel(dest_ref, tail_ref, x2_ref, xs_ref, zbuf, sem, zsem):
    tt = x2_ref.shape[0]

    @pl.when(pl.program_id(0) == 0)
    def _():
        zbuf[...] = jnp.zeros_like(zbuf)

        def zero_copy(e):
            start = pl.multiple_of(tail_ref[e], EBLK)
            return pltpu.make_async_copy(zbuf, xs_ref.at[pl.ds(start, EBLK)], zsem)

        for e in range(NE):
            @pl.when(tail_ref[e] >= 0)
            def _():
                zero_copy(e).start()
        for e in range(NE):
            @pl.when(tail_ref[e] >= 0)
            def _():
                zero_copy(e).wait()

        def unused_copy(blk):
            start = pl.multiple_of(blk * EBLK, EBLK)
            return pltpu.make_async_copy(zbuf, xs_ref.at[pl.ds(start, EBLK)], zsem)

        n_blk = xs_ref.shape[0] // EBLK
        lax.fori_loop(tail_ref[NE], n_blk, lambda blk, c: (unused_copy(blk).start(), c)[1], 0)
        lax.fori_loop(tail_ref[NE], n_blk, lambda blk, c: (unused_copy(blk).wait(), c)[1], 0)

    def issue(rw, carry):
        for kk in range(TOPK):
            pltpu.make_async_copy(x2_ref.at[pl.ds(rw, 1)],
                                  xs_ref.at[pl.ds(dest_ref[kk, rw], 1)], sem).start()
        return carry

    lax.fori_loop(0, tt, issue, 0, unroll=8)
    for kk in range(TOPK):
        pltpu.make_async_copy(x2_ref, xs_ref.at[pl.ds(0, tt)], sem).wait()


def _dispatch(dest, tail, x2, n_rows, tt):
    t = x2.shape[0]
    return pl.pallas_call(
        _dispatch_kernel,
        name="dispatch",
        grid=(t // tt,),
        in_specs=[pl.BlockSpec((TOPK, tt), lambda i: (0, i), memory_space=pltpu.SMEM),
                  pl.BlockSpec(memory_space=pltpu.SMEM),
                  pl.BlockSpec((tt, D), lambda i: (i, 0))],
        out_specs=pl.BlockSpec(memory_space=pl.ANY),
        out_shape=jax.ShapeDtypeStruct((n_rows, D), f32),
        scratch_shapes=[pltpu.VMEM((EBLK, D), f32), pltpu.SemaphoreType.DMA(()),
                        pltpu.SemaphoreType.DMA(())],
        compiler_params=_cparams(("arbitrary",)),
    )(dest, tail, x2)


def _expert_kernel(be_ref, nu_ref, fx_ref, xs_ref, wg_ref, bg_ref, wu_ref, bu_ref, wd_ref, bd_ref,
                   ys_ref, wgb, wub, wdb):
    del be_ref
    i = pl.program_id(0)

    @pl.when(fx_ref[i] == 1)
    def _():
        wgb[...] = wg_ref[...].astype(bf16)
        wub[...] = wu_ref[...].astype(bf16)
        wdb[...] = wd_ref[...].astype(bf16)

    @pl.when(i < nu_ref[0])
    def _():
        xb = xs_ref[...].astype(bf16)
        gt = jnp.minimum(jnp.dot(xb, wgb[...], preferred_element_type=f32) + bg_ref[...], LIMIT)
        up = jnp.clip(jnp.dot(xb, wub[...], preferred_element_type=f32) + bu_ref[...],
                      -LIMIT, LIMIT)
        act = (up + 1.0) * (gt * jax.nn.sigmoid(gt * ALPHA))
        ys_ref[...] = jnp.dot(act.astype(bf16), wdb[...], preferred_element_type=f32) + bd_ref[...]

    @pl.when(i >= nu_ref[0])
    def _():
        ys_ref[...] = jnp.zeros_like(ys_ref)


def _experts(blk_e, n_used, first, xs, wg, bg, wu, bu, wd, bd):
    p = xs.shape[0]
    wspec = pl.BlockSpec((None, D, D), lambda i, be, nu, fx: (be[i], 0, 0))
    bspec = pl.BlockSpec((None, 1, D), lambda i, be, nu, fx: (be[i], 0, 0))
    xrows = pl.BlockSpec((EBLK, D), lambda i, be, nu, fx: (jnp.where(i < nu[0], i, 0), 0))
    yrows = pl.BlockSpec((EBLK, D), lambda i, be, nu, fx: (i, 0))
    return pl.pallas_call(
        _expert_kernel,
        name="experts",
        grid_spec=pltpu.PrefetchScalarGridSpec(
            num_scalar_prefetch=3, grid=(p // EBLK,),
            in_specs=[xrows, wspec, bspec, wspec, bspec, wspec, bspec],
            out_specs=yrows,
            scratch_shapes=[pltpu.VMEM((D, D), bf16)] * 3),
        out_shape=jax.ShapeDtypeStruct((p, D), f32),
        compiler_params=_cparams(("arbitrary",)),
    )(blk_e, n_used, first, xs, wg, bg, wu, bu, wd, bd)


def _combine_kernel(dc_ref, dn_ref, ys_ref, h_ref, w_ref, g_ref, o_ref, buf, sem):
    i = pl.program_id(0)
    tt = h_ref.shape[0]
    cur = i % 2

    def issue(d_ref, b):
        def body(rw, carry):
            for kk in range(TOPK):
                pltpu.make_async_copy(ys_ref.at[pl.ds(d_ref[kk, rw], 1)],
                                      buf.at[b, kk, pl.ds(rw, 1)], sem.at[b]).start()
            return carry
        lax.fori_loop(0, tt, body, 0, unroll=8)

    @pl.when(i == 0)
    def _():
        issue(dc_ref, 0)

    @pl.when(i + 1 < pl.num_programs(0))
    def _():
        issue(dn_ref, 1 - cur)

    for kk in range(TOPK):
        pltpu.make_async_copy(ys_ref.at[pl.ds(0, tt)], buf.at[cur, kk], sem.at[cur]).wait()

    w = w_ref[...]
    acc = h_ref[...]
    for kk in range(TOPK):
        acc = acc + w[:, kk:kk + 1] * buf[cur, kk]
    o_ref[...] = acc * lax.rsqrt(jnp.mean(acc * acc, axis=-1, keepdims=True) + NORM_EPS) * g_ref[...]


def _combine(dest, ys, hres, gates_t, lnf, tt):
    t = hres.shape[0]
    n = t // tt
    return pl.pallas_call(
        _combine_kernel,
        name="combine",
        grid=(n,),
        in_specs=[pl.BlockSpec((TOPK, tt), lambda i: (0, i), memory_space=pltpu.SMEM),
                  pl.BlockSpec((TOPK, tt), lambda i: (0, jnp.minimum(i + 1, n - 1)),
                               memory_space=pltpu.SMEM),
                  pl.BlockSpec(memory_space=pl.ANY),
                  pl.BlockSpec((tt, D), lambda i: (i, 0)),
                  pl.BlockSpec((tt, TOPK), lambda i: (i, 0)),
                  pl.BlockSpec((1, D), lambda i: (0, 0))],
        out_specs=pl.BlockSpec((tt, D), lambda i: (i, 0)),
        out_shape=jax.ShapeDtypeStruct((t, D), f32),
        scratch_shapes=[pltpu.VMEM((2, TOPK, tt, D), f32), pltpu.SemaphoreType.DMA((2,))],
        compiler_params=_cparams(("arbitrary",)),
    )(dest, dest, ys, hres, gates_t, lnf)


def _row_tile(n, want):
    t = min(n, want)
    assert n % t == 0
    return t


def _mixer_inputs(x2d, seq_len, init, p):
    n = x2d.shape[0]
    rkv, cv, gt, lo = _inproj(x2d, p["ln1"], p["w_in"], _row_tile(n, 512))
    outs = _mix(rkv, cv, lo, init, p["vec_mix"], p["ww2"], p["wa2"], p["wg2"], p["cw"], p["seg"],
                seq_len, _row_tile(seq_len, 256))
    return (rkv, cv, lo), gt, outs


def kernel(x, meta_tokens, ln1_g, w_in, mu_r, mu_k, mu_v, mu_w, mu_a, mu_g, w0, w_w1, w_w2, a0, w_a1, w_a2, w_g1, w_g2, k_k, k_a, r_k, lnx_g, lnx_b, w_o_rwkv, conv_w, w_o_conv, w_o, ln2_g, w_router, b_router, w_e_gate, b_e_gate, w_e_up, b_e_up, w_e_down, b_e_down, lnf_g):
    nb, seq, _ = x.shape
    t = nb * seq
    assert ln1_g.shape[0] == 1, "single layer"

    muw, mua, mug = mu_w[0][:, None], mu_a[0][:, None], mu_g[0][:, None]
    lora_cur = jnp.concatenate([(1 - muw) * w_w1[0], (1 - mua) * w_a1[0], (1 - mug) * w_g1[0]], axis=1)
    lora_prev = jnp.concatenate([muw * w_w1[0], mua * w_a1[0], mug * w_g1[0]], axis=1)
    zrow = jnp.zeros((1, R), f32)
    p = {
        "ln1": ln1_g[0][None, :],
        "w_in": jnp.concatenate([w_in[0], lora_cur, lora_prev], axis=1).astype(bf16),
        "vec_mix": jnp.concatenate([mu_r, mu_k, mu_v, w0, a0, k_k, k_a, zrow], axis=0),
        "ww2": w_w2[0].astype(bf16), "wa2": w_a2[0].astype(bf16), "wg2": w_g2[0].astype(bf16),
        "cw": jnp.concatenate([conv_w[0], jnp.zeros((5, R), f32)], axis=0),
        "seg": (jnp.arange(R)[:, None] // HS == jnp.arange(R)[None, :] // HS).astype(bf16),
    }
    vec_merge = jnp.concatenate([lnx_g, lnx_b, r_k, jnp.zeros((5, R), f32)], axis=0)

    zero_init = (jnp.zeros((HALO, 3 * R), bf16), jnp.zeros((HALO, 3 * R), bf16),
                 jnp.zeros((HALO, 2 * LORA), bf16))
    meta_proj, _, meta_ops = _mixer_inputs(meta_tokens.astype(f32), NMETA, zero_init, p)
    pad = lambda z: jnp.pad(z, ((CHUNK - NMETA, 0), (0, 0)))[None]
    r_m, k_m, v_m, kk_m, b_m, lw_m = (pad(z) for z in meta_ops[:6])
    _, s_meta = _wkv(r_m, k_m, v_m, kk_m, b_m, lw_m, jnp.zeros((NH, HS, HS), f32), 1)

    x2d = x.reshape(t, D)
    _, gt, ops = _mixer_inputs(x2d, seq, meta_proj, p)
    r, k, v, kk, b, lw, g, cb = ops
    as3 = lambda z: z.reshape(nb, seq, R)
    y, _ = _wkv(as3(r), as3(k), as3(v), as3(kk), as3(b), as3(lw), s_meta[0], WKV_SEQS)
    hres, x2, logits_t = _merge(
        y.reshape(t, R), r, k, v, g, cb, gt, x2d, vec_merge, ln2_g[0][None, :], p["seg"],
        w_o_rwkv[0].astype(bf16), w_o_conv[0].astype(bf16), w_o[0].astype(bf16),
        w_router[0].T, b_router[0][:, None], _row_tile(t, 256))

    top_e, gates, rank, counts = _route(logits_t, _row_tile(t, 512))
    counts = counts[:, 0]
    padded = ((counts + EBLK - 1) // EBLK) * EBLK
    pend = jnp.cumsum(padded)
    pstart = pend - padded
    eids = jnp.arange(NE, dtype=jnp.int32)[:, None, None]
    dest = rank + jnp.sum(jnp.where(top_e[None] == eids, pstart[:, None, None], 0), axis=0)
    n_rows = t * TOPK + NE * EBLK
    n_blk = n_rows // EBLK
    blk_start = jnp.arange(n_blk, dtype=jnp.int32) * EBLK
    blk_e = jnp.minimum(jnp.sum(pend[None, :] <= blk_start[:, None], axis=1), NE - 1).astype(jnp.int32)
    n_used = (pend[NE - 1:] // EBLK).astype(jnp.int32)
    tail = jnp.concatenate([jnp.where(padded > 0, pend - EBLK, -1), n_used]).astype(jnp.int32)
    xs = _dispatch(dest, tail, x2, n_rows, _row_tile(t, 256))
    first = jnp.concatenate([jnp.ones((1,), jnp.int32), (blk_e[1:] != blk_e[:-1]).astype(jnp.int32)])
    ys = _experts(blk_e, n_used, first, xs,
                  w_e_gate[0], b_e_gate[0][:, None, :],
                  w_e_up[0], b_e_up[0][:, None, :],
                  w_e_down[0], b_e_down[0][:, None, :])
    out = _combine(dest, ys, hres, gates.T, lnf_g[None, :], _row_tile(t, 128))
    return out.reshape(nb, seq, D)
```

```python
import functools

import jax
import jax.numpy as jnp
from jax import lax
from jax.experimental import pallas as pl
from jax.experimental.pallas import tpu as pltpu

D = 1024
R = 512
NH = 8
HS = 64
NE = 32
TOPK = 4
NMETA = 16
CHUNK = 64
WKV_SEQS = 4
EBLK = 512
TT = 512
ALIGN = 16
SB = 4 * TT + 512
SLABS = (512, 256, 128, 64, 32, 16)
NORM_EPS = 1e-5
LNX_EPS = 64e-5
ALPHA = 1.702
LIMIT = 7.0
LORA_W, LORA_A, LORA_G = 64, 64, 128
LORA = LORA_W + LORA_A + LORA_G
NPROJ = 3 * R + 3 * R + 2 * D + 2 * LORA
VMEM_LIMIT = 56 * 1024 * 1024

f32 = jnp.float32
bf16 = jnp.bfloat16


def _bdot(a, b):
    return jnp.dot(a.astype(bf16), b.astype(bf16), preferred_element_type=f32)


def _bdot_nt(a, b):
    return lax.dot_general(a.astype(bf16), b.astype(bf16), (((1,), (1,)), ((), ())),
                           preferred_element_type=f32)


def _bdot_tn(a, b):
    return lax.dot_general(a.astype(bf16), b.astype(bf16), (((0,), (0,)), ((), ())),
                           preferred_element_type=f32)


def _split3(x):
    h = x.astype(bf16)
    r1 = x - h.astype(f32)
    m = r1.astype(bf16)
    l = (r1 - m.astype(f32)).astype(bf16)
    return h, m, l


def _head_sum(x, seg):
    h = x.astype(bf16)
    l = (x - h.astype(f32)).astype(bf16)
    dot = functools.partial(jnp.dot, preferred_element_type=f32)
    return dot(h, seg) + dot(l, seg)


def _cparams(sem):
    return pltpu.CompilerParams(dimension_semantics=sem, vmem_limit_bytes=VMEM_LIMIT)


def _inproj_kernel(x_ref, g_ref, w_ref, rkv_ref, cv_ref, gt_ref, lo_ref):
    x = x_ref[...]
    xn = x * lax.rsqrt(jnp.mean(x * x, axis=-1, keepdims=True) + NORM_EPS) * g_ref[...]
    xb = xn.astype(bf16)
    col = 0
    for ref in (rkv_ref, cv_ref, gt_ref, lo_ref):
        width = ref.shape[-1]
        for c in range(0, width, 512):
            ref[:, c:c + 512] = jnp.dot(
                xb, w_ref[:, col + c:col + c + 512], preferred_element_type=f32
            ).astype(ref.dtype)
        col += width


def _inproj(x2d, g, w, tm):
    t = x2d.shape[0]
    widths = (3 * R, 3 * R, 2 * D, 2 * LORA)
    return pl.pallas_call(
        _inproj_kernel,
        name="inproj",
        grid=(t // tm,),
        in_specs=[pl.BlockSpec((tm, D), lambda i: (i, 0)),
                  pl.BlockSpec((1, D), lambda i: (0, 0)),
                  pl.BlockSpec((D, NPROJ), lambda i: (0, 0))],
        out_specs=[pl.BlockSpec((tm, n), lambda i: (i, 0)) for n in widths],
        out_shape=[jax.ShapeDtypeStruct((t, n), bf16) for n in widths],
        compiler_params=_cparams(("parallel",)),
    )(x2d, g, w)


HALO = 16


def _mix_kernel(tiles_per_seq, rkv_ref, cv_ref, lo_ref, rkv_h_ref, cv_h_ref, lo_h_ref,
                rkv_i_ref, cv_i_ref, lo_i_ref, vec_ref, ww2_ref, wa2_ref, wg2_ref, cw_ref,
                seg_ref, r_out, k_out, v_out, kk_out, b_out, lw_out, g_out, cb_out):
    first = (pl.program_id(0) % tiles_per_seq) == 0
    tm = rkv_ref.shape[0]
    row = lax.broadcasted_iota(jnp.int32, (tm, 1), 0)

    def history(h_ref, i_ref):
        return jnp.where(first, i_ref[...].astype(f32), h_ref[...].astype(f32))

    def shifted(cur, hist, n):
        out = pltpu.roll(cur, n, axis=0)
        for j in range(n):
            out = jnp.where(row == j, hist[HALO - n + j:HALO - n + j + 1, :], out)
        return out

    vec = vec_ref[...]
    mu_r, mu_k, mu_v = vec[0:1, :], vec[1:2, :], vec[2:3, :]
    w0, a0, k_k, k_a = vec[3:4, :], vec[4:5, :], vec[5:6, :], vec[6:7, :]

    rkv = rkv_ref[...].astype(f32)
    rkv_prev = shifted(rkv, history(rkv_h_ref, rkv_i_ref), 1)
    pr, pk, pv = rkv[:, :R], rkv[:, R:2 * R], rkv[:, 2 * R:]
    r = pr + (rkv_prev[:, :R] - pr) * mu_r
    k = pk + (rkv_prev[:, R:2 * R] - pk) * mu_k
    v = pv + (rkv_prev[:, 2 * R:] - pv) * mu_v

    lo = lo_ref[...].astype(f32)
    lo_prev = shifted(lo, history(lo_h_ref, lo_i_ref), 1)
    mixed = lo[:, :LORA] + lo_prev[:, LORA:]
    hw = jnp.tanh(mixed[:, :LORA_W])
    ha = mixed[:, LORA_W:LORA_W + LORA_A]
    hg = jax.nn.sigmoid(mixed[:, LORA_W + LORA_A:])
    w_log = -jax.nn.softplus(-(w0 + _bdot(hw, ww2_ref[...]))) - 0.5
    a = jax.nn.sigmoid(a0 + _bdot(ha, wa2_ref[...]))
    g = _bdot(hg, wg2_ref[...])

    kk = k * k_k
    norm = jnp.sqrt(_head_sum(kk * kk, seg_ref[...]))
    kk = kk / jnp.maximum(norm, 1e-12)
    k = k * (1.0 + (a - 1.0) * k_a)

    r_out[...] = r.astype(r_out.dtype)
    k_out[...] = k.astype(k_out.dtype)
    v_out[...] = v.astype(v_out.dtype)
    kk_out[...] = kk.astype(kk_out.dtype)
    b_out[...] = (kk * a).astype(b_out.dtype)
    lw_out[...] = -jnp.exp(w_log)
    g_out[...] = g.astype(g_out.dtype)

    cv = cv_ref[...].astype(f32)
    hist = history(cv_h_ref, cv_i_ref)
    u = cv[:, R:2 * R] * cv[:, 2 * R:]
    uh = hist[:, R:2 * R] * hist[:, 2 * R:]
    cw = cw_ref[...]
    conv = cw[2:3, :] * u + cw[1:2, :] * shifted(u, uh, 1) + cw[0:1, :] * shifted(u, uh, 2)
    cb_out[...] = (cv[:, :R] * conv).astype(cb_out.dtype)


def _mix(rkv, cv, lo, init, vec, ww2, wa2, wg2, cw, seg, seq_len, tm):
    t = rkv.shape[0]
    tiles_per_seq = seq_len // tm
    hb = tm // HALO

    def cur(n):
        return pl.BlockSpec((tm, n), lambda i: (i, 0))

    def halo(n):
        return pl.BlockSpec((HALO, n), lambda i: (jnp.maximum(i * hb - 1, 0), 0))

    def whole(shape):
        return pl.BlockSpec(shape, lambda i: (0,) * len(shape))

    out_spec = pl.BlockSpec((tm, R), lambda i: (i, 0))
    return pl.pallas_call(
        functools.partial(_mix_kernel, tiles_per_seq),
        name="mix",
        grid=(t // tm,),
        in_specs=[cur(3 * R), cur(3 * R), cur(2 * LORA),
                  halo(3 * R), halo(3 * R), halo(2 * LORA),
                  whole((HALO, 3 * R)), whole((HALO, 3 * R)), whole((HALO, 2 * LORA)),
                  whole((8, R)), whole((LORA_W, R)), whole((LORA_A, R)), whole((LORA_G, R)),
                  whole((8, R)), whole((R, R))],
        out_specs=[out_spec] * 8,
        out_shape=[jax.ShapeDtypeStruct((t, R), f32 if n == 5 else bf16) for n in range(8)],
        compiler_params=_cparams(("parallel",)),
    )(rkv, cv, lo, rkv, cv, lo, *init, vec, ww2, wa2, wg2, cw, seg)


def _wkv_kernel(r_ref, k_ref, v_ref, kk_ref, b_ref, lw_ref, s0_ref, y_ref, sT_ref, s_scr):
    c = pl.program_id(1)
    nbw = r_ref.shape[0]

    @pl.when(c == 0)
    def _():
        for j in range(nbw):
            s_scr[j] = s0_ref[...]

    C = CHUNK
    ri = lax.broadcasted_iota(jnp.int32, (C, C), 0)
    ci = lax.broadcasted_iota(jnp.int32, (C, C), 1)
    incl = ri >= ci
    strict = ri > ci
    ri2 = lax.broadcasted_iota(jnp.int32, (C, 2 * C), 0)
    ci2 = lax.broadcasted_iota(jnp.int32, (C, 2 * C), 1)
    incl2 = ri2 >= jnp.where(ci2 >= C, ci2 - C, ci2)
    tri = jnp.where(incl, 1.0, 0.0).astype(bf16)
    eye = jnp.where(ri == ci, 1.0, 0.0).astype(f32)
    dot = functools.partial(jnp.dot, preferred_element_type=f32)

    at, rt, bt, kt, bh, kh, vv, g_tot = [], [], [], [], [], [], [], []
    for j in range(nbw):
        lw = lw_ref[j]
        h, m, l = _split3(lw)
        cum = dot(tri, h) + dot(tri, m) + dot(tri, l)
        tot = cum[C - 1:C, :]
        k = k_ref[j].astype(f32)
        b = b_ref[j].astype(f32)
        g_inv = jnp.exp(-cum)
        tail = jnp.exp(tot - cum)
        rt.append(r_ref[j].astype(f32) * jnp.exp(cum))
        kt.append(k * g_inv)
        bt.append(b * g_inv)
        at.append(-kk_ref[j].astype(f32) * jnp.exp(cum - lw))
        bh.append(b * tail)
        kh.append(k * tail)
        vv.append(v_ref[j].astype(f32))
        g_tot.append(jnp.exp(tot))

    chains = [(j, hd) for j in range(nbw) for hd in range(NH)]
    cs = range(len(chains))
    sl = lambda hd: slice(hd * HS, (hd + 1) * HS)
    s_old = [s_scr[j, hd] for j, hd in chains]
    lhs = [jnp.concatenate([at[j][:, sl(hd)], rt[j][:, sl(hd)]], axis=0).astype(bf16)
           for j, hd in chains]
    rhs = [jnp.concatenate([bt[j][:, sl(hd)], kt[j][:, sl(hd)]], axis=0).astype(bf16)
           for j, hd in chains]
    amat = [_bdot_nt(lhs[n], rhs[n]) for n in cs]
    a_ab = [jnp.where(strict, amat[n][:C, :C], 0.0) for n in cs]
    a_ak = [jnp.where(strict, amat[n][:C, C:], 0.0) for n in cs]
    a_r = [jnp.where(incl2, amat[n][C:, :], 0.0) for n in cs]
    x = [eye + a_ab[n] for n in cs]
    p = a_ab
    for _ in range(5):
        pb = [p[n].astype(bf16) for n in cs]
        p = [dot(pb[n], pb[n]) for n in cs]
        x = [x[n] + _bdot(x[n], p[n]) for n in cs]
    vh = [vv[j][:, sl(hd)] for j, hd in chains]
    sh = [_bdot_nt(lhs[n], s_old[n]) for n in cs]
    av = [_bdot(a_ak[n], vh[n]) for n in cs]
    u = [_bdot(x[n], sh[n][:C] + av[n]) for n in cs]
    uv = [jnp.concatenate([u[n], vh[n]], axis=0).astype(bf16) for n in cs]
    ys = [sh[n][C:] + _bdot(a_r[n], uv[n]) for n in cs]
    bk = [jnp.concatenate([bh[j][:, sl(hd)], kh[j][:, sl(hd)]], axis=0) for j, hd in chains]
    s_new = [s_old[n] * g_tot[j][:, sl(hd)] + _bdot_tn(uv[n], bk[n])
             for n, (j, hd) in enumerate(chains)]
    for n, (j, hd) in enumerate(chains):
        s_scr[j, hd] = s_new[n]
    for j in range(nbw):
        y_ref[j] = jnp.concatenate(ys[j * NH:(j + 1) * NH], axis=1)

    @pl.when(c == pl.num_programs(1) - 1)
    def _():
        sT_ref[...] = s_scr[...]


def _wkv(r, k, v, kk, b, lw, s0, nbw):
    nb, seq, _ = r.shape
    assert nb % nbw == 0 and seq % CHUNK == 0
    blk = pl.BlockSpec((nbw, CHUNK, R), lambda i, c: (i, c, 0))
    return pl.pallas_call(
        _wkv_kernel,
        name="wkv",
        grid=(nb // nbw, seq // CHUNK),
        in_specs=[blk] * 6 + [pl.BlockSpec((NH, HS, HS), lambda i, c: (0, 0, 0))],
        out_specs=[blk, pl.BlockSpec((nbw, NH, HS, HS), lambda i, c: (i, 0, 0, 0))],
        out_shape=[jax.ShapeDtypeStruct((nb, seq, R), f32),
                   jax.ShapeDtypeStruct((nb, NH, HS, HS), f32)],
        scratch_shapes=[pltpu.VMEM((nbw, NH, HS, HS), f32)],
        compiler_params=_cparams(("parallel", "arbitrary")),
    )(r, k, v, kk, b, lw, s0)


def _merge_kernel(y_ref, r_ref, k_ref, v_ref, g_ref, cb_ref, gt_ref, x_ref, vec_ref, ln2_ref,
                  seg_ref, worw_ref, woc_ref, wo_ref, wr_ref, br_ref,
                  h_out, x2_out, lg_out):
    seg = seg_ref[...]
    vec = vec_ref[...]
    lnx_g, lnx_b, r_k = vec[0:1, :], vec[1:2, :], vec[2:3, :]
    y = y_ref[...]
    mean = _head_sum(y, seg) * (1.0 / HS)
    yc = y - mean
    var = _head_sum(yc * yc, seg) * (1.0 / HS)
    yn = yc * lax.rsqrt(var + LNX_EPS) * lnx_g + lnx_b
    v = v_ref[...].astype(f32)
    bonus = _head_sum(r_ref[...].astype(f32) * k_ref[...].astype(f32) * r_k, seg) * v
    y_a = _bdot((yn + bonus) * g_ref[...].astype(f32), worw_ref[...])
    y_b = _bdot(cb_ref[...], woc_ref[...])
    gt = gt_ref[...].astype(f32)
    merged = jax.nn.sigmoid(gt[:, :D]) * y_a + jax.nn.sigmoid(gt[:, D:]) * y_b
    hres = x_ref[...] + _bdot(merged, wo_ref[...])
    h_out[...] = hres
    x2 = hres * lax.rsqrt(jnp.mean(hres * hres, axis=-1, keepdims=True) + NORM_EPS) * ln2_ref[...]
    x2_out[...] = x2.astype(x2_out.dtype)
    xh, xm, _ = _split3(x2)
    wh, wm, _ = _split3(wr_ref[...])
    nt = functools.partial(lax.dot_general, dimension_numbers=(((1,), (1,)), ((), ())),
                           preferred_element_type=f32)
    lg_out[...] = nt(wh, xh) + nt(wh, xm) + nt(wm, xh) + br_ref[...]


def _merge(y, r, k, v, g, cb, gt, x2d, vec, ln2, seg, worw, woc, wo, wr_t, br, tm):
    t = y.shape[0]

    def rows(n):
        return pl.BlockSpec((tm, n), lambda i: (i, 0))

    def whole(shape):
        return pl.BlockSpec(shape, lambda i: (0,) * len(shape))

    return pl.pallas_call(
        _merge_kernel,
        name="merge",
        grid=(t // tm,),
        in_specs=[rows(R)] * 6 + [rows(2 * D), rows(D), whole((8, R)), whole((1, D)),
                                   whole((R, R)), whole((R, D)), whole((R, D)), whole((D, D)),
                                   whole((NE, D)), whole((NE, 1))],
        out_specs=[rows(D), rows(D), pl.BlockSpec((NE, tm), lambda i: (0, i))],
        out_shape=[jax.ShapeDtypeStruct((t, D), f32), jax.ShapeDtypeStruct((t, D), bf16),
                   jax.ShapeDtypeStruct((NE, t), f32)],
        compiler_params=_cparams(("parallel",)),
    )(y, r, k, v, g, cb, gt, x2d, vec, ln2, seg, worw, woc, wo, wr_t, br)


def _route_kernel(lg_ref, e_out, w_out, rank_out, cnt_out):
    lg = lg_ref[...]
    tr = lg.shape[1]
    erow = lax.broadcasted_iota(jnp.int32, lg.shape, 0)
    work = lg
    hits, vals, idxs = [], [], []
    for _ in range(TOPK):
        m = jnp.max(work, axis=0, keepdims=True)
        idx = jnp.min(jnp.where(work == m, erow, NE), axis=0, keepdims=True)
        hit = erow == idx
        hits.append(hit)
        vals.append(m)
        idxs.append(idx)
        work = jnp.where(hit, -jnp.inf, work)
    ex = [jnp.exp(vk - vals[0]) for vk in vals]
    den = ex[0] + ex[1] + ex[2] + ex[3]
    multi = jnp.where(hits[0] | hits[1] | hits[2] | hits[3], 1.0, 0.0)
    ti = lax.broadcasted_iota(jnp.int32, (tr, tr), 0)
    tj = lax.broadcasted_iota(jnp.int32, (tr, tr), 1)
    before = jnp.where(ti < tj, 1.0, 0.0).astype(bf16)
    excl = jnp.dot(multi.astype(bf16), before, preferred_element_type=f32)
    for kk in range(TOPK):
        e_out[kk:kk + 1, :] = idxs[kk]
        w_out[kk:kk + 1, :] = ex[kk] / den
        rank_out[kk:kk + 1, :] = jnp.sum(jnp.where(hits[kk], excl, 0.0), axis=0,
                                         keepdims=True).astype(jnp.int32)
    total = jnp.sum(multi, axis=1, keepdims=True)
    cnt_out[...] = jnp.broadcast_to(total, cnt_out.shape).astype(jnp.int32)


def _route(logits_t):
    t = logits_t.shape[1]
    sel = pl.BlockSpec((TOPK, TT), lambda i: (0, i))
    return pl.pallas_call(
        _route_kernel,
        name="route",
        grid=(t // TT,),
        in_specs=[pl.BlockSpec((NE, TT), lambda i: (0, i))],
        out_specs=[sel, sel, sel, pl.BlockSpec((None, NE, 128), lambda i: (i, 0, 0))],
        out_shape=[jax.ShapeDtypeStruct((TOPK, t), jnp.int32),
                   jax.ShapeDtypeStruct((TOPK, t), f32),
                   jax.ShapeDtypeStruct((TOPK, t), jnp.int32),
                   jax.ShapeDtypeStruct((t // TT, NE, 128), jnp.int32)],
        compiler_params=_cparams(("parallel",)),
    )(logits_t)


def _run_copies(tile, cnt_ref, make_copy, act):
    def per_expert(e, carry):
        cnt = cnt_ref[tile * NE + e]
        done = jnp.int32(0)
        for size in SLABS:
            @pl.when((cnt & size) != 0)
            def _():
                act(make_copy(e, done, size))
            done = done + (cnt & size)
        return carry
    lax.fori_loop(0, NE, per_expert, 0)


def _dispatch_kernel(cnt_ref, run_ref, off_ref, tail_ref, pos_ref, x2_ref, xs_ref, sbuf, zbuf, sem, zsem):
    i = pl.program_id(0)

    @pl.when(i == 0)
    def _():
        zbuf[...] = jnp.zeros_like(zbuf)

        def zero_copy(start):
            return pltpu.make_async_copy(zbuf, xs_ref.at[pl.ds(pl.multiple_of(start, EBLK), EBLK)], zsem)

        for e in range(NE):
            @pl.when(tail_ref[e] >= 0)
            def _():
                zero_copy(tail_ref[e]).start()
        for e in range(NE):
            @pl.when(tail_ref[e] >= 0)
            def _():
                zero_copy(tail_ref[e]).wait()
        n_blk = xs_ref.shape[0] // EBLK
        lax.fori_loop(tail_ref[NE], n_blk, lambda blk, c: (zero_copy(blk * EBLK).start(), c)[1], 0)
        lax.fori_loop(tail_ref[NE], n_blk, lambda blk, c: (zero_copy(blk * EBLK).wait(), c)[1], 0)

    def slab(e, done, size):
        src = pl.multiple_of(off_ref[i * NE + e] + done, ALIGN)
        dst = pl.multiple_of(run_ref[i * NE + e] + done, ALIGN)
        return pltpu.make_async_copy(sbuf.at[pl.ds(src, size)], xs_ref.at[pl.ds(dst, size)], sem)

    pos = pos_ref[...]
    slot = lax.broadcasted_iota(jnp.int32, (SB, TT), 0)
    hit = (slot == pos[0:1, :]) | (slot == pos[1:2, :]) | (slot == pos[2:3, :]) | (slot == pos[3:4, :])
    place = jnp.where(hit, 1.0, 0.0).astype(bf16)
    sbuf[...] = jnp.dot(place, x2_ref[...], preferred_element_type=f32).astype(bf16)
    _run_copies(i, cnt_ref, slab, lambda cp: cp.start())
    _run_copies(i, cnt_ref, slab, lambda cp: cp.wait())


def _dispatch(cnt16, run_start, off16, tail, pos, x2, n_rows):
    t = x2.shape[0]
    return pl.pallas_call(
        _dispatch_kernel,
        name="dispatch",
        grid_spec=pltpu.PrefetchScalarGridSpec(
            num_scalar_prefetch=4, grid=(t // TT,),
            in_specs=[pl.BlockSpec((TOPK, TT), lambda i, *_: (0, i)),
                      pl.BlockSpec((TT, D), lambda i, *_: (i, 0))],
            out_specs=pl.BlockSpec(memory_space=pl.ANY),
            scratch_shapes=[pltpu.VMEM((SB, D), bf16), pltpu.VMEM((EBLK, D), bf16),
                            pltpu.SemaphoreType.DMA(()), pltpu.SemaphoreType.DMA(())]),
        out_shape=jax.ShapeDtypeStruct((n_rows, D), bf16),
        compiler_params=_cparams(("arbitrary",)),
    )(cnt16, run_start, off16, tail, pos, x2)


def _expert_kernel(be_ref, nu_ref, fx_ref, xs_ref, wg_ref, bg_ref, wu_ref, bu_ref, wd_ref, bd_ref,
                   ys_ref, wgb, wub, wdb):
    del be_ref
    i = pl.program_id(0)

    @pl.when(fx_ref[i] == 1)
    def _():
        wgb[...] = wg_ref[...].astype(bf16)
        wub[...] = wu_ref[...].astype(bf16)
        wdb[...] = wd_ref[...].astype(bf16)

    @pl.when(i < nu_ref[0])
    def _():
        xb = xs_ref[...]
        gt = jnp.minimum(jnp.dot(xb, wgb[...], preferred_element_type=f32) + bg_ref[...], LIMIT)
        up = jnp.clip(jnp.dot(xb, wub[...], preferred_element_type=f32) + bu_ref[...],
                      -LIMIT, LIMIT)
        act = (up + 1.0) * (gt * jax.nn.sigmoid(gt * ALPHA))
        ys_ref[...] = (jnp.dot(act.astype(bf16), wdb[...], preferred_element_type=f32)
                       + bd_ref[...]).astype(ys_ref.dtype)

    @pl.when(i >= nu_ref[0])
    def _():
        ys_ref[...] = jnp.zeros_like(ys_ref)


def _experts(blk_e, n_used, first, xs, wg, bg, wu, bu, wd, bd):
    p = xs.shape[0]
    wspec = pl.BlockSpec((None, D, D), lambda i, be, nu, fx: (be[i], 0, 0))
    bspec = pl.BlockSpec((None, 1, D), lambda i, be, nu, fx: (be[i], 0, 0))
    xrows = pl.BlockSpec((EBLK, D), lambda i, be, nu, fx: (jnp.where(i < nu[0], i, 0), 0))
    yrows = pl.BlockSpec((EBLK, D), lambda i, be, nu, fx: (i, 0))
    return pl.pallas_call(
        _expert_kernel,
        name="experts",
        grid_spec=pltpu.PrefetchScalarGridSpec(
            num_scalar_prefetch=3, grid=(p // EBLK,),
            in_specs=[xrows, wspec, bspec, wspec, bspec, wspec, bspec],
            out_specs=yrows,
            scratch_shapes=[pltpu.VMEM((D, D), bf16)] * 3),
        out_shape=jax.ShapeDtypeStruct((p, D), bf16),
        compiler_params=_cparams(("arbitrary",)),
    )(blk_e, n_used, first, xs, wg, bg, wu, bu, wd, bd)


def _combine_kernel(cnt_ref, run_ref, off_ref, ys_ref, pos_ref, w_ref, h_ref, g_ref, o_ref, rbuf, sem):
    i = pl.program_id(0)
    cur = i % 2

    def fetch(tile, half):
        def slab(e, done, size):
            src = pl.multiple_of(run_ref[tile * NE + e] + done, ALIGN)
            dst = pl.multiple_of(off_ref[tile * NE + e] + done, ALIGN)
            return pltpu.make_async_copy(ys_ref.at[pl.ds(src, size)],
                                         rbuf.at[half, pl.ds(dst, size)], sem.at[half])
        return slab

    @pl.when(i == 0)
    def _():
        rbuf[...] = jnp.zeros_like(rbuf)
        _run_copies(0, cnt_ref, fetch(0, 0), lambda cp: cp.start())

    @pl.when(i + 1 < pl.num_programs(0))
    def _():
        _run_copies(i + 1, cnt_ref, fetch(i + 1, 1 - cur), lambda cp: cp.start())

    pos = pos_ref[...]
    w = w_ref[...]
    slot = lax.broadcasted_iota(jnp.int32, (TT, SB), 1)
    gate = jnp.zeros((TT, SB), f32)
    for kk in range(TOPK):
        gate = gate + jnp.where(slot == pos[:, kk:kk + 1], w[:, kk:kk + 1], 0.0)
    _run_copies(i, cnt_ref, fetch(i, cur), lambda cp: cp.wait())
    acc = h_ref[...] + jnp.dot(gate.astype(bf16), rbuf[cur], preferred_element_type=f32)
    o_ref[...] = acc * lax.rsqrt(jnp.mean(acc * acc, axis=-1, keepdims=True) + NORM_EPS) * g_ref[...]


def _combine(cnt16, run_start, off16, ys, pos_t, gates_t, hres, lnf):
    t = hres.shape[0]
    return pl.pallas_call(
        _combine_kernel,
        name="combine",
        grid_spec=pltpu.PrefetchScalarGridSpec(
            num_scalar_prefetch=3, grid=(t // TT,),
            in_specs=[pl.BlockSpec(memory_space=pl.ANY),
                      pl.BlockSpec((TT, TOPK), lambda i, *_: (i, 0)),
                      pl.BlockSpec((TT, TOPK), lambda i, *_: (i, 0)),
                      pl.BlockSpec((TT, D), lambda i, *_: (i, 0)),
                      pl.BlockSpec((1, D), lambda i, *_: (0, 0))],
            out_specs=pl.BlockSpec((TT, D), lambda i, *_: (i, 0)),
            scratch_shapes=[pltpu.VMEM((2, SB, D), bf16), pltpu.SemaphoreType.DMA((2,))]),
        out_shape=jax.ShapeDtypeStruct((t, D), f32),
        compiler_params=_cparams(("arbitrary",)),
    )(cnt16, run_start, off16, ys, pos_t, gates_t, hres, lnf)


def _row_tile(n, want):
    t = min(n, want)
    assert n % t == 0
    return t


def _mixer_inputs(x2d, seq_len, init, p):
    n = x2d.shape[0]
    rkv, cv, gt, lo = _inproj(x2d, p["ln1"], p["w_in"], _row_tile(n, 512))
    outs = _mix(rkv, cv, lo, init, p["vec_mix"], p["ww2"], p["wa2"], p["wg2"], p["cw"], p["seg"],
                seq_len, _row_tile(seq_len, 256))
    return (rkv, cv, lo), gt, outs


def kernel(x, meta_tokens, ln1_g, w_in, mu_r, mu_k, mu_v, mu_w, mu_a, mu_g, w0, w_w1, w_w2, a0, w_a1, w_a2, w_g1, w_g2, k_k, k_a, r_k, lnx_g, lnx_b, w_o_rwkv, conv_w, w_o_conv, w_o, ln2_g, w_router, b_router, w_e_gate, b_e_gate, w_e_up, b_e_up, w_e_down, b_e_down, lnf_g):
    nb, seq, _ = x.shape
    t = nb * seq
    assert ln1_g.shape[0] == 1, "single layer"

    muw, mua, mug = mu_w[0][:, None], mu_a[0][:, None], mu_g[0][:, None]
    lora_cur = jnp.concatenate([(1 - muw) * w_w1[0], (1 - mua) * w_a1[0], (1 - mug) * w_g1[0]], axis=1)
    lora_prev = jnp.concatenate([muw * w_w1[0], mua * w_a1[0], mug * w_g1[0]], axis=1)
    zrow = jnp.zeros((1, R), f32)
    p = {
        "ln1": ln1_g[0][None, :],
        "w_in": jnp.concatenate([w_in[0], lora_cur, lora_prev], axis=1).astype(bf16),
        "vec_mix": jnp.concatenate([mu_r, mu_k, mu_v, w0, a0, k_k, k_a, zrow], axis=0),
        "ww2": w_w2[0].astype(bf16), "wa2": w_a2[0].astype(bf16), "wg2": w_g2[0].astype(bf16),
        "cw": jnp.concatenate([conv_w[0], jnp.zeros((5, R), f32)], axis=0),
        "seg": (jnp.arange(R)[:, None] // HS == jnp.arange(R)[None, :] // HS).astype(bf16),
    }
    vec_merge = jnp.concatenate([lnx_g, lnx_b, r_k, jnp.zeros((5, R), f32)], axis=0)

    zero_init = (jnp.zeros((HALO, 3 * R), bf16), jnp.zeros((HALO, 3 * R), bf16),
                 jnp.zeros((HALO, 2 * LORA), bf16))
    meta_proj, _, meta_ops = _mixer_inputs(meta_tokens.astype(f32), NMETA, zero_init, p)
    pad = lambda z: jnp.pad(z, ((CHUNK - NMETA, 0), (0, 0)))[None]
    r_m, k_m, v_m, kk_m, b_m, lw_m = (pad(z) for z in meta_ops[:6])
    _, s_meta = _wkv(r_m, k_m, v_m, kk_m, b_m, lw_m, jnp.zeros((NH, HS, HS), f32), 1)

    x2d = x.reshape(t, D)
    _, gt, ops = _mixer_inputs(x2d, seq, meta_proj, p)
    r, k, v, kk, b, lw, g, cb = ops
    as3 = lambda z: z.reshape(nb, seq, R)
    y, _ = _wkv(as3(r), as3(k), as3(v), as3(kk), as3(b), as3(lw), s_meta[0], WKV_SEQS)
    hres, x2, logits_t = _merge(
        y.reshape(t, R), r, k, v, g, cb, gt, x2d, vec_merge, ln2_g[0][None, :], p["seg"],
        w_o_rwkv[0].astype(bf16), w_o_conv[0].astype(bf16), w_o[0].astype(bf16),
        w_router[0].T, b_router[0][:, None], _row_tile(t, 256))

    assert t % TT == 0
    nt = t // TT
    top_e, gates, lrank, cnt = _route(logits_t)
    cnt = cnt[:, :, 0]
    cnt16 = ((cnt + ALIGN - 1) // ALIGN) * ALIGN
    base16 = jnp.cumsum(cnt16, axis=0) - cnt16
    tot16 = jnp.sum(cnt16, axis=0)
    padded = ((tot16 + EBLK - 1) // EBLK) * EBLK
    pend = jnp.cumsum(padded)
    pstart = pend - padded
    run_start = pstart[None, :] + base16
    off16 = jnp.cumsum(cnt16, axis=1) - cnt16
    eids = jnp.arange(NE, dtype=jnp.int32)[:, None, None, None]
    te = top_e.reshape(TOPK, nt, TT)
    pos = lrank + jnp.sum(jnp.where(te[None] == eids, off16.T[:, None, :, None], 0), axis=0).reshape(TOPK, t)
    n_rows = -(-(t * TOPK + nt * NE * (ALIGN - 1) + NE * EBLK) // EBLK) * EBLK
    n_blk = n_rows // EBLK
    blk_start = jnp.arange(n_blk, dtype=jnp.int32) * EBLK
    blk_e = jnp.minimum(jnp.sum(pend[None, :] <= blk_start[:, None], axis=1), NE - 1).astype(jnp.int32)
    n_used = (pend[NE - 1:] // EBLK).astype(jnp.int32)
    tail = jnp.concatenate([jnp.where(padded > 0, pend - EBLK, -1), n_used]).astype(jnp.int32)
    flat = lambda z: z.reshape(-1).astype(jnp.int32)
    xs = _dispatch(flat(cnt16), flat(run_start), flat(off16), tail, pos, x2, n_rows)
    first = jnp.concatenate([jnp.ones((1,), jnp.int32), (blk_e[1:] != blk_e[:-1]).astype(jnp.int32)])
    ys = _experts(blk_e, n_used, first, xs,
                  w_e_gate[0], b_e_gate[0][:, None, :],
                  w_e_up[0], b_e_up[0][:, None, :],
                  w_e_down[0], b_e_down[0][:, None, :])
    out = _combine(flat(cnt16), flat(run_start), flat(off16), ys, pos.T, gates.T, hres, lnf_g[None, :])
    return out.reshape(nb, seq, D)
```

```python
import functools

import jax
import jax.numpy as jnp
from jax import lax
from jax.experimental import pallas as pl
from jax.experimental.pallas import tpu as pltpu

D = 1024
R = 512
NH = 8
HS = 64
NE = 32
TOPK = 4
NMETA = 16
CHUNK = 64
WKV_SEQS = 4
EBLK = 512
TT = 512
ALIGN = 16
SB = 4 * TT + 512
MERGE_SUB = 256
SCHUNK = 512
NORM_EPS = 1e-5
LNX_EPS = 64e-5
ALPHA = 1.702
LIMIT = 7.0
LORA_W, LORA_A, LORA_G = 64, 64, 128
LORA = LORA_W + LORA_A + LORA_G
NPROJ = 3 * R + 3 * R + 2 * D + 2 * LORA
VMEM_LIMIT = 56 * 1024 * 1024

f32 = jnp.float32
bf16 = jnp.bfloat16


def _bdot(a, b):
    return jnp.dot(a.astype(bf16), b.astype(bf16), preferred_element_type=f32)


def _bdot_nt(a, b):
    return lax.dot_general(a.astype(bf16), b.astype(bf16), (((1,), (1,)), ((), ())),
                           preferred_element_type=f32)


def _bdot_tn(a, b):
    return lax.dot_general(a.astype(bf16), b.astype(bf16), (((0,), (0,)), ((), ())),
                           preferred_element_type=f32)


def _split3(x):
    h = x.astype(bf16)
    r1 = x - h.astype(f32)
    m = r1.astype(bf16)
    l = (r1 - m.astype(f32)).astype(bf16)
    return h, m, l


def _head_sum(x, seg):
    h = x.astype(bf16)
    l = (x - h.astype(f32)).astype(bf16)
    dot = functools.partial(jnp.dot, preferred_element_type=f32)
    return dot(h, seg) + dot(l, seg)


def _cparams(sem):
    return pltpu.CompilerParams(dimension_semantics=sem, vmem_limit_bytes=VMEM_LIMIT)


def _inproj_kernel(x_ref, g_ref, w_ref, rkv_ref, cv_ref, gt_ref, lo_ref):
    x = x_ref[...]
    xn = x * lax.rsqrt(jnp.mean(x * x, axis=-1, keepdims=True) + NORM_EPS) * g_ref[...]
    xb = xn.astype(bf16)
    col = 0
    for ref in (rkv_ref, cv_ref, gt_ref, lo_ref):
        width = ref.shape[-1]
        for c in range(0, width, 512):
            ref[:, c:c + 512] = jnp.dot(
                xb, w_ref[:, col + c:col + c + 512], preferred_element_type=f32
            ).astype(ref.dtype)
        col += width


def _inproj(x2d, g, w, tm):
    t = x2d.shape[0]
    widths = (3 * R, 3 * R, 2 * D, 2 * LORA)
    return pl.pallas_call(
        _inproj_kernel,
        name="inproj",
        grid=(t // tm,),
        in_specs=[pl.BlockSpec((tm, D), lambda i: (i, 0)),
                  pl.BlockSpec((1, D), lambda i: (0, 0)),
                  pl.BlockSpec((D, NPROJ), lambda i: (0, 0))],
        out_specs=[pl.BlockSpec((tm, n), lambda i: (i, 0)) for n in widths],
        out_shape=[jax.ShapeDtypeStruct((t, n), bf16) for n in widths],
        compiler_params=_cparams(("parallel",)),
    )(x2d, g, w)


HALO = 16


def _mix_kernel(tiles_per_seq, rkv_ref, cv_ref, lo_ref, rkv_h_ref, cv_h_ref, lo_h_ref,
                rkv_i_ref, cv_i_ref, lo_i_ref, vec_ref, ww2_ref, wa2_ref, wg2_ref, cw_ref,
                seg_ref, r_out, k_out, v_out, kk_out, b_out, lw_out, g_out, cb_out):
    first = (pl.program_id(0) % tiles_per_seq) == 0
    tm = rkv_ref.shape[0]
    row = lax.broadcasted_iota(jnp.int32, (tm, 1), 0)

    def history(h_ref, i_ref):
        return jnp.where(first, i_ref[...].astype(f32), h_ref[...].astype(f32))

    def shifted(cur, hist, n):
        out = pltpu.roll(cur, n, axis=0)
        for j in range(n):
            out = jnp.where(row == j, hist[HALO - n + j:HALO - n + j + 1, :], out)
        return out

    vec = vec_ref[...]
    mu_r, mu_k, mu_v = vec[0:1, :], vec[1:2, :], vec[2:3, :]
    w0, a0, k_k, k_a = vec[3:4, :], vec[4:5, :], vec[5:6, :], vec[6:7, :]

    rkv = rkv_ref[...].astype(f32)
    rkv_prev = shifted(rkv, history(rkv_h_ref, rkv_i_ref), 1)
    pr, pk, pv = rkv[:, :R], rkv[:, R:2 * R], rkv[:, 2 * R:]
    r = pr + (rkv_prev[:, :R] - pr) * mu_r
    k = pk + (rkv_prev[:, R:2 * R] - pk) * mu_k
    v = pv + (rkv_prev[:, 2 * R:] - pv) * mu_v

    lo = lo_ref[...].astype(f32)
    lo_prev = shifted(lo, history(lo_h_ref, lo_i_ref), 1)
    mixed = lo[:, :LORA] + lo_prev[:, LORA:]
    hw = jnp.tanh(mixed[:, :LORA_W])
    ha = mixed[:, LORA_W:LORA_W + LORA_A]
    hg = jax.nn.sigmoid(mixed[:, LORA_W + LORA_A:])
    w_log = -jax.nn.softplus(-(w0 + _bdot(hw, ww2_ref[...]))) - 0.5
    a = jax.nn.sigmoid(a0 + _bdot(ha, wa2_ref[...]))
    g = _bdot(hg, wg2_ref[...])

    kk = k * k_k
    norm = jnp.sqrt(_head_sum(kk * kk, seg_ref[...]))
    kk = kk / jnp.maximum(norm, 1e-12)
    k = k * (1.0 + (a - 1.0) * k_a)

    r_out[...] = r.astype(r_out.dtype)
    k_out[...] = k.astype(k_out.dtype)
    v_out[...] = v.astype(v_out.dtype)
    kk_out[...] = kk.astype(kk_out.dtype)
    b_out[...] = (kk * a).astype(b_out.dtype)
    lw_out[...] = -jnp.exp(w_log)
    g_out[...] = g.astype(g_out.dtype)

    cv = cv_ref[...].astype(f32)
    hist = history(cv_h_ref, cv_i_ref)
    u = cv[:, R:2 * R] * cv[:, 2 * R:]
    uh = hist[:, R:2 * R] * hist[:, 2 * R:]
    cw = cw_ref[...]
    conv = cw[2:3, :] * u + cw[1:2, :] * shifted(u, uh, 1) + cw[0:1, :] * shifted(u, uh, 2)
    cb_out[...] = (cv[:, :R] * conv).astype(cb_out.dtype)


def _mix(rkv, cv, lo, init, vec, ww2, wa2, wg2, cw, seg, seq_len, tm):
    t = rkv.shape[0]
    tiles_per_seq = seq_len // tm
    hb = tm // HALO

    def cur(n):
        return pl.BlockSpec((tm, n), lambda i: (i, 0))

    def halo(n):
        return pl.BlockSpec((HALO, n), lambda i: (jnp.maximum(i * hb - 1, 0), 0))

    def whole(shape):
        return pl.BlockSpec(shape, lambda i: (0,) * len(shape))

    out_spec = pl.BlockSpec((tm, R), lambda i: (i, 0))
    return pl.pallas_call(
        functools.partial(_mix_kernel, tiles_per_seq),
        name="mix",
        grid=(t // tm,),
        in_specs=[cur(3 * R), cur(3 * R), cur(2 * LORA),
                  halo(3 * R), halo(3 * R), halo(2 * LORA),
                  whole((HALO, 3 * R)), whole((HALO, 3 * R)), whole((HALO, 2 * LORA)),
                  whole((8, R)), whole((LORA_W, R)), whole((LORA_A, R)), whole((LORA_G, R)),
                  whole((8, R)), whole((R, R))],
        out_specs=[out_spec] * 8,
        out_shape=[jax.ShapeDtypeStruct((t, R), f32 if n == 5 else bf16) for n in range(8)],
        compiler_params=_cparams(("parallel",)),
    )(rkv, cv, lo, rkv, cv, lo, *init, vec, ww2, wa2, wg2, cw, seg)


def _wkv_kernel(r_ref, k_ref, v_ref, kk_ref, b_ref, lw_ref, s0_ref, y_ref, sT_ref, s_scr):
    c = pl.program_id(1)
    nbw = r_ref.shape[0]

    @pl.when(c == 0)
    def _():
        for j in range(nbw):
            s_scr[j] = s0_ref[...]

    C = CHUNK
    ri = lax.broadcasted_iota(jnp.int32, (C, C), 0)
    ci = lax.broadcasted_iota(jnp.int32, (C, C), 1)
    incl = ri >= ci
    strict = ri > ci
    ri2 = lax.broadcasted_iota(jnp.int32, (C, 2 * C), 0)
    ci2 = lax.broadcasted_iota(jnp.int32, (C, 2 * C), 1)
    incl2 = ri2 >= jnp.where(ci2 >= C, ci2 - C, ci2)
    tri = jnp.where(incl, 1.0, 0.0).astype(bf16)
    eye = jnp.where(ri == ci, 1.0, 0.0).astype(f32)
    dot = functools.partial(jnp.dot, preferred_element_type=f32)

    at, rt, bt, kt, bh, kh, vv, g_tot = [], [], [], [], [], [], [], []
    for j in range(nbw):
        lw = lw_ref[j]
        h, m, l = _split3(lw)
        cum = dot(tri, h) + dot(tri, m) + dot(tri, l)
        tot = cum[C - 1:C, :]
        k = k_ref[j].astype(f32)
        b = b_ref[j].astype(f32)
        g_inv = jnp.exp(-cum)
        tail = jnp.exp(tot - cum)
        rt.append(r_ref[j].astype(f32) * jnp.exp(cum))
        kt.append(k * g_inv)
        bt.append(b * g_inv)
        at.append(-kk_ref[j].astype(f32) * jnp.exp(cum - lw))
        bh.append(b * tail)
        kh.append(k * tail)
        vv.append(v_ref[j].astype(f32))
        g_tot.append(jnp.exp(tot))

    chains = [(j, hd) for j in range(nbw) for hd in range(NH)]
    cs = range(len(chains))
    sl = lambda hd: slice(hd * HS, (hd + 1) * HS)
    s_old = [s_scr[j, hd] for j, hd in chains]
    lhs = [jnp.concatenate([at[j][:, sl(hd)], rt[j][:, sl(hd)]], axis=0).astype(bf16)
           for j, hd in chains]
    rhs = [jnp.concatenate([bt[j][:, sl(hd)], kt[j][:, sl(hd)]], axis=0).astype(bf16)
           for j, hd in chains]
    amat = [_bdot_nt(lhs[n], rhs[n]) for n in cs]
    a_ab = [jnp.where(strict, amat[n][:C, :C], 0.0) for n in cs]
    a_ak = [jnp.where(strict, amat[n][:C, C:], 0.0) for n in cs]
    a_r = [jnp.where(incl2, amat[n][C:, :], 0.0) for n in cs]
    x = [eye + a_ab[n] for n in cs]
    p = a_ab
    for _ in range(5):
        pb = [p[n].astype(bf16) for n in cs]
        p = [dot(pb[n], pb[n]) for n in cs]
        x = [x[n] + _bdot(x[n], p[n]) for n in cs]
    vh = [vv[j][:, sl(hd)] for j, hd in chains]
    sh = [_bdot_nt(lhs[n], s_old[n]) for n in cs]
    av = [_bdot(a_ak[n], vh[n]) for n in cs]
    u = [_bdot(x[n], sh[n][:C] + av[n]) for n in cs]
    uv = [jnp.concatenate([u[n], vh[n]], axis=0).astype(bf16) for n in cs]
    ys = [sh[n][C:] + _bdot(a_r[n], uv[n]) for n in cs]
    bk = [jnp.concatenate([bh[j][:, sl(hd)], kh[j][:, sl(hd)]], axis=0) for j, hd in chains]
    s_new = [s_old[n] * g_tot[j][:, sl(hd)] + _bdot_tn(uv[n], bk[n])
             for n, (j, hd) in enumerate(chains)]
    for n, (j, hd) in enumerate(chains):
        s_scr[j, hd] = s_new[n]
    for j in range(nbw):
        y_ref[j] = jnp.concatenate(ys[j * NH:(j + 1) * NH], axis=1)

    @pl.when(c == pl.num_programs(1) - 1)
    def _():
        sT_ref[...] = s_scr[...]


def _wkv(r, k, v, kk, b, lw, s0, nbw):
    nb, seq, _ = r.shape
    assert nb % nbw == 0 and seq % CHUNK == 0
    blk = pl.BlockSpec((nbw, CHUNK, R), lambda i, c: (i, c, 0))
    return pl.pallas_call(
        _wkv_kernel,
        name="wkv",
        grid=(nb // nbw, seq // CHUNK),
        in_specs=[blk] * 6 + [pl.BlockSpec((NH, HS, HS), lambda i, c: (0, 0, 0))],
        out_specs=[blk, pl.BlockSpec((nbw, NH, HS, HS), lambda i, c: (i, 0, 0, 0))],
        out_shape=[jax.ShapeDtypeStruct((nb, seq, R), f32),
                   jax.ShapeDtypeStruct((nb, NH, HS, HS), f32)],
        scratch_shapes=[pltpu.VMEM((nbw, NH, HS, HS), f32)],
        compiler_params=_cparams(("parallel", "arbitrary")),
    )(r, k, v, kk, b, lw, s0)


def _merge_kernel(y_ref, r_ref, k_ref, v_ref, g_ref, cb_ref, gt_ref, x_ref, vec_ref, ln2_ref,
                  seg_ref, worw_ref, woc_ref, wo_ref, wr_ref, br_ref,
                  h_out, x2_out, lg_out):
    seg = seg_ref[...]
    vec = vec_ref[...]
    lnx_g, lnx_b, r_k = vec[0:1, :], vec[1:2, :], vec[2:3, :]
    tm = y_ref.shape[0]
    rows = [slice(s0, s0 + MERGE_SUB) for s0 in range(0, tm, MERGE_SUB)]
    gs = range(len(rows))
    y = [y_ref[sl, :] for sl in rows]
    mean = [_head_sum(y[n], seg) * (1.0 / HS) for n in gs]
    yc = [y[n] - mean[n] for n in gs]
    var = [_head_sum(yc[n] * yc[n], seg) * (1.0 / HS) for n in gs]
    yn = [yc[n] * lax.rsqrt(var[n] + LNX_EPS) * lnx_g + lnx_b for n in gs]
    rk = [_head_sum(r_ref[sl, :].astype(f32) * k_ref[sl, :].astype(f32) * r_k, seg) for sl in rows]
    z = [(yn[n] + rk[n] * v_ref[rows[n], :].astype(f32)) * g_ref[rows[n], :].astype(f32) for n in gs]
    y_a = [_bdot(z[n], worw_ref[...]) for n in gs]
    y_b = [_bdot(cb_ref[sl, :], woc_ref[...]) for sl in rows]
    merged = [jax.nn.sigmoid(gt_ref[rows[n], :D].astype(f32)) * y_a[n]
              + jax.nn.sigmoid(gt_ref[rows[n], D:].astype(f32)) * y_b[n] for n in gs]
    hres = [x_ref[rows[n], :] + _bdot(merged[n], wo_ref[...]) for n in gs]
    x2 = [hres[n] * lax.rsqrt(jnp.mean(hres[n] * hres[n], axis=-1, keepdims=True) + NORM_EPS)
          * ln2_ref[...] for n in gs]
    wh, wm, _ = _split3(wr_ref[...])
    nt = functools.partial(lax.dot_general, dimension_numbers=(((1,), (1,)), ((), ())),
                           preferred_element_type=f32)
    for n, sl in enumerate(rows):
        h_out[sl, :] = hres[n]
        x2_out[sl, :] = x2[n].astype(x2_out.dtype)
        xh, xm, _ = _split3(x2[n])
        lg_out[:, sl] = nt(wh, xh) + nt(wh, xm) + nt(wm, xh) + br_ref[...]


def _merge(y, r, k, v, g, cb, gt, x2d, vec, ln2, seg, worw, woc, wo, wr_t, br, tm):
    t = y.shape[0]

    def rows(n):
        return pl.BlockSpec((tm, n), lambda i: (i, 0))

    def whole(shape):
        return pl.BlockSpec(shape, lambda i: (0,) * len(shape))

    return pl.pallas_call(
        _merge_kernel,
        name="merge",
        grid=(t // tm,),
        in_specs=[rows(R)] * 6 + [rows(2 * D), rows(D), whole((8, R)), whole((1, D)),
                                   whole((R, R)), whole((R, D)), whole((R, D)), whole((D, D)),
                                   whole((NE, D)), whole((NE, 1))],
        out_specs=[rows(D), rows(D), pl.BlockSpec((NE, tm), lambda i: (0, i))],
        out_shape=[jax.ShapeDtypeStruct((t, D), f32), jax.ShapeDtypeStruct((t, D), bf16),
                   jax.ShapeDtypeStruct((NE, t), f32)],
        compiler_params=_cparams(("parallel",)),
    )(y, r, k, v, g, cb, gt, x2d, vec, ln2, seg, worw, woc, wo, wr_t, br)


def _route_kernel(lg_ref, e_out, w_out, rank_out, cnt_out):
    lg = lg_ref[...]
    tr = lg.shape[1]
    erow = lax.broadcasted_iota(jnp.int32, lg.shape, 0)
    work = lg
    hits, vals, idxs = [], [], []
    for _ in range(TOPK):
        m = jnp.max(work, axis=0, keepdims=True)
        idx = jnp.min(jnp.where(work == m, erow, NE), axis=0, keepdims=True)
        hit = erow == idx
        hits.append(hit)
        vals.append(m)
        idxs.append(idx)
        work = jnp.where(hit, -jnp.inf, work)
    ex = [jnp.exp(vk - vals[0]) for vk in vals]
    den = ex[0] + ex[1] + ex[2] + ex[3]
    multi = jnp.where(hits[0] | hits[1] | hits[2] | hits[3], 1.0, 0.0)
    ti = lax.broadcasted_iota(jnp.int32, (tr, tr), 0)
    tj = lax.broadcasted_iota(jnp.int32, (tr, tr), 1)
    before = jnp.where(ti < tj, 1.0, 0.0).astype(bf16)
    excl = jnp.dot(multi.astype(bf16), before, preferred_element_type=f32)
    for kk in range(TOPK):
        e_out[kk:kk + 1, :] = idxs[kk]
        w_out[kk:kk + 1, :] = ex[kk] / den
        rank_out[kk:kk + 1, :] = jnp.sum(jnp.where(hits[kk], excl, 0.0), axis=0,
                                         keepdims=True).astype(jnp.int32)
    total = jnp.sum(multi, axis=1, keepdims=True)
    cnt_out[...] = jnp.broadcast_to(total, cnt_out.shape).astype(jnp.int32)


def _route(logits_t):
    t = logits_t.shape[1]
    sel = pl.BlockSpec((TOPK, TT), lambda i: (0, i))
    return pl.pallas_call(
        _route_kernel,
        name="route",
        grid=(t // TT,),
        in_specs=[pl.BlockSpec((NE, TT), lambda i: (0, i))],
        out_specs=[sel, sel, sel, pl.BlockSpec((None, NE, 128), lambda i: (i, 0, 0))],
        out_shape=[jax.ShapeDtypeStruct((TOPK, t), jnp.int32),
                   jax.ShapeDtypeStruct((TOPK, t), f32),
                   jax.ShapeDtypeStruct((TOPK, t), jnp.int32),
                   jax.ShapeDtypeStruct((t // TT, NE, 128), jnp.int32)],
        compiler_params=_cparams(("parallel",)),
    )(logits_t)


def _run_copies(tile, cnt_ref, make_copy, act):
    def per_expert(e, carry):
        cnt = pl.multiple_of(cnt_ref[tile * NE + e], ALIGN)

        @pl.when(cnt > 0)
        def _():
            act(make_copy(e, cnt))
        return carry
    lax.fori_loop(0, NE, per_expert, 0)


def _dispatch_kernel(cnt_ref, run_ref, off_ref, tot_ref, tail_ref, pos_ref, x2_ref, xs_ref,
                     sbuf, zbuf, sem, zsem):
    i = pl.program_id(0)
    last = pl.num_programs(0) - 1
    cur = i % 2

    @pl.when(i == 0)
    def _():
        zbuf[...] = jnp.zeros_like(zbuf)

        def zero_copy(start):
            return pltpu.make_async_copy(zbuf, xs_ref.at[pl.ds(pl.multiple_of(start, EBLK), EBLK)], zsem)

        for e in range(NE):
            @pl.when(tail_ref[e] >= 0)
            def _():
                zero_copy(tail_ref[e]).start()
        for e in range(NE):
            @pl.when(tail_ref[e] >= 0)
            def _():
                zero_copy(tail_ref[e]).wait()
        n_blk = xs_ref.shape[0] // EBLK
        lax.fori_loop(tail_ref[NE], n_blk, lambda blk, c: (zero_copy(blk * EBLK).start(), c)[1], 0)
        lax.fori_loop(tail_ref[NE], n_blk, lambda blk, c: (zero_copy(blk * EBLK).wait(), c)[1], 0)

    def slab(tile, half):
        def make(e, rows):
            src = pl.multiple_of(off_ref[tile * NE + e], ALIGN)
            dst = pl.multiple_of(run_ref[tile * NE + e], ALIGN)
            return pltpu.make_async_copy(sbuf.at[half, pl.ds(src, rows)],
                                         xs_ref.at[pl.ds(dst, rows)], sem.at[half])
        return make

    def drain(tile, half):
        n = pl.multiple_of(tot_ref[tile], ALIGN)
        pltpu.make_async_copy(sbuf.at[half, pl.ds(0, n)], xs_ref.at[pl.ds(0, n)], sem.at[half]).wait()

    @pl.when(i >= 2)
    def _():
        drain(i - 2, cur)

    pos = pos_ref[...]
    xt = x2_ref[...]
    for c in range(0, SB, SCHUNK):
        slot = lax.broadcasted_iota(jnp.int32, (SCHUNK, TT), 0) + c
        place = jnp.zeros((SCHUNK, TT), f32)
        for kk in range(TOPK):
            place = jnp.where(slot == pos[kk:kk + 1, :], 1.0, place)
        sbuf[cur, c:c + SCHUNK] = jnp.dot(place.astype(bf16), xt,
                                          preferred_element_type=f32).astype(bf16)
    _run_copies(i, cnt_ref, slab(i, cur), lambda cp: cp.start())

    @pl.when(i == last)
    def _():
        @pl.when(i >= 1)
        def _():
            drain(i - 1, 1 - cur)
        drain(i, cur)


def _dispatch(cnt16, run_start, off16, tot, tail, pos, x2, n_rows):
    t = x2.shape[0]
    return pl.pallas_call(
        _dispatch_kernel,
        name="dispatch",
        grid_spec=pltpu.PrefetchScalarGridSpec(
            num_scalar_prefetch=5, grid=(t // TT,),
            in_specs=[pl.BlockSpec((TOPK, TT), lambda i, *_: (0, i)),
                      pl.BlockSpec((TT, D), lambda i, *_: (i, 0))],
            out_specs=pl.BlockSpec(memory_space=pl.ANY),
            scratch_shapes=[pltpu.VMEM((2, SB, D), bf16), pltpu.VMEM((EBLK, D), bf16),
                            pltpu.SemaphoreType.DMA((2,)), pltpu.SemaphoreType.DMA(())]),
        out_shape=jax.ShapeDtypeStruct((n_rows, D), bf16),
        compiler_params=_cparams(("arbitrary",)),
    )(cnt16, run_start, off16, tot, tail, pos, x2)


def _expert_kernel(be_ref, nu_ref, fx_ref, xs_ref, wg_ref, bg_ref, wu_ref, bu_ref, wd_ref, bd_ref,
                   ys_ref, wgb, wub, wdb):
    del be_ref
    i = pl.program_id(0)

    @pl.when(fx_ref[i] == 1)
    def _():
        wgb[...] = wg_ref[...].astype(bf16)
        wub[...] = wu_ref[...].astype(bf16)
        wdb[...] = wd_ref[...].astype(bf16)

    @pl.when(i < nu_ref[0])
    def _():
        xb = xs_ref[...]
        gt = jnp.minimum(jnp.dot(xb, wgb[...], preferred_element_type=f32) + bg_ref[...], LIMIT)
        up = jnp.clip(jnp.dot(xb, wub[...], preferred_element_type=f32) + bu_ref[...],
                      -LIMIT, LIMIT)
        act = (up + 1.0) * (gt * jax.nn.sigmoid(gt * ALPHA))
        ys_ref[...] = (jnp.dot(act.astype(bf16), wdb[...], preferred_element_type=f32)
                       + bd_ref[...]).astype(ys_ref.dtype)

    @pl.when(i >= nu_ref[0])
    def _():
        ys_ref[...] = jnp.zeros_like(ys_ref)


def _experts(blk_e, n_used, first, xs, wg, bg, wu, bu, wd, bd):
    p = xs.shape[0]
    wspec = pl.BlockSpec((None, D, D), lambda i, be, nu, fx: (be[i], 0, 0))
    bspec = pl.BlockSpec((None, 1, D), lambda i, be, nu, fx: (be[i], 0, 0))
    xrows = pl.BlockSpec((EBLK, D), lambda i, be, nu, fx: (jnp.where(i < nu[0], i, 0), 0))
    yrows = pl.BlockSpec((EBLK, D), lambda i, be, nu, fx: (i, 0))
    return pl.pallas_call(
        _expert_kernel,
        name="experts",
        grid_spec=pltpu.PrefetchScalarGridSpec(
            num_scalar_prefetch=3, grid=(p // EBLK,),
            in_specs=[xrows, wspec, bspec, wspec, bspec, wspec, bspec],
            out_specs=yrows,
            scratch_shapes=[pltpu.VMEM((D, D), bf16)] * 3),
        out_shape=jax.ShapeDtypeStruct((p, D), bf16),
        compiler_params=_cparams(("arbitrary",)),
    )(blk_e, n_used, first, xs, wg, bg, wu, bu, wd, bd)


def _combine_kernel(cnt_ref, run_ref, off_ref, tot_ref, ys_ref, pos_ref, w_ref, h_ref, g_ref, o_ref,
                    rbuf, sem):
    i = pl.program_id(0)
    cur = i % 2

    def fetch(tile, half):
        def slab(e, rows):
            src = pl.multiple_of(run_ref[tile * NE + e], ALIGN)
            dst = pl.multiple_of(off_ref[tile * NE + e], ALIGN)
            return pltpu.make_async_copy(ys_ref.at[pl.ds(src, rows)],
                                         rbuf.at[half, pl.ds(dst, rows)], sem.at[half])
        return slab

    @pl.when(i == 0)
    def _():
        rbuf[...] = jnp.zeros_like(rbuf)
        _run_copies(0, cnt_ref, fetch(0, 0), lambda cp: cp.start())

    @pl.when(i + 1 < pl.num_programs(0))
    def _():
        _run_copies(i + 1, cnt_ref, fetch(i + 1, 1 - cur), lambda cp: cp.start())

    n = pl.multiple_of(tot_ref[i], ALIGN)
    pltpu.make_async_copy(ys_ref.at[pl.ds(0, n)], rbuf.at[cur, pl.ds(0, n)], sem.at[cur]).wait()

    pos = pos_ref[...]
    w = w_ref[...]
    acc = h_ref[...]
    for c in range(0, SB, SCHUNK):
        slot = lax.broadcasted_iota(jnp.int32, (TT, SCHUNK), 1) + c
        gate = jnp.zeros((TT, SCHUNK), f32)
        for kk in range(TOPK):
            gate = jnp.where(slot == pos[:, kk:kk + 1], w[:, kk:kk + 1], gate)
        acc = acc + jnp.dot(gate.astype(bf16), rbuf[cur, c:c + SCHUNK], preferred_element_type=f32)
    o_ref[...] = acc * lax.rsqrt(jnp.mean(acc * acc, axis=-1, keepdims=True) + NORM_EPS) * g_ref[...]


def _combine(cnt16, run_start, off16, tot, ys, pos_t, gates_t, hres, lnf):
    t = hres.shape[0]
    return pl.pallas_call(
        _combine_kernel,
        name="combine",
        grid_spec=pltpu.PrefetchScalarGridSpec(
            num_scalar_prefetch=4, grid=(t // TT,),
            in_specs=[pl.BlockSpec(memory_space=pl.ANY),
                      pl.BlockSpec((TT, TOPK), lambda i, *_: (i, 0)),
                      pl.BlockSpec((TT, TOPK), lambda i, *_: (i, 0)),
                      pl.BlockSpec((TT, D), lambda i, *_: (i, 0)),
                      pl.BlockSpec((1, D), lambda i, *_: (0, 0))],
            out_specs=pl.BlockSpec((TT, D), lambda i, *_: (i, 0)),
            scratch_shapes=[pltpu.VMEM((2, SB, D), bf16), pltpu.SemaphoreType.DMA((2,))]),
        out_shape=jax.ShapeDtypeStruct((t, D), f32),
        compiler_params=_cparams(("arbitrary",)),
    )(cnt16, run_start, off16, tot, ys, pos_t, gates_t, hres, lnf)


def _row_tile(n, want):
    t = min(n, want)
    assert n % t == 0
    return t


def _mixer_inputs(x2d, seq_len, init, p):
    n = x2d.shape[0]
    rkv, cv, gt, lo = _inproj(x2d, p["ln1"], p["w_in"], _row_tile(n, 512))
    outs = _mix(rkv, cv, lo, init, p["vec_mix"], p["ww2"], p["wa2"], p["wg2"], p["cw"], p["seg"],
                seq_len, _row_tile(seq_len, 256))
    return (rkv, cv, lo), gt, outs


def kernel(x, meta_tokens, ln1_g, w_in, mu_r, mu_k, mu_v, mu_w, mu_a, mu_g, w0, w_w1, w_w2, a0, w_a1, w_a2, w_g1, w_g2, k_k, k_a, r_k, lnx_g, lnx_b, w_o_rwkv, conv_w, w_o_conv, w_o, ln2_g, w_router, b_router, w_e_gate, b_e_gate, w_e_up, b_e_up, w_e_down, b_e_down, lnf_g):
    nb, seq, _ = x.shape
    t = nb * seq
    assert ln1_g.shape[0] == 1, "single layer"

    muw, mua, mug = mu_w[0][:, None], mu_a[0][:, None], mu_g[0][:, None]
    lora_cur = jnp.concatenate([(1 - muw) * w_w1[0], (1 - mua) * w_a1[0], (1 - mug) * w_g1[0]], axis=1)
    lora_prev = jnp.concatenate([muw * w_w1[0], mua * w_a1[0], mug * w_g1[0]], axis=1)
    zrow = jnp.zeros((1, R), f32)
    p = {
        "ln1": ln1_g[0][None, :],
        "w_in": jnp.concatenate([w_in[0], lora_cur, lora_prev], axis=1).astype(bf16),
        "vec_mix": jnp.concatenate([mu_r, mu_k, mu_v, w0, a0, k_k, k_a, zrow], axis=0),
        "ww2": w_w2[0].astype(bf16), "wa2": w_a2[0].astype(bf16), "wg2": w_g2[0].astype(bf16),
        "cw": jnp.concatenate([conv_w[0], jnp.zeros((5, R), f32)], axis=0),
        "seg": (jnp.arange(R)[:, None] // HS == jnp.arange(R)[None, :] // HS).astype(bf16),
    }
    vec_merge = jnp.concatenate([lnx_g, lnx_b, r_k, jnp.zeros((5, R), f32)], axis=0)

    zero_init = (jnp.zeros((HALO, 3 * R), bf16), jnp.zeros((HALO, 3 * R), bf16),
                 jnp.zeros((HALO, 2 * LORA), bf16))
    meta_proj, _, meta_ops = _mixer_inputs(meta_tokens.astype(f32), NMETA, zero_init, p)
    pad = lambda z: jnp.pad(z, ((CHUNK - NMETA, 0), (0, 0)))[None]
    r_m, k_m, v_m, kk_m, b_m, lw_m = (pad(z) for z in meta_ops[:6])
    _, s_meta = _wkv(r_m, k_m, v_m, kk_m, b_m, lw_m, jnp.zeros((NH, HS, HS), f32), 1)

    x2d = x.reshape(t, D)
    _, gt, ops = _mixer_inputs(x2d, seq, meta_proj, p)
    r, k, v, kk, b, lw, g, cb = ops
    as3 = lambda z: z.reshape(nb, seq, R)
    y, _ = _wkv(as3(r), as3(k), as3(v), as3(kk), as3(b), as3(lw), s_meta[0], WKV_SEQS)
    hres, x2, logits_t = _merge(
        y.reshape(t, R), r, k, v, g, cb, gt, x2d, vec_merge, ln2_g[0][None, :], p["seg"],
        w_o_rwkv[0].astype(bf16), w_o_conv[0].astype(bf16), w_o[0].astype(bf16),
        w_router[0].T, b_router[0][:, None], _row_tile(t, 512))

    assert t % TT == 0
    nt = t // TT
    top_e, gates, lrank, cnt = _route(logits_t)
    cnt = cnt[:, :, 0]
    cnt16 = ((cnt + ALIGN - 1) // ALIGN) * ALIGN
    base16 = jnp.cumsum(cnt16, axis=0) - cnt16
    tot16 = jnp.sum(cnt16, axis=0)
    padded = ((tot16 + EBLK - 1) // EBLK) * EBLK
    pend = jnp.cumsum(padded)
    pstart = pend - padded
    run_start = pstart[None, :] + base16
    off16 = jnp.cumsum(cnt16, axis=1) - cnt16
    eids = jnp.arange(NE, dtype=jnp.int32)[:, None, None, None]
    te = top_e.reshape(TOPK, nt, TT)
    pos = lrank + jnp.sum(jnp.where(te[None] == eids, off16.T[:, None, :, None], 0), axis=0).reshape(TOPK, t)
    n_rows = -(-(t * TOPK + nt * NE * (ALIGN - 1) + NE * EBLK) // EBLK) * EBLK
    n_blk = n_rows // EBLK
    blk_start = jnp.arange(n_blk, dtype=jnp.int32) * EBLK
    blk_e = jnp.minimum(jnp.sum(pend[None, :] <= blk_start[:, None], axis=1), NE - 1).astype(jnp.int32)
    n_used = (pend[NE - 1:] // EBLK).astype(jnp.int32)
    tail = jnp.concatenate([jnp.where(padded > 0, pend - EBLK, -1), n_used]).astype(jnp.int32)
    flat = lambda z: z.reshape(-1).astype(jnp.int32)
    tot = jnp.sum(cnt16, axis=1).astype(jnp.int32)
    xs = _dispatch(flat(cnt16), flat(run_start), flat(off16), tot, tail, pos, x2, n_rows)
    first = jnp.concatenate([jnp.ones((1,), jnp.int32), (blk_e[1:] != blk_e[:-1]).astype(jnp.int32)])
    ys = _experts(blk_e, n_used, first, xs,
                  w_e_gate[0], b_e_gate[0][:, None, :],
                  w_e_up[0], b_e_up[0][:, None, :],
                  w_e_down[0], b_e_down[0][:, None, :])
    out = _combine(flat(cnt16), flat(run_start), flat(off16), tot, ys, pos.T, gates.T, hres, lnf_g[None, :])
    return out.reshape(nb, seq, D)
```

```python
import functools

import jax
import jax.numpy as jnp
from jax import lax
from jax.experimental import pallas as pl
from jax.experimental.pallas import tpu as pltpu

D = 1024
R = 512
NH = 8
HS = 64
NE = 32
TOPK = 4
NMETA = 16
CHUNK = 64
WKV_SEQS = 4
EBLK = 512
TT = 512
ALIGN = 16
SB = 4 * TT + 512
MERGE_SUB = 256
SCHUNK = 512
NORM_EPS = 1e-5
LNX_EPS = 64e-5
ALPHA = 1.702
LIMIT = 7.0
LORA_W, LORA_A, LORA_G = 64, 64, 128
LORA = LORA_W + LORA_A + LORA_G
NPROJ = 3 * R + 3 * R + 2 * D + 2 * LORA
VMEM_LIMIT = 56 * 1024 * 1024

f32 = jnp.float32
bf16 = jnp.bfloat16


def _bdot(a, b):
    return jnp.dot(a.astype(bf16), b.astype(bf16), preferred_element_type=f32)


def _bdot_nt(a, b):
    return lax.dot_general(a.astype(bf16), b.astype(bf16), (((1,), (1,)), ((), ())),
                           preferred_element_type=f32)


def _bdot_tn(a, b):
    return lax.dot_general(a.astype(bf16), b.astype(bf16), (((0,), (0,)), ((), ())),
                           preferred_element_type=f32)


def _split3(x):
    h = x.astype(bf16)
    r1 = x - h.astype(f32)
    m = r1.astype(bf16)
    l = (r1 - m.astype(f32)).astype(bf16)
    return h, m, l


def _head_sum(x, seg):
    h = x.astype(bf16)
    l = (x - h.astype(f32)).astype(bf16)
    dot = functools.partial(jnp.dot, preferred_element_type=f32)
    return dot(h, seg) + dot(l, seg)


def _cparams(sem):
    return pltpu.CompilerParams(dimension_semantics=sem, vmem_limit_bytes=VMEM_LIMIT)


MARGIN = 16
PW = 3 * R + 3 * R + 2 * LORA


def _mix_rows(pbuf, a, n, vec, cw, seg, ww2, wa2, wg2, outs, o):
    r_out, k_out, v_out, kk_out, b_out, lw_out, g_out, cb_out = outs
    mu_r, mu_k, mu_v = vec[0:1, :], vec[1:2, :], vec[2:3, :]
    w0, a0, k_k, k_a = vec[3:4, :], vec[4:5, :], vec[5:6, :], vec[6:7, :]
    cur = pbuf[a:a + n, :]
    prev = pbuf[a - 1:a - 1 + n, :]
    r = cur[:, :R] + (prev[:, :R] - cur[:, :R]) * mu_r
    k = cur[:, R:2 * R] + (prev[:, R:2 * R] - cur[:, R:2 * R]) * mu_k
    v = cur[:, 2 * R:3 * R] + (prev[:, 2 * R:3 * R] - cur[:, 2 * R:3 * R]) * mu_v
    lo = 6 * R
    mixed = cur[:, lo:lo + LORA] + prev[:, lo + LORA:lo + 2 * LORA]
    hw = jnp.tanh(mixed[:, :LORA_W])
    ha = mixed[:, LORA_W:LORA_W + LORA_A]
    hg = jax.nn.sigmoid(mixed[:, LORA_W + LORA_A:])
    w_log = -jax.nn.softplus(-(w0 + _bdot(hw, ww2))) - 0.5
    aa = jax.nn.sigmoid(a0 + _bdot(ha, wa2))
    g = _bdot(hg, wg2)
    kk = k * k_k
    norm = jnp.sqrt(_head_sum(kk * kk, seg))
    kk = kk / jnp.maximum(norm, 1e-12)
    k = k * (1.0 + (aa - 1.0) * k_a)
    rows = slice(o, o + n)
    r_out[rows, :] = r.astype(r_out.dtype)
    k_out[rows, :] = k.astype(k_out.dtype)
    v_out[rows, :] = v.astype(v_out.dtype)
    kk_out[rows, :] = kk.astype(kk_out.dtype)
    b_out[rows, :] = (kk * aa).astype(b_out.dtype)
    lw_out[rows, :] = -jnp.exp(w_log)
    g_out[rows, :] = g.astype(g_out.dtype)
    prev2 = pbuf[a - 2:a - 2 + n, 4 * R:6 * R]
    u0 = cur[:, 4 * R:5 * R] * cur[:, 5 * R:6 * R]
    u1 = prev[:, 4 * R:5 * R] * prev[:, 5 * R:6 * R]
    u2 = prev2[:, :R] * prev2[:, R:]
    conv = cw[2:3, :] * u0 + cw[1:2, :] * u1 + cw[0:1, :] * u2
    cb_out[rows, :] = (cur[:, 3 * R:4 * R] * conv).astype(cb_out.dtype)


def _projmix_kernel(tiles_per_seq, group, x_ref, g_ref, w_ref, init_ref, vec_ref, ww2_ref, wa2_ref,
                    wg2_ref, cw_ref, seg_ref, gt_out, r_out, k_out, v_out, kk_out, b_out, lw_out,
                    g_out, cb_out, tail_out, pbuf_a, pbuf_b):
    i = pl.program_id(0)
    tm = x_ref.shape[0]

    @pl.when(i == 0)
    def _():
        pbuf_a[...] = jnp.zeros_like(pbuf_a)
        pbuf_b[...] = jnp.zeros_like(pbuf_b)

    def step(pcur, pprv):
        first = (i % tiles_per_seq) == 0
        pcur[0:MARGIN, :] = jnp.where(first, init_ref[...], pprv[tm:tm + MARGIN, :])
        tail_out[...] = pprv[tm:tm + MARGIN, :]

        x = x_ref[...]
        xb = (x * lax.rsqrt(jnp.mean(x * x, axis=-1, keepdims=True) + NORM_EPS)
              * g_ref[...]).astype(bf16)
        outs = (r_out, k_out, v_out, kk_out, b_out, lw_out, g_out, cb_out)
        vec, cw, seg = vec_ref[...], cw_ref[...], seg_ref[...]
        ww2, wa2, wg2 = ww2_ref[...], wa2_ref[...], wg2_ref[...]

        def project(c):
            res = jnp.dot(xb, w_ref[:, c:c + 512], preferred_element_type=f32)
            if c < PW:
                pcur[MARGIN:MARGIN + tm, c:c + 512] = res
            else:
                gt_out[:, c - PW:c - PW + 512] = res.astype(gt_out.dtype)

        cols = list(range(0, NPROJ, 512))
        groups = list(range(0, tm, group))
        per = -(-len(cols) // len(groups))
        for n, o in enumerate(groups):
            for c in cols[n * per:(n + 1) * per]:
                project(c)
            _mix_rows(pprv, MARGIN + o, group, vec, cw, seg, ww2, wa2, wg2, outs, o)

    @pl.when(i % 2 == 0)
    def _():
        step(pbuf_a, pbuf_b)

    @pl.when(i % 2 == 1)
    def _():
        step(pbuf_b, pbuf_a)


def _projmix(x2d, seq_len, init, p, tm):
    t = x2d.shape[0]
    n = t // tm
    group = min(tm, 64)

    def whole(shape):
        return pl.BlockSpec(shape, lambda i: (0,) * len(shape))

    lag = pl.BlockSpec((tm, R), lambda i: (jnp.maximum(i - 1, 0), 0))
    return pl.pallas_call(
        functools.partial(_projmix_kernel, seq_len // tm, group),
        name="projmix",
        grid=(n + 1,),
        in_specs=[pl.BlockSpec((tm, D), lambda i: (jnp.minimum(i, n - 1), 0)),
                  whole((1, D)),
                  pl.BlockSpec((D, NPROJ), lambda i: (0, 0), pipeline_mode=pl.Buffered(1)),
                  whole((MARGIN, PW)), whole((8, R)), whole((LORA_W, R)), whole((LORA_A, R)),
                  whole((LORA_G, R)), whole((8, R)), whole((R, R))],
        out_specs=[pl.BlockSpec((tm, 2 * D), lambda i: (jnp.minimum(i, n - 1), 0))] + [lag] * 8
                  + [whole((MARGIN, PW))],
        out_shape=[jax.ShapeDtypeStruct((t, 2 * D), bf16)]
                  + [jax.ShapeDtypeStruct((t, R), f32 if m == 5 else bf16) for m in range(8)]
                  + [jax.ShapeDtypeStruct((MARGIN, PW), f32)],
        scratch_shapes=[pltpu.VMEM((MARGIN + tm, PW), f32)] * 2,
        compiler_params=_cparams(("arbitrary",)),
    )(x2d, p["ln1"], p["w_in"], init, p["vec_mix"], p["ww2"], p["wa2"], p["wg2"], p["cw"], p["seg"])


def _wkv_kernel(r_ref, k_ref, v_ref, kk_ref, b_ref, lw_ref, s0_ref, y_ref, sT_ref, s_scr):
    c = pl.program_id(1)
    nbw = r_ref.shape[0]

    @pl.when(c == 0)
    def _():
        for j in range(nbw):
            s_scr[j] = s0_ref[...]

    C = CHUNK
    ri = lax.broadcasted_iota(jnp.int32, (C, C), 0)
    ci = lax.broadcasted_iota(jnp.int32, (C, C), 1)
    incl = ri >= ci
    strict = ri > ci
    ri2 = lax.broadcasted_iota(jnp.int32, (C, 2 * C), 0)
    ci2 = lax.broadcasted_iota(jnp.int32, (C, 2 * C), 1)
    incl2 = ri2 >= jnp.where(ci2 >= C, ci2 - C, ci2)
    tri = jnp.where(incl, 1.0, 0.0).astype(bf16)
    eye = jnp.where(ri == ci, 1.0, 0.0).astype(f32)
    dot = functools.partial(jnp.dot, preferred_element_type=f32)

    at, rt, bt, kt, bh, kh, vv, g_tot = [], [], [], [], [], [], [], []
    for j in range(nbw):
        lw = lw_ref[j]
        h, m, l = _split3(lw)
        cum = dot(tri, h) + dot(tri, m) + dot(tri, l)
        tot = cum[C - 1:C, :]
        k = k_ref[j].astype(f32)
        b = b_ref[j].astype(f32)
        g_inv = jnp.exp(-cum)
        tail = jnp.exp(tot - cum)
        rt.append(r_ref[j].astype(f32) * jnp.exp(cum))
        kt.append(k * g_inv)
        bt.append(b * g_inv)
        at.append(-kk_ref[j].astype(f32) * jnp.exp(cum - lw))
        bh.append(b * tail)
        kh.append(k * tail)
        vv.append(v_ref[j].astype(f32))
        g_tot.append(jnp.exp(tot))

    chains = [(j, hd) for j in range(nbw) for hd in range(NH)]
    cs = range(len(chains))
    sl = lambda hd: slice(hd * HS, (hd + 1) * HS)
    s_old = [s_scr[j, hd] for j, hd in chains]
    lhs = [jnp.concatenate([at[j][:, sl(hd)], rt[j][:, sl(hd)]], axis=0).astype(bf16)
           for j, hd in chains]
    rhs = [jnp.concatenate([bt[j][:, sl(hd)], kt[j][:, sl(hd)]], axis=0).astype(bf16)
           for j, hd in chains]
    amat = [_bdot_nt(lhs[n], rhs[n]) for n in cs]
    a_ab = [jnp.where(strict, amat[n][:C, :C], 0.0) for n in cs]
    a_ak = [jnp.where(strict, amat[n][:C, C:], 0.0) for n in cs]
    a_r = [jnp.where(incl2, amat[n][C:, :], 0.0) for n in cs]
    x = [eye + a_ab[n] for n in cs]
    p = a_ab
    for _ in range(5):
        pb = [p[n].astype(bf16) for n in cs]
        p = [dot(pb[n], pb[n]) for n in cs]
        x = [x[n] + _bdot(x[n], p[n]) for n in cs]
    vh = [vv[j][:, sl(hd)] for j, hd in chains]
    sh = [_bdot_nt(lhs[n], s_old[n]) for n in cs]
    av = [_bdot(a_ak[n], vh[n]) for n in cs]
    u = [_bdot(x[n], sh[n][:C] + av[n]) for n in cs]
    uv = [jnp.concatenate([u[n], vh[n]], axis=0).astype(bf16) for n in cs]
    ys = [sh[n][C:] + _bdot(a_r[n], uv[n]) for n in cs]
    bk = [jnp.concatenate([bh[j][:, sl(hd)], kh[j][:, sl(hd)]], axis=0) for j, hd in chains]
    s_new = [s_old[n] * g_tot[j][:, sl(hd)] + _bdot_tn(uv[n], bk[n])
             for n, (j, hd) in enumerate(chains)]
    for n, (j, hd) in enumerate(chains):
        s_scr[j, hd] = s_new[n]
    for j in range(nbw):
        y_ref[j] = jnp.concatenate(ys[j * NH:(j + 1) * NH], axis=1)

    @pl.when(c == pl.num_programs(1) - 1)
    def _():
        sT_ref[...] = s_scr[...]


def _wkv(r, k, v, kk, b, lw, s0, nbw):
    nb, seq, _ = r.shape
    assert nb % nbw == 0 and seq % CHUNK == 0
    blk = pl.BlockSpec((nbw, CHUNK, R), lambda i, c: (i, c, 0))
    return pl.pallas_call(
        _wkv_kernel,
        name="wkv",
        grid=(nb // nbw, seq // CHUNK),
        in_specs=[blk] * 6 + [pl.BlockSpec((NH, HS, HS), lambda i, c: (0, 0, 0))],
        out_specs=[blk, pl.BlockSpec((nbw, NH, HS, HS), lambda i, c: (i, 0, 0, 0))],
        out_shape=[jax.ShapeDtypeStruct((nb, seq, R), f32),
                   jax.ShapeDtypeStruct((nb, NH, HS, HS), f32)],
        scratch_shapes=[pltpu.VMEM((nbw, NH, HS, HS), f32)],
        compiler_params=_cparams(("parallel", "arbitrary")),
    )(r, k, v, kk, b, lw, s0)


def _merge_kernel(y_ref, r_ref, k_ref, v_ref, g_ref, cb_ref, gt_ref, x_ref, vec_ref, ln2_ref,
                  seg_ref, worw_ref, woc_ref, wo_ref, wr_ref, br_ref,
                  h_out, x2_out, lg_out):
    seg = seg_ref[...]
    vec = vec_ref[...]
    lnx_g, lnx_b, r_k = vec[0:1, :], vec[1:2, :], vec[2:3, :]
    tm = y_ref.shape[0]
    rows = [slice(s0, s0 + MERGE_SUB) for s0 in range(0, tm, MERGE_SUB)]
    gs = range(len(rows))
    y = [y_ref[sl, :] for sl in rows]
    mean = [_head_sum(y[n], seg) * (1.0 / HS) for n in gs]
    yc = [y[n] - mean[n] for n in gs]
    var = [_head_sum(yc[n] * yc[n], seg) * (1.0 / HS) for n in gs]
    yn = [yc[n] * lax.rsqrt(var[n] + LNX_EPS) * lnx_g + lnx_b for n in gs]
    rk = [_head_sum(r_ref[sl, :].astype(f32) * k_ref[sl, :].astype(f32) * r_k, seg) for sl in rows]
    z = [(yn[n] + rk[n] * v_ref[rows[n], :].astype(f32)) * g_ref[rows[n], :].astype(f32) for n in gs]
    y_a = [_bdot(z[n], worw_ref[...]) for n in gs]
    y_b = [_bdot(cb_ref[sl, :], woc_ref[...]) for sl in rows]
    merged = [jax.nn.sigmoid(gt_ref[rows[n], :D].astype(f32)) * y_a[n]
              + jax.nn.sigmoid(gt_ref[rows[n], D:].astype(f32)) * y_b[n] for n in gs]
    hres = [x_ref[rows[n], :] + _bdot(merged[n], wo_ref[...]) for n in gs]
    x2 = [hres[n] * lax.rsqrt(jnp.mean(hres[n] * hres[n], axis=-1, keepdims=True) + NORM_EPS)
          * ln2_ref[...] for n in gs]
    wh, wm, _ = _split3(wr_ref[...])
    nt = functools.partial(lax.dot_general, dimension_numbers=(((1,), (1,)), ((), ())),
                           preferred_element_type=f32)
    for n, sl in enumerate(rows):
        h_out[sl, :] = hres[n]
        x2_out[sl, :] = x2[n].astype(x2_out.dtype)
        xh, xm, _ = _split3(x2[n])
        lg_out[:, sl] = nt(wh, xh) + nt(wh, xm) + nt(wm, xh) + br_ref[...]


def _merge(y, r, k, v, g, cb, gt, x2d, vec, ln2, seg, worw, woc, wo, wr_t, br, tm):
    t = y.shape[0]

    def rows(n):
        return pl.BlockSpec((tm, n), lambda i: (i, 0))

    def whole(shape):
        return pl.BlockSpec(shape, lambda i: (0,) * len(shape))

    return pl.pallas_call(
        _merge_kernel,
        name="merge",
        grid=(t // tm,),
        in_specs=[rows(R)] * 6 + [rows(2 * D), rows(D), whole((8, R)), whole((1, D)),
                                   whole((R, R)), whole((R, D)), whole((R, D)), whole((D, D)),
                                   whole((NE, D)), whole((NE, 1))],
        out_specs=[rows(D), rows(D), pl.BlockSpec((NE, tm), lambda i: (0, i))],
        out_shape=[jax.ShapeDtypeStruct((t, D), f32), jax.ShapeDtypeStruct((t, D), bf16),
                   jax.ShapeDtypeStruct((NE, t), f32)],
        compiler_params=_cparams(("parallel",)),
    )(y, r, k, v, g, cb, gt, x2d, vec, ln2, seg, worw, woc, wo, wr_t, br)


def _route_kernel(lg_ref, e_out, w_out, rank_out, cnt_out):
    lg = lg_ref[...]
    tr = lg.shape[1]
    erow = lax.broadcasted_iota(jnp.int32, lg.shape, 0)
    work = lg
    hits, vals, idxs = [], [], []
    for _ in range(TOPK):
        m = jnp.max(work, axis=0, keepdims=True)
        idx = jnp.min(jnp.where(work == m, erow, NE), axis=0, keepdims=True)
        hit = erow == idx
        hits.append(hit)
        vals.append(m)
        idxs.append(idx)
        work = jnp.where(hit, -jnp.inf, work)
    ex = [jnp.exp(vk - vals[0]) for vk in vals]
    den = ex[0] + ex[1] + ex[2] + ex[3]
    multi = jnp.where(hits[0] | hits[1] | hits[2] | hits[3], 1.0, 0.0)
    ti = lax.broadcasted_iota(jnp.int32, (tr, tr), 0)
    tj = lax.broadcasted_iota(jnp.int32, (tr, tr), 1)
    before = jnp.where(ti < tj, 1.0, 0.0).astype(bf16)
    excl = jnp.dot(multi.astype(bf16), before, preferred_element_type=f32)
    for kk in range(TOPK):
        e_out[kk:kk + 1, :] = idxs[kk]
        w_out[kk:kk + 1, :] = ex[kk] / den
        rank_out[kk:kk + 1, :] = jnp.sum(jnp.where(hits[kk], excl, 0.0), axis=0,
                                         keepdims=True).astype(jnp.int32)
    total = jnp.sum(multi, axis=1, keepdims=True)
    cnt_out[...] = jnp.broadcast_to(total, cnt_out.shape).astype(jnp.int32)


def _route(logits_t):
    t = logits_t.shape[1]
    sel = pl.BlockSpec((TOPK, TT), lambda i: (0, i))
    return pl.pallas_call(
        _route_kernel,
        name="route",
        grid=(t // TT,),
        in_specs=[pl.BlockSpec((NE, TT), lambda i: (0, i))],
        out_specs=[sel, sel, sel, pl.BlockSpec((None, NE, 128), lambda i: (i, 0, 0))],
        out_shape=[jax.ShapeDtypeStruct((TOPK, t), jnp.int32),
                   jax.ShapeDtypeStruct((TOPK, t), f32),
                   jax.ShapeDtypeStruct((TOPK, t), jnp.int32),
                   jax.ShapeDtypeStruct((t // TT, NE, 128), jnp.int32)],
        compiler_params=_cparams(("parallel",)),
    )(logits_t)


def _run_copies(tile, cnt_ref, make_copy, act):
    def per_expert(e, carry):
        cnt = pl.multiple_of(cnt_ref[tile * NE + e], ALIGN)

        @pl.when(cnt > 0)
        def _():
            act(make_copy(e, cnt))
        return carry
    lax.fori_loop(0, NE, per_expert, 0)


def _dispatch_kernel(cnt_ref, run_ref, off_ref, tot_ref, tail_ref, pos_ref, x2_ref, xs_ref,
                     sbuf, zbuf, sem, zsem):
    i = pl.program_id(0)
    last = pl.num_programs(0) - 1
    cur = i % 2

    @pl.when(i == 0)
    def _():
        zbuf[...] = jnp.zeros_like(zbuf)

        def zero_copy(start):
            return pltpu.make_async_copy(zbuf, xs_ref.at[pl.ds(pl.multiple_of(start, EBLK), EBLK)], zsem)

        for e in range(NE):
            @pl.when(tail_ref[e] >= 0)
            def _():
                zero_copy(tail_ref[e]).start()
        for e in range(NE):
            @pl.when(tail_ref[e] >= 0)
            def _():
                zero_copy(tail_ref[e]).wait()
        n_blk = xs_ref.shape[0] // EBLK
        lax.fori_loop(tail_ref[NE], n_blk, lambda blk, c: (zero_copy(blk * EBLK).start(), c)[1], 0)
        lax.fori_loop(tail_ref[NE], n_blk, lambda blk, c: (zero_copy(blk * EBLK).wait(), c)[1], 0)

    def slab(tile, half):
        def make(e, rows):
            src = pl.multiple_of(off_ref[tile * NE + e], ALIGN)
            dst = pl.multiple_of(run_ref[tile * NE + e], ALIGN)
            return pltpu.make_async_copy(sbuf.at[half, pl.ds(src, rows)],
                                         xs_ref.at[pl.ds(dst, rows)], sem.at[half])
        return make

    def drain(tile, half):
        n = pl.multiple_of(tot_ref[tile], ALIGN)
        pltpu.make_async_copy(sbuf.at[half, pl.ds(0, n)], xs_ref.at[pl.ds(0, n)], sem.at[half]).wait()

    @pl.when(i >= 2)
    def _():
        drain(i - 2, cur)

    pos = pos_ref[...]
    xt = x2_ref[...]
    for c in range(0, SB, SCHUNK):
        slot = lax.broadcasted_iota(jnp.int32, (SCHUNK, TT), 0) + c
        place = jnp.zeros((SCHUNK, TT), f32)
        for kk in range(TOPK):
            place = jnp.where(slot == pos[kk:kk + 1, :], 1.0, place)
        sbuf[cur, c:c + SCHUNK] = jnp.dot(place.astype(bf16), xt,
                                          preferred_element_type=f32).astype(bf16)
    _run_copies(i, cnt_ref, slab(i, cur), lambda cp: cp.start())

    @pl.when(i == last)
    def _():
        @pl.when(i >= 1)
        def _():
            drain(i - 1, 1 - cur)
        drain(i, cur)


def _dispatch(cnt16, run_start, off16, tot, tail, pos, x2, n_rows):
    t = x2.shape[0]
    return pl.pallas_call(
        _dispatch_kernel,
        name="dispatch",
        grid_spec=pltpu.PrefetchScalarGridSpec(
            num_scalar_prefetch=5, grid=(t // TT,),
            in_specs=[pl.BlockSpec((TOPK, TT), lambda i, *_: (0, i)),
                      pl.BlockSpec((TT, D), lambda i, *_: (i, 0))],
            out_specs=pl.BlockSpec(memory_space=pl.ANY),
            scratch_shapes=[pltpu.VMEM((2, SB, D), bf16), pltpu.VMEM((EBLK, D), bf16),
                            pltpu.SemaphoreType.DMA((2,)), pltpu.SemaphoreType.DMA(())]),
        out_shape=jax.ShapeDtypeStruct((n_rows, D), bf16),
        compiler_params=_cparams(("arbitrary",)),
    )(cnt16, run_start, off16, tot, tail, pos, x2)


def _expert_kernel(be_ref, nu_ref, fx_ref, xs_ref, wg_ref, bg_ref, wu_ref, bu_ref, wd_ref, bd_ref,
                   ys_ref, wgb, wub, wdb):
    del be_ref
    i = pl.program_id(0)

    @pl.when(fx_ref[i] == 1)
    def _():
        wgb[...] = wg_ref[...].astype(bf16)
        wub[...] = wu_ref[...].astype(bf16)
        wdb[...] = wd_ref[...].astype(bf16)

    @pl.when(i < nu_ref[0])
    def _():
        xb = xs_ref[...]
        gt = jnp.minimum(jnp.dot(xb, wgb[...], preferred_element_type=f32) + bg_ref[...], LIMIT)
        up = jnp.clip(jnp.dot(xb, wub[...], preferred_element_type=f32) + bu_ref[...],
                      -LIMIT, LIMIT)
        act = (up + 1.0) * (gt * jax.nn.sigmoid(gt * ALPHA))
        ys_ref[...] = (jnp.dot(act.astype(bf16), wdb[...], preferred_element_type=f32)
                       + bd_ref[...]).astype(ys_ref.dtype)

    @pl.when(i >= nu_ref[0])
    def _():
        ys_ref[...] = jnp.zeros_like(ys_ref)


def _experts(blk_e, n_used, first, xs, wg, bg, wu, bu, wd, bd):
    p = xs.shape[0]
    wspec = pl.BlockSpec((None, D, D), lambda i, be, nu, fx: (be[i], 0, 0))
    bspec = pl.BlockSpec((None, 1, D), lambda i, be, nu, fx: (be[i], 0, 0))
    xrows = pl.BlockSpec((EBLK, D), lambda i, be, nu, fx: (jnp.where(i < nu[0], i, 0), 0))
    yrows = pl.BlockSpec((EBLK, D), lambda i, be, nu, fx: (i, 0))
    return pl.pallas_call(
        _expert_kernel,
        name="experts",
        grid_spec=pltpu.PrefetchScalarGridSpec(
            num_scalar_prefetch=3, grid=(p // EBLK,),
            in_specs=[xrows, wspec, bspec, wspec, bspec, wspec, bspec],
            out_specs=yrows,
            scratch_shapes=[pltpu.VMEM((D, D), bf16)] * 3),
        out_shape=jax.ShapeDtypeStruct((p, D), bf16),
        compiler_params=_cparams(("arbitrary",)),
    )(blk_e, n_used, first, xs, wg, bg, wu, bu, wd, bd)


def _combine_kernel(cnt_ref, run_ref, off_ref, tot_ref, ys_ref, pos_ref, w_ref, h_ref, g_ref, o_ref,
                    rbuf, sem):
    i = pl.program_id(0)
    cur = i % 2

    def fetch(tile, half):
        def slab(e, rows):
            src = pl.multiple_of(run_ref[tile * NE + e], ALIGN)
            dst = pl.multiple_of(off_ref[tile * NE + e], ALIGN)
            return pltpu.make_async_copy(ys_ref.at[pl.ds(src, rows)],
                                         rbuf.at[half, pl.ds(dst, rows)], sem.at[half])
        return slab

    @pl.when(i == 0)
    def _():
        rbuf[...] = jnp.zeros_like(rbuf)
        _run_copies(0, cnt_ref, fetch(0, 0), lambda cp: cp.start())

    @pl.when(i + 1 < pl.num_programs(0))
    def _():
        _run_copies(i + 1, cnt_ref, fetch(i + 1, 1 - cur), lambda cp: cp.start())

    n = pl.multiple_of(tot_ref[i], ALIGN)
    pltpu.make_async_copy(ys_ref.at[pl.ds(0, n)], rbuf.at[cur, pl.ds(0, n)], sem.at[cur]).wait()

    pos = pos_ref[...]
    w = w_ref[...]
    acc = h_ref[...]
    for c in range(0, SB, SCHUNK):
        slot = lax.broadcasted_iota(jnp.int32, (TT, SCHUNK), 1) + c
        gate = jnp.zeros((TT, SCHUNK), f32)
        for kk in range(TOPK):
            gate = jnp.where(slot == pos[:, kk:kk + 1], w[:, kk:kk + 1], gate)
        acc = acc + jnp.dot(gate.astype(bf16), rbuf[cur, c:c + SCHUNK], preferred_element_type=f32)
    o_ref[...] = acc * lax.rsqrt(jnp.mean(acc * acc, axis=-1, keepdims=True) + NORM_EPS) * g_ref[...]


def _combine(cnt16, run_start, off16, tot, ys, pos_t, gates_t, hres, lnf):
    t = hres.shape[0]
    return pl.pallas_call(
        _combine_kernel,
        name="combine",
        grid_spec=pltpu.PrefetchScalarGridSpec(
            num_scalar_prefetch=4, grid=(t // TT,),
            in_specs=[pl.BlockSpec(memory_space=pl.ANY),
                      pl.BlockSpec((TT, TOPK), lambda i, *_: (i, 0)),
                      pl.BlockSpec((TT, TOPK), lambda i, *_: (i, 0)),
                      pl.BlockSpec((TT, D), lambda i, *_: (i, 0)),
                      pl.BlockSpec((1, D), lambda i, *_: (0, 0))],
            out_specs=pl.BlockSpec((TT, D), lambda i, *_: (i, 0)),
            scratch_shapes=[pltpu.VMEM((2, SB, D), bf16), pltpu.SemaphoreType.DMA((2,))]),
        out_shape=jax.ShapeDtypeStruct((t, D), f32),
        compiler_params=_cparams(("arbitrary",)),
    )(cnt16, run_start, off16, tot, ys, pos_t, gates_t, hres, lnf)


def _row_tile(n, want):
    t = min(n, want)
    assert n % t == 0
    return t


def kernel(x, meta_tokens, ln1_g, w_in, mu_r, mu_k, mu_v, mu_w, mu_a, mu_g, w0, w_w1, w_w2, a0, w_a1, w_a2, w_g1, w_g2, k_k, k_a, r_k, lnx_g, lnx_b, w_o_rwkv, conv_w, w_o_conv, w_o, ln2_g, w_router, b_router, w_e_gate, b_e_gate, w_e_up, b_e_up, w_e_down, b_e_down, lnf_g):
    nb, seq, _ = x.shape
    t = nb * seq
    assert ln1_g.shape[0] == 1, "single layer"

    muw, mua, mug = mu_w[0][:, None], mu_a[0][:, None], mu_g[0][:, None]
    lora_cur = jnp.concatenate([(1 - muw) * w_w1[0], (1 - mua) * w_a1[0], (1 - mug) * w_g1[0]], axis=1)
    lora_prev = jnp.concatenate([muw * w_w1[0], mua * w_a1[0], mug * w_g1[0]], axis=1)
    zrow = jnp.zeros((1, R), f32)
    p = {
        "ln1": ln1_g[0][None, :],
        "w_in": jnp.concatenate([w_in[0][:, :6 * R], lora_cur, lora_prev, w_in[0][:, 6 * R:]],
                                axis=1).astype(bf16),
        "vec_mix": jnp.concatenate([mu_r, mu_k, mu_v, w0, a0, k_k, k_a, zrow], axis=0),
        "ww2": w_w2[0].astype(bf16), "wa2": w_a2[0].astype(bf16), "wg2": w_g2[0].astype(bf16),
        "cw": jnp.concatenate([conv_w[0], jnp.zeros((5, R), f32)], axis=0),
        "seg": (jnp.arange(R)[:, None] // HS == jnp.arange(R)[None, :] // HS).astype(bf16),
    }
    vec_merge = jnp.concatenate([lnx_g, lnx_b, r_k, jnp.zeros((5, R), f32)], axis=0)

    meta_out = _projmix(meta_tokens.astype(f32), NMETA, jnp.zeros((MARGIN, PW), f32), p, NMETA)
    pad = lambda z: jnp.pad(z, ((CHUNK - NMETA, 0), (0, 0)))[None]
    r_m, k_m, v_m, kk_m, b_m, lw_m = (pad(z) for z in meta_out[1:7])
    _, s_meta = _wkv(r_m, k_m, v_m, kk_m, b_m, lw_m, jnp.zeros((NH, HS, HS), f32), 1)

    x2d = x.reshape(t, D)
    gt, r, k, v, kk, b, lw, g, cb, _ = _projmix(x2d, seq, meta_out[9], p, _row_tile(seq, 512))
    as3 = lambda z: z.reshape(nb, seq, R)
    y, _ = _wkv(as3(r), as3(k), as3(v), as3(kk), as3(b), as3(lw), s_meta[0], WKV_SEQS)
    hres, x2, logits_t = _merge(
        y.reshape(t, R), r, k, v, g, cb, gt, x2d, vec_merge, ln2_g[0][None, :], p["seg"],
        w_o_rwkv[0].astype(bf16), w_o_conv[0].astype(bf16), w_o[0].astype(bf16),
        w_router[0].T, b_router[0][:, None], _row_tile(t, 512))

    assert t % TT == 0
    nt = t // TT
    top_e, gates, lrank, cnt = _route(logits_t)
    cnt = cnt[:, :, 0]
    cnt16 = ((cnt + ALIGN - 1) // ALIGN) * ALIGN
    base16 = jnp.cumsum(cnt16, axis=0) - cnt16
    tot16 = jnp.sum(cnt16, axis=0)
    padded = ((tot16 + EBLK - 1) // EBLK) * EBLK
    pend = jnp.cumsum(padded)
    pstart = pend - padded
    run_start = pstart[None, :] + base16
    off16 = jnp.cumsum(cnt16, axis=1) - cnt16
    eids = jnp.arange(NE, dtype=jnp.int32)[:, None, None, None]
    te = top_e.reshape(TOPK, nt, TT)
    pos = lrank + jnp.sum(jnp.where(te[None] == eids, off16.T[:, None, :, None], 0), axis=0).reshape(TOPK, t)
    n_rows = -(-(t * TOPK + nt * NE * (ALIGN - 1) + NE * EBLK) // EBLK) * EBLK
    n_blk = n_rows // EBLK
    blk_start = jnp.arange(n_blk, dtype=jnp.int32) * EBLK
    blk_e = jnp.minimum(jnp.sum(pend[None, :] <= blk_start[:, None], axis=1), NE - 1).astype(jnp.int32)
    n_used = (pend[NE - 1:] // EBLK).astype(jnp.int32)
    tail = jnp.concatenate([jnp.where(padded > 0, pend - EBLK, -1), n_used]).astype(jnp.int32)
    flat = lambda z: z.reshape(-1).astype(jnp.int32)
    tot = jnp.sum(cnt16, axis=1).astype(jnp.int32)
    xs = _dispatch(flat(cnt16), flat(run_start), flat(off16), tot, tail, pos, x2, n_rows)
    first = jnp.concatenate([jnp.ones((1,), jnp.int32), (blk_e[1:] != blk_e[:-1]).astype(jnp.int32)])
    ys = _experts(blk_e, n_used, first, xs,
                  w_e_gate[0], b_e_gate[0][:, None, :],
                  w_e_up[0], b_e_up[0][:, None, :],
                  w_e_down[0], b_e_down[0][:, None, :])
    out = _combine(flat(cnt16), flat(run_start), flat(off16), tot, ys, pos.T, gates.T, hres, lnf_g[None, :])
    return out.reshape(nb, seq, D)
```

```python
import functools

import jax
import jax.numpy as jnp
from jax import lax
from jax.experimental import pallas as pl
from jax.experimental.pallas import tpu as pltpu

D = 1024
R = 512
NH = 8
HS = 64
NE = 32
TOPK = 4
NMETA = 16
CHUNK = 64
WKV_SEQS = 4
EBLK = 512
TT = 512
ALIGN = 16
SB = 4 * TT + 512
MERGE_SUB = 256
SCHUNK = 512
NORM_EPS = 1e-5
LNX_EPS = 64e-5
ALPHA = 1.702
LIMIT = 7.0
LORA_W, LORA_A, LORA_G = 64, 64, 128
LORA = LORA_W + LORA_A + LORA_G
NPROJ = 3 * R + 3 * R + 2 * D + 2 * LORA
VMEM_LIMIT = 56 * 1024 * 1024

f32 = jnp.float32
bf16 = jnp.bfloat16


def _bdot(a, b):
    return jnp.dot(a.astype(bf16), b.astype(bf16), preferred_element_type=f32)


def _bdot_nt(a, b):
    return lax.dot_general(a.astype(bf16), b.astype(bf16), (((1,), (1,)), ((), ())),
                           preferred_element_type=f32)


def _bdot_tn(a, b):
    return lax.dot_general(a.astype(bf16), b.astype(bf16), (((0,), (0,)), ((), ())),
                           preferred_element_type=f32)


def _split3(x):
    h = x.astype(bf16)
    r1 = x - h.astype(f32)
    m = r1.astype(bf16)
    l = (r1 - m.astype(f32)).astype(bf16)
    return h, m, l


def _head_sum(x, seg):
    h = x.astype(bf16)
    l = (x - h.astype(f32)).astype(bf16)
    dot = functools.partial(jnp.dot, preferred_element_type=f32)
    return dot(h, seg) + dot(l, seg)


def _cparams(sem):
    return pltpu.CompilerParams(dimension_semantics=sem, vmem_limit_bytes=VMEM_LIMIT)


MARGIN = 16
PW = 3 * R + 3 * R + 2 * LORA


def _mix_rows(pbuf, a, n, vec, cw, seg, ww2, wa2, wg2, outs, o):
    r_out, k_out, v_out, kk_out, b_out, lw_out, g_out, cb_out = outs
    mu_r, mu_k, mu_v = vec[0:1, :], vec[1:2, :], vec[2:3, :]
    w0, a0, k_k, k_a = vec[3:4, :], vec[4:5, :], vec[5:6, :], vec[6:7, :]
    cur = pbuf[a:a + n, :]
    prev = pbuf[a - 1:a - 1 + n, :]
    r = cur[:, :R] + (prev[:, :R] - cur[:, :R]) * mu_r
    k = cur[:, R:2 * R] + (prev[:, R:2 * R] - cur[:, R:2 * R]) * mu_k
    v = cur[:, 2 * R:3 * R] + (prev[:, 2 * R:3 * R] - cur[:, 2 * R:3 * R]) * mu_v
    lo = 6 * R
    mixed = cur[:, lo:lo + LORA] + prev[:, lo + LORA:lo + 2 * LORA]
    hw = jnp.tanh(mixed[:, :LORA_W])
    ha = mixed[:, LORA_W:LORA_W + LORA_A]
    hg = jax.nn.sigmoid(mixed[:, LORA_W + LORA_A:])
    w_log = -jax.nn.softplus(-(w0 + _bdot(hw, ww2))) - 0.5
    aa = jax.nn.sigmoid(a0 + _bdot(ha, wa2))
    g = _bdot(hg, wg2)
    kk = k * k_k
    norm = jnp.sqrt(_head_sum(kk * kk, seg))
    kk = kk / jnp.maximum(norm, 1e-12)
    k = k * (1.0 + (aa - 1.0) * k_a)
    rows = slice(o, o + n)
    r_out[rows, :] = r.astype(r_out.dtype)
    k_out[rows, :] = k.astype(k_out.dtype)
    v_out[rows, :] = v.astype(v_out.dtype)
    kk_out[rows, :] = kk.astype(kk_out.dtype)
    b_out[rows, :] = (kk * aa).astype(b_out.dtype)
    lw_out[rows, :] = -jnp.exp(w_log)
    g_out[rows, :] = g.astype(g_out.dtype)
    prev2 = pbuf[a - 2:a - 2 + n, 4 * R:6 * R]
    u0 = cur[:, 4 * R:5 * R] * cur[:, 5 * R:6 * R]
    u1 = prev[:, 4 * R:5 * R] * prev[:, 5 * R:6 * R]
    u2 = prev2[:, :R] * prev2[:, R:]
    conv = cw[2:3, :] * u0 + cw[1:2, :] * u1 + cw[0:1, :] * u2
    cb_out[rows, :] = (cur[:, 3 * R:4 * R] * conv).astype(cb_out.dtype)


def _projmix_kernel(tiles_per_seq, group, x_ref, g_ref, w_ref, init_ref, vec_ref, ww2_ref, wa2_ref,
                    wg2_ref, cw_ref, seg_ref, gt_out, r_out, k_out, v_out, kk_out, b_out, lw_out,
                    g_out, cb_out, tail_out, pbuf_a, pbuf_b):
    i = pl.program_id(0)
    tm = x_ref.shape[0]

    @pl.when(i == 0)
    def _():
        pbuf_a[...] = jnp.zeros_like(pbuf_a)
        pbuf_b[...] = jnp.zeros_like(pbuf_b)

    def step(pcur, pprv):
        first = (i % tiles_per_seq) == 0
        pcur[0:MARGIN, :] = jnp.where(first, init_ref[...], pprv[tm:tm + MARGIN, :])
        tail_out[...] = pprv[tm:tm + MARGIN, :]

        x = x_ref[...]
        xb = (x * lax.rsqrt(jnp.mean(x * x, axis=-1, keepdims=True) + NORM_EPS)
              * g_ref[...]).astype(bf16)
        outs = (r_out, k_out, v_out, kk_out, b_out, lw_out, g_out, cb_out)
        vec, cw, seg = vec_ref[...], cw_ref[...], seg_ref[...]
        ww2, wa2, wg2 = ww2_ref[...], wa2_ref[...], wg2_ref[...]

        def project(c):
            res = jnp.dot(xb, w_ref[:, c:c + 512], preferred_element_type=f32)
            if c < PW:
                pcur[MARGIN:MARGIN + tm, c:c + 512] = res
            else:
                gt_out[:, c - PW:c - PW + 512] = res.astype(gt_out.dtype)
            return res[0:8, :] * 0.0

        cols = list(range(0, NPROJ, 512))
        groups = list(range(0, tm, group))
        per = -(-len(cols) // len(groups))
        tie = jnp.zeros((8, R), f32)
        for n, o in enumerate(groups):
            for c in cols[n * per:(n + 1) * per]:
                tie = project(c)
            _mix_rows(pprv, MARGIN + o, group, vec + tie, cw + tie, seg, ww2, wa2, wg2, outs, o)

    @pl.when(i % 2 == 0)
    def _():
        step(pbuf_a, pbuf_b)

    @pl.when(i % 2 == 1)
    def _():
        step(pbuf_b, pbuf_a)


def _projmix(x2d, seq_len, init, p, tm):
    t = x2d.shape[0]
    n = t // tm
    group = min(tm, 64)

    def whole(shape):
        return pl.BlockSpec(shape, lambda i: (0,) * len(shape))

    lag = pl.BlockSpec((tm, R), lambda i: (jnp.maximum(i - 1, 0), 0))
    return pl.pallas_call(
        functools.partial(_projmix_kernel, seq_len // tm, group),
        name="projmix",
        grid=(n + 1,),
        in_specs=[pl.BlockSpec((tm, D), lambda i: (jnp.minimum(i, n - 1), 0)),
                  whole((1, D)),
                  pl.BlockSpec((D, NPROJ), lambda i: (0, 0), pipeline_mode=pl.Buffered(1)),
                  whole((MARGIN, PW)), whole((8, R)), whole((LORA_W, R)), whole((LORA_A, R)),
                  whole((LORA_G, R)), whole((8, R)), whole((R, R))],
        out_specs=[pl.BlockSpec((tm, 2 * D), lambda i: (jnp.minimum(i, n - 1), 0))] + [lag] * 8
                  + [whole((MARGIN, PW))],
        out_shape=[jax.ShapeDtypeStruct((t, 2 * D), bf16)]
                  + [jax.ShapeDtypeStruct((t, R), f32 if m == 5 else bf16) for m in range(8)]
                  + [jax.ShapeDtypeStruct((MARGIN, PW), f32)],
        scratch_shapes=[pltpu.VMEM((MARGIN + tm, PW), f32)] * 2,
        compiler_params=_cparams(("arbitrary",)),
    )(x2d, p["ln1"], p["w_in"], init, p["vec_mix"], p["ww2"], p["wa2"], p["wg2"], p["cw"], p["seg"])


def _wkv_kernel(r_ref, k_ref, v_ref, kk_ref, b_ref, lw_ref, s0_ref, y_ref, sT_ref, s_scr):
    c = pl.program_id(1)
    nbw = r_ref.shape[0]

    @pl.when(c == 0)
    def _():
        for j in range(nbw):
            s_scr[j] = s0_ref[...]

    C = CHUNK
    W2 = 2 * HS
    ri = lax.broadcasted_iota(jnp.int32, (C, C), 0)
    ci = lax.broadcasted_iota(jnp.int32, (C, C), 1)
    tri = jnp.where(ri >= ci, 1.0, 0.0).astype(bf16)
    rp = lax.broadcasted_iota(jnp.int32, (C, W2), 0)
    cp = lax.broadcasted_iota(jnp.int32, (C, W2), 1)
    lo = cp < HS
    lo1 = lax.broadcasted_iota(jnp.int32, (1, W2), 1) < HS
    sp = jnp.where(lo, cp, cp - HS)
    strict = rp > sp
    incl = rp >= sp
    eye = jnp.where(rp == sp, 1.0, 0.0).astype(f32)
    rq = lax.broadcasted_iota(jnp.int32, (W2, W2), 0)
    cq = lax.broadcasted_iota(jnp.int32, (W2, W2), 1)
    diag_blocks = (rq < HS) == (cq < HS)
    dot = functools.partial(jnp.dot, preferred_element_type=f32)

    def first(z):
        return jnp.where(lo1, z, jnp.zeros_like(z))

    def second(z):
        return jnp.where(lo1, jnp.zeros_like(z), z)

    def bdiag(z):
        return jnp.concatenate([first(z), second(z)], axis=0)

    def adiag(z):
        return jnp.concatenate([second(z), first(z)], axis=0)

    at, rt, bt, kt, bh, kh, vv, g_tot = [], [], [], [], [], [], [], []
    for j in range(nbw):
        lw = lw_ref[j]
        h, m, l = _split3(lw)
        cum = dot(tri, h) + dot(tri, m) + dot(tri, l)
        tot = cum[C - 1:C, :]
        k = k_ref[j].astype(f32)
        b = b_ref[j].astype(f32)
        g_inv = jnp.exp(-cum)
        tail = jnp.exp(tot - cum)
        rt.append(r_ref[j].astype(f32) * jnp.exp(cum))
        kt.append(k * g_inv)
        bt.append(b * g_inv)
        at.append(-kk_ref[j].astype(f32) * jnp.exp(cum - lw))
        bh.append(b * tail)
        kh.append(k * tail)
        vv.append(v_ref[j].astype(f32))
        g_tot.append(jnp.exp(tot))

    chains = [(j, p) for j in range(nbw) for p in range(NH // 2)]
    cs = range(len(chains))
    sl = lambda p: slice(p * W2, (p + 1) * W2)
    s_old = [s_scr[j, p] for j, p in chains]
    lhs = [jnp.concatenate([at[j][:, sl(p)], rt[j][:, sl(p)]], axis=0) for j, p in chains]
    rhs_n = [jnp.concatenate([bt[j][:, sl(p)], kt[j][:, sl(p)]], axis=0).astype(bf16) for j, p in chains]
    rhs_s = [jnp.concatenate([kt[j][:, sl(p)], bt[j][:, sl(p)]], axis=0).astype(bf16) for j, p in chains]
    lhs_b = [x.astype(bf16) for x in lhs]
    am1 = [_bdot_nt(first(lhs[n]), rhs_n[n]) for n in cs]
    am2 = [_bdot_nt(second(lhs[n]), rhs_s[n]) for n in cs]
    a_ab = [jnp.where(strict, jnp.where(lo, am1[n][:C], am2[n][:C]), 0.0) for n in cs]
    a_ak = [jnp.where(strict, jnp.where(lo, am2[n][:C], am1[n][:C]), 0.0) for n in cs]
    a_r1 = [jnp.where(incl, am1[n][C:], 0.0) for n in cs]
    a_r2 = [jnp.where(incl, am2[n][C:], 0.0) for n in cs]
    x = [eye + a_ab[n] for n in cs]
    pw = a_ab
    for _ in range(5):
        pb = [pw[n].astype(bf16) for n in cs]
        pw = [dot(pb[n], bdiag(pb[n])) for n in cs]
        x = [x[n] + dot(x[n].astype(bf16), bdiag(pw[n].astype(bf16))) for n in cs]
    vp = [vv[j][:, sl(p)] for j, p in chains]
    sh = [_bdot_nt(lhs_b[n], s_old[n]) for n in cs]
    av = [dot(a_ak[n].astype(bf16), adiag(vp[n].astype(bf16))) for n in cs]
    u = [dot(x[n].astype(bf16), bdiag((sh[n][:C] + av[n]).astype(bf16))) for n in cs]
    ub = [u[n].astype(bf16) for n in cs]
    vb = [vp[n].astype(bf16) for n in cs]
    mix4 = [jnp.concatenate([first(ub[n]), first(vb[n]), second(vb[n]), second(ub[n])], axis=0)
            for n in cs]
    ys = [sh[n][C:] + dot(jnp.concatenate([a_r1[n], a_r2[n]], axis=1).astype(bf16), mix4[n])
          for n in cs]
    uv = [jnp.concatenate([ub[n], vb[n]], axis=0) for n in cs]
    bk = [jnp.concatenate([bh[j][:, sl(p)], kh[j][:, sl(p)]], axis=0) for j, p in chains]
    s_new = [s_old[n] * g_tot[j][:, sl(p)] + jnp.where(diag_blocks, _bdot_tn(uv[n], bk[n]), 0.0)
             for n, (j, p) in enumerate(chains)]
    for n, (j, p) in enumerate(chains):
        s_scr[j, p] = s_new[n]
        y_ref[j, :, sl(p)] = ys[n]

    @pl.when(c == pl.num_programs(1) - 1)
    def _():
        sT_ref[...] = s_scr[...]


def _wkv(r, k, v, kk, b, lw, s0, nbw):
    nb, seq, _ = r.shape
    assert nb % nbw == 0 and seq % CHUNK == 0
    blk = pl.BlockSpec((nbw, CHUNK, R), lambda i, c: (i, c, 0))
    st = (NH // 2, 2 * HS, 2 * HS)
    return pl.pallas_call(
        _wkv_kernel,
        name="wkv",
        grid=(nb // nbw, seq // CHUNK),
        in_specs=[blk] * 6 + [pl.BlockSpec(st, lambda i, c: (0, 0, 0))],
        out_specs=[blk, pl.BlockSpec((nbw,) + st, lambda i, c: (i, 0, 0, 0))],
        out_shape=[jax.ShapeDtypeStruct((nb, seq, R), f32),
                   jax.ShapeDtypeStruct((nb,) + st, f32)],
        scratch_shapes=[pltpu.VMEM((nbw,) + st, f32)],
        compiler_params=_cparams(("parallel", "arbitrary")),
    )(r, k, v, kk, b, lw, s0)


def _merge_kernel(y_ref, r_ref, k_ref, v_ref, g_ref, cb_ref, gt_ref, x_ref, vec_ref, ln2_ref,
                  seg_ref, worw_ref, woc_ref, wo_ref, wr_ref, br_ref,
                  h_out, x2_out, lg_out):
    seg = seg_ref[...]
    vec = vec_ref[...]
    lnx_g, lnx_b, r_k = vec[0:1, :], vec[1:2, :], vec[2:3, :]
    tm = y_ref.shape[0]
    rows = [slice(s0, s0 + MERGE_SUB) for s0 in range(0, tm, MERGE_SUB)]
    gs = range(len(rows))
    y = [y_ref[sl, :] for sl in rows]
    mean = [_head_sum(y[n], seg) * (1.0 / HS) for n in gs]
    yc = [y[n] - mean[n] for n in gs]
    var = [_head_sum(yc[n] * yc[n], seg) * (1.0 / HS) for n in gs]
    yn = [yc[n] * lax.rsqrt(var[n] + LNX_EPS) * lnx_g + lnx_b for n in gs]
    rk = [_head_sum(r_ref[sl, :].astype(f32) * k_ref[sl, :].astype(f32) * r_k, seg) for sl in rows]
    z = [(yn[n] + rk[n] * v_ref[rows[n], :].astype(f32)) * g_ref[rows[n], :].astype(f32) for n in gs]
    y_a = [_bdot(z[n], worw_ref[...]) for n in gs]
    y_b = [_bdot(cb_ref[sl, :], woc_ref[...]) for sl in rows]
    merged = [jax.nn.sigmoid(gt_ref[rows[n], :D].astype(f32)) * y_a[n]
              + jax.nn.sigmoid(gt_ref[rows[n], D:].astype(f32)) * y_b[n] for n in gs]
    hres = [x_ref[rows[n], :] + _bdot(merged[n], wo_ref[...]) for n in gs]
    x2 = [hres[n] * lax.rsqrt(jnp.mean(hres[n] * hres[n], axis=-1, keepdims=True) + NORM_EPS)
          * ln2_ref[...] for n in gs]
    wh, wm, _ = _split3(wr_ref[...])
    nt = functools.partial(lax.dot_general, dimension_numbers=(((1,), (1,)), ((), ())),
                           preferred_element_type=f32)
    for n, sl in enumerate(rows):
        h_out[sl, :] = hres[n]
        x2_out[sl, :] = x2[n].astype(x2_out.dtype)
        xh, xm, _ = _split3(x2[n])
        lg_out[:, sl] = nt(wh, xh) + nt(wh, xm) + nt(wm, xh) + br_ref[...]


def _merge(y, r, k, v, g, cb, gt, x2d, vec, ln2, seg, worw, woc, wo, wr_t, br, tm):
    t = y.shape[0]

    def rows(n):
        return pl.BlockSpec((tm, n), lambda i: (i, 0))

    def whole(shape):
        return pl.BlockSpec(shape, lambda i: (0,) * len(shape))

    return pl.pallas_call(
        _merge_kernel,
        name="merge",
        grid=(t // tm,),
        in_specs=[rows(R)] * 6 + [rows(2 * D), rows(D), whole((8, R)), whole((1, D)),
                                   whole((R, R)), whole((R, D)), whole((R, D)), whole((D, D)),
                                   whole((NE, D)), whole((NE, 1))],
        out_specs=[rows(D), rows(D), pl.BlockSpec((NE, tm), lambda i: (0, i))],
        out_shape=[jax.ShapeDtypeStruct((t, D), f32), jax.ShapeDtypeStruct((t, D), bf16),
                   jax.ShapeDtypeStruct((NE, t), f32)],
        compiler_params=_cparams(("parallel",)),
    )(y, r, k, v, g, cb, gt, x2d, vec, ln2, seg, worw, woc, wo, wr_t, br)


def _route_kernel(lg_ref, e_out, w_out, pos_out, cnt_out):
    lg = lg_ref[...]
    tr = lg.shape[1]
    erow = lax.broadcasted_iota(jnp.int32, lg.shape, 0)
    work = lg
    hits, vals, idxs = [], [], []
    for _ in range(TOPK):
        m = jnp.max(work, axis=0, keepdims=True)
        idx = jnp.min(jnp.where(work == m, erow, NE), axis=0, keepdims=True)
        hit = erow == idx
        hits.append(hit)
        vals.append(m)
        idxs.append(idx)
        work = jnp.where(hit, -jnp.inf, work)
    ex = [jnp.exp(vk - vals[0]) for vk in vals]
    den = ex[0] + ex[1] + ex[2] + ex[3]
    multi = jnp.where(hits[0] | hits[1] | hits[2] | hits[3], 1.0, 0.0)
    ti = lax.broadcasted_iota(jnp.int32, (tr, tr), 0)
    tj = lax.broadcasted_iota(jnp.int32, (tr, tr), 1)
    before = jnp.where(ti < tj, 1.0, 0.0).astype(bf16)
    excl = jnp.dot(multi.astype(bf16), before, preferred_element_type=f32)
    total = jnp.sum(multi, axis=1, keepdims=True)
    units = jnp.floor((total + (ALIGN - 1)) * (1.0 / ALIGN))
    ei = lax.broadcasted_iota(jnp.int32, (NE, NE), 0)
    ej = lax.broadcasted_iota(jnp.int32, (NE, NE), 1)
    below = jnp.where(ej < ei, 1.0, 0.0).astype(bf16)
    off = ALIGN * jnp.dot(below, jnp.broadcast_to(units, (NE, 128)).astype(bf16),
                          preferred_element_type=f32)[:, 0:1]
    for kk in range(TOPK):
        e_out[kk:kk + 1, :] = idxs[kk]
        w_out[kk:kk + 1, :] = ex[kk] / den
        pos_out[kk:kk + 1, :] = jnp.sum(jnp.where(hits[kk], excl + off, 0.0), axis=0,
                                        keepdims=True).astype(jnp.int32)
    cnt_out[...] = jnp.broadcast_to(total, cnt_out.shape).astype(jnp.int32)


def _route(logits_t):
    t = logits_t.shape[1]
    sel = pl.BlockSpec((TOPK, TT), lambda i: (0, i))
    return pl.pallas_call(
        _route_kernel,
        name="route",
        grid=(t // TT,),
        in_specs=[pl.BlockSpec((NE, TT), lambda i: (0, i))],
        out_specs=[sel, sel, sel, pl.BlockSpec((None, NE, 128), lambda i: (i, 0, 0))],
        out_shape=[jax.ShapeDtypeStruct((TOPK, t), jnp.int32),
                   jax.ShapeDtypeStruct((TOPK, t), f32),
                   jax.ShapeDtypeStruct((TOPK, t), jnp.int32),
                   jax.ShapeDtypeStruct((t // TT, NE, 128), jnp.int32)],
        compiler_params=_cparams(("parallel",)),
    )(logits_t)


def _run_copies(tile, cnt_ref, make_copy, act):
    def per_expert(e, carry):
        cnt = pl.multiple_of(cnt_ref[tile * NE + e], ALIGN)

        @pl.when(cnt > 0)
        def _():
            act(make_copy(e, cnt))
        return carry
    lax.fori_loop(0, NE, per_expert, 0)


def _dispatch_kernel(cnt_ref, run_ref, off_ref, tot_ref, tail_ref, pos_ref, x2_ref, xs_ref,
                     sbuf, zbuf, sem, zsem):
    i = pl.program_id(0)
    last = pl.num_programs(0) - 1
    cur = i % 2

    @pl.when(i == 0)
    def _():
        zbuf[...] = jnp.zeros_like(zbuf)

        def zero_copy(start):
            return pltpu.make_async_copy(zbuf, xs_ref.at[pl.ds(pl.multiple_of(start, EBLK), EBLK)], zsem)

        for e in range(NE):
            @pl.when(tail_ref[e] >= 0)
            def _():
                zero_copy(tail_ref[e]).start()
        for e in range(NE):
            @pl.when(tail_ref[e] >= 0)
            def _():
                zero_copy(tail_ref[e]).wait()
        n_blk = xs_ref.shape[0] // EBLK
        lax.fori_loop(tail_ref[NE], n_blk, lambda blk, c: (zero_copy(blk * EBLK).start(), c)[1], 0)
        lax.fori_loop(tail_ref[NE], n_blk, lambda blk, c: (zero_copy(blk * EBLK).wait(), c)[1], 0)

    def slab(tile, half):
        def make(e, rows):
            src = pl.multiple_of(off_ref[tile * NE + e], ALIGN)
            dst = pl.multiple_of(run_ref[tile * NE + e], ALIGN)
            return pltpu.make_async_copy(sbuf.at[half, pl.ds(src, rows)],
                                         xs_ref.at[pl.ds(dst, rows)], sem.at[half])
        return make

    def drain(tile, half):
        n = pl.multiple_of(tot_ref[tile], ALIGN)
        pltpu.make_async_copy(sbuf.at[half, pl.ds(0, n)], xs_ref.at[pl.ds(0, n)], sem.at[half]).wait()

    @pl.when(i >= 2)
    def _():
        drain(i - 2, cur)

    pos = pos_ref[...]
    xt = x2_ref[...]
    for c in range(0, SB, SCHUNK):
        slot = lax.broadcasted_iota(jnp.int32, (SCHUNK, TT), 0) + c
        place = jnp.zeros((SCHUNK, TT), f32)
        for kk in range(TOPK):
            place = jnp.where(slot == pos[kk:kk + 1, :], 1.0, place)
        sbuf[cur, c:c + SCHUNK] = jnp.dot(place.astype(bf16), xt,
                                          preferred_element_type=f32).astype(bf16)
    _run_copies(i, cnt_ref, slab(i, cur), lambda cp: cp.start())

    @pl.when(i == last)
    def _():
        @pl.when(i >= 1)
        def _():
            drain(i - 1, 1 - cur)
        drain(i, cur)


def _dispatch(cnt16, run_start, off16, tot, tail, pos, x2, n_rows):
    t = x2.shape[0]
    return pl.pallas_call(
        _dispatch_kernel,
        name="dispatch",
        grid_spec=pltpu.PrefetchScalarGridSpec(
            num_scalar_prefetch=5, grid=(t // TT,),
            in_specs=[pl.BlockSpec((TOPK, TT), lambda i, *_: (0, i)),
                      pl.BlockSpec((TT, D), lambda i, *_: (i, 0))],
            out_specs=pl.BlockSpec(memory_space=pl.ANY),
            scratch_shapes=[pltpu.VMEM((2, SB, D), bf16), pltpu.VMEM((EBLK, D), bf16),
                            pltpu.SemaphoreType.DMA((2,)), pltpu.SemaphoreType.DMA(())]),
        out_shape=jax.ShapeDtypeStruct((n_rows, D), bf16),
        compiler_params=_cparams(("arbitrary",)),
    )(cnt16, run_start, off16, tot, tail, pos, x2)


def _expert_kernel(be_ref, nu_ref, fx_ref, xs_ref, wg_ref, bg_ref, wu_ref, bu_ref, wd_ref, bd_ref,
                   ys_ref, wgb, wub, wdb):
    del be_ref
    i = pl.program_id(0)

    @pl.when(fx_ref[i] == 1)
    def _():
        wgb[...] = wg_ref[...].astype(bf16)
        wub[...] = wu_ref[...].astype(bf16)
        wdb[...] = wd_ref[...].astype(bf16)

    @pl.when(i < nu_ref[0])
    def _():
        xb = xs_ref[...]
        gt = jnp.minimum(jnp.dot(xb, wgb[...], preferred_element_type=f32) + bg_ref[...], LIMIT)
        up = jnp.clip(jnp.dot(xb, wub[...], preferred_element_type=f32) + bu_ref[...],
                      -LIMIT, LIMIT)
        act = (up + 1.0) * (gt * jax.nn.sigmoid(gt * ALPHA))
        ys_ref[...] = (jnp.dot(act.astype(bf16), wdb[...], preferred_element_type=f32)
                       + bd_ref[...]).astype(ys_ref.dtype)

    @pl.when(i >= nu_ref[0])
    def _():
        ys_ref[...] = jnp.zeros_like(ys_ref)


def _experts(blk_e, n_used, first, xs, wg, bg, wu, bu, wd, bd):
    p = xs.shape[0]
    wspec = pl.BlockSpec((None, D, D), lambda i, be, nu, fx: (be[i], 0, 0))
    bspec = pl.BlockSpec((None, 1, D), lambda i, be, nu, fx: (be[i], 0, 0))
    xrows = pl.BlockSpec((EBLK, D), lambda i, be, nu, fx: (jnp.where(i < nu[0], i, 0), 0))
    yrows = pl.BlockSpec((EBLK, D), lambda i, be, nu, fx: (i, 0))
    return pl.pallas_call(
        _expert_kernel,
        name="experts",
        grid_spec=pltpu.PrefetchScalarGridSpec(
            num_scalar_prefetch=3, grid=(p // EBLK,),
            in_specs=[xrows, wspec, bspec, wspec, bspec, wspec, bspec],
            out_specs=yrows,
            scratch_shapes=[pltpu.VMEM((D, D), bf16)] * 3),
        out_shape=jax.ShapeDtypeStruct((p, D), bf16),
        compiler_params=_cparams(("arbitrary",)),
    )(blk_e, n_used, first, xs, wg, bg, wu, bu, wd, bd)


def _combine_kernel(cnt_ref, run_ref, off_ref, tot_ref, ys_ref, pos_ref, w_ref, h_ref, g_ref, o_ref,
                    rbuf, sem):
    i = pl.program_id(0)
    cur = i % 2

    def fetch(tile, half):
        def slab(e, rows):
            src = pl.multiple_of(run_ref[tile * NE + e], ALIGN)
            dst = pl.multiple_of(off_ref[tile * NE + e], ALIGN)
            return pltpu.make_async_copy(ys_ref.at[pl.ds(src, rows)],
                                         rbuf.at[half, pl.ds(dst, rows)], sem.at[half])
        return slab

    @pl.when(i == 0)
    def _():
        rbuf[...] = jnp.zeros_like(rbuf)
        _run_copies(0, cnt_ref, fetch(0, 0), lambda cp: cp.start())

    @pl.when(i + 1 < pl.num_programs(0))
    def _():
        _run_copies(i + 1, cnt_ref, fetch(i + 1, 1 - cur), lambda cp: cp.start())

    n = pl.multiple_of(tot_ref[i], ALIGN)
    pltpu.make_async_copy(ys_ref.at[pl.ds(0, n)], rbuf.at[cur, pl.ds(0, n)], sem.at[cur]).wait()

    pos = pos_ref[...]
    w = w_ref[...]
    acc = h_ref[...]
    for c in range(0, SB, SCHUNK):
        slot = lax.broadcasted_iota(jnp.int32, (TT, SCHUNK), 1) + c
        gate = jnp.zeros((TT, SCHUNK), f32)
        for kk in range(TOPK):
            gate = jnp.where(slot == pos[:, kk:kk + 1], w[:, kk:kk + 1], gate)
        acc = acc + jnp.dot(gate.astype(bf16), rbuf[cur, c:c + SCHUNK], preferred_element_type=f32)
    o_ref[...] = acc * lax.rsqrt(jnp.mean(acc * acc, axis=-1, keepdims=True) + NORM_EPS) * g_ref[...]


def _combine(cnt16, run_start, off16, tot, ys, pos_t, gates_t, hres, lnf):
    t = hres.shape[0]
    return pl.pallas_call(
        _combine_kernel,
        name="combine",
        grid_spec=pltpu.PrefetchScalarGridSpec(
            num_scalar_prefetch=4, grid=(t // TT,),
            in_specs=[pl.BlockSpec(memory_space=pl.ANY),
                      pl.BlockSpec((TT, TOPK), lambda i, *_: (i, 0)),
                      pl.BlockSpec((TT, TOPK), lambda i, *_: (i, 0)),
                      pl.BlockSpec((TT, D), lambda i, *_: (i, 0)),
                      pl.BlockSpec((1, D), lambda i, *_: (0, 0))],
            out_specs=pl.BlockSpec((TT, D), lambda i, *_: (i, 0)),
            scratch_shapes=[pltpu.VMEM((2, SB, D), bf16), pltpu.SemaphoreType.DMA((2,))]),
        out_shape=jax.ShapeDtypeStruct((t, D), f32),
        compiler_params=_cparams(("arbitrary",)),
    )(cnt16, run_start, off16, tot, ys, pos_t, gates_t, hres, lnf)


def _row_tile(n, want):
    t = min(n, want)
    assert n % t == 0
    return t


def kernel(x, meta_tokens, ln1_g, w_in, mu_r, mu_k, mu_v, mu_w, mu_a, mu_g, w0, w_w1, w_w2, a0, w_a1, w_a2, w_g1, w_g2, k_k, k_a, r_k, lnx_g, lnx_b, w_o_rwkv, conv_w, w_o_conv, w_o, ln2_g, w_router, b_router, w_e_gate, b_e_gate, w_e_up, b_e_up, w_e_down, b_e_down, lnf_g):
    nb, seq, _ = x.shape
    t = nb * seq
    assert ln1_g.shape[0] == 1, "single layer"

    muw, mua, mug = mu_w[0][:, None], mu_a[0][:, None], mu_g[0][:, None]
    lora_cur = jnp.concatenate([(1 - muw) * w_w1[0], (1 - mua) * w_a1[0], (1 - mug) * w_g1[0]], axis=1)
    lora_prev = jnp.concatenate([muw * w_w1[0], mua * w_a1[0], mug * w_g1[0]], axis=1)
    zrow = jnp.zeros((1, R), f32)
    p = {
        "ln1": ln1_g[0][None, :],
        "w_in": jnp.concatenate([w_in[0][:, :6 * R], lora_cur, lora_prev, w_in[0][:, 6 * R:]],
                                axis=1).astype(bf16),
        "vec_mix": jnp.concatenate([mu_r, mu_k, mu_v, w0, a0, k_k, k_a, zrow], axis=0),
        "ww2": w_w2[0].astype(bf16), "wa2": w_a2[0].astype(bf16), "wg2": w_g2[0].astype(bf16),
        "cw": jnp.concatenate([conv_w[0], jnp.zeros((5, R), f32)], axis=0),
        "seg": (jnp.arange(R)[:, None] // HS == jnp.arange(R)[None, :] // HS).astype(bf16),
    }
    vec_merge = jnp.concatenate([lnx_g, lnx_b, r_k, jnp.zeros((5, R), f32)], axis=0)

    meta_out = _projmix(meta_tokens.astype(f32), NMETA, jnp.zeros((MARGIN, PW), f32), p, NMETA)
    pad = lambda z: jnp.pad(z, ((CHUNK - NMETA, 0), (0, 0)))[None]
    r_m, k_m, v_m, kk_m, b_m, lw_m = (pad(z) for z in meta_out[1:7])
    _, s_meta = _wkv(r_m, k_m, v_m, kk_m, b_m, lw_m, jnp.zeros((NH // 2, 2 * HS, 2 * HS), f32), 1)

    x2d = x.reshape(t, D)
    gt, r, k, v, kk, b, lw, g, cb, _ = _projmix(x2d, seq, meta_out[9], p, _row_tile(seq, 512))
    as3 = lambda z: z.reshape(nb, seq, R)
    y, _ = _wkv(as3(r), as3(k), as3(v), as3(kk), as3(b), as3(lw), s_meta[0], WKV_SEQS)
    hres, x2, logits_t = _merge(
        y.reshape(t, R), r, k, v, g, cb, gt, x2d, vec_merge, ln2_g[0][None, :], p["seg"],
        w_o_rwkv[0].astype(bf16), w_o_conv[0].astype(bf16), w_o[0].astype(bf16),
        w_router[0].T, b_router[0][:, None], _row_tile(t, 512))

    assert t % TT == 0
    nt = t // TT
    _, gates, pos, cnt = _route(logits_t)
    cnt = cnt[:, :, 0]
    cnt16 = ((cnt + ALIGN - 1) // ALIGN) * ALIGN
    base16 = jnp.cumsum(cnt16, axis=0) - cnt16
    tot16 = jnp.sum(cnt16, axis=0)
    padded = ((tot16 + EBLK - 1) // EBLK) * EBLK
    pend = jnp.cumsum(padded)
    pstart = pend - padded
    run_start = pstart[None, :] + base16
    off16 = jnp.cumsum(cnt16, axis=1) - cnt16
    n_rows = -(-(t * TOPK + nt * NE * (ALIGN - 1) + NE * EBLK) // EBLK) * EBLK
    n_blk = n_rows // EBLK
    blk_start = jnp.arange(n_blk, dtype=jnp.int32) * EBLK
    blk_e = jnp.minimum(jnp.sum(pend[None, :] <= blk_start[:, None], axis=1), NE - 1).astype(jnp.int32)
    n_used = (pend[NE - 1:] // EBLK).astype(jnp.int32)
    tail = jnp.concatenate([jnp.where(padded > 0, pend - EBLK, -1), n_used]).astype(jnp.int32)
    flat = lambda z: z.reshape(-1).astype(jnp.int32)
    tot = jnp.sum(cnt16, axis=1).astype(jnp.int32)
    xs = _dispatch(flat(cnt16), flat(run_start), flat(off16), tot, tail, pos, x2, n_rows)
    first = jnp.concatenate([jnp.ones((1,), jnp.int32), (blk_e[1:] != blk_e[:-1]).astype(jnp.int32)])
    ys = _experts(blk_e, n_used, first, xs,
                  w_e_gate[0], b_e_gate[0][:, None, :],
                  w_e_up[0], b_e_up[0][:, None, :],
                  w_e_down[0], b_e_down[0][:, None, :])
    out = _combine(flat(cnt16), flat(run_start), flat(off16), tot, ys, pos.T, gates.T, hres, lnf_g[None, :])
    return out.reshape(nb, seq, D)
```

```python
import functools

import jax
import jax.numpy as jnp
from jax import lax
from jax.experimental import pallas as pl
from jax.experimental.pallas import tpu as pltpu

D = 1024
R = 512
NH = 8
HS = 64
NE = 32
TOPK = 4
NMETA = 16
CHUNK = 64
WKV_SEQS = 4
EBLK = 512
TT = 512
ALIGN = 16
SB = 4 * TT + 512
MERGE_SUB = 256
SCHUNK = 512
NORM_EPS = 1e-5
LNX_EPS = 64e-5
ALPHA = 1.702
LIMIT = 7.0
DECAY_SCALE = 0.6065306597126334
LORA_W, LORA_A, LORA_G = 64, 64, 128
LORA = LORA_W + LORA_A + LORA_G
NPROJ = 3 * R + 3 * R + 2 * D + 2 * LORA
VMEM_LIMIT = 56 * 1024 * 1024

f32 = jnp.float32
bf16 = jnp.bfloat16


def _bdot(a, b):
    return jnp.dot(a.astype(bf16), b.astype(bf16), preferred_element_type=f32)


def _bdot_nt(a, b):
    return lax.dot_general(a.astype(bf16), b.astype(bf16), (((1,), (1,)), ((), ())),
                           preferred_element_type=f32)


def _bdot_tn(a, b):
    return lax.dot_general(a.astype(bf16), b.astype(bf16), (((0,), (0,)), ((), ())),
                           preferred_element_type=f32)


def _split3(x):
    h = x.astype(bf16)
    r1 = x - h.astype(f32)
    m = r1.astype(bf16)
    l = (r1 - m.astype(f32)).astype(bf16)
    return h, m, l


def _head_sum(x, seg):
    return jnp.dot(x.astype(bf16), seg, preferred_element_type=f32)


def _cparams(sem):
    return pltpu.CompilerParams(dimension_semantics=sem, vmem_limit_bytes=VMEM_LIMIT)


MARGIN = 16
PW = 3 * R + 3 * R + 2 * LORA


def _mix_rows(pbuf, a, n, vec, cw, seg, ww2, wa2, wg2, outs, o):
    r_out, k_out, v_out, kk_out, b_out, lw_out, g_out, cb_out = outs
    mu_r, mu_k, mu_v = vec[0:1, :], vec[1:2, :], vec[2:3, :]
    w0, a0, k_k, k_a = vec[3:4, :], vec[4:5, :], vec[5:6, :], vec[6:7, :]
    cur = pbuf[a:a + n, :]
    prev = pbuf[a - 1:a - 1 + n, :]
    r = cur[:, :R] + (prev[:, :R] - cur[:, :R]) * mu_r
    k = cur[:, R:2 * R] + (prev[:, R:2 * R] - cur[:, R:2 * R]) * mu_k
    v = cur[:, 2 * R:3 * R] + (prev[:, 2 * R:3 * R] - cur[:, 2 * R:3 * R]) * mu_v
    lo = 6 * R
    mixed = cur[:, lo:lo + LORA] + prev[:, lo + LORA:lo + 2 * LORA]
    hw = jnp.tanh(mixed[:, :LORA_W])
    ha = mixed[:, LORA_W:LORA_W + LORA_A]
    hg = jax.nn.sigmoid(mixed[:, LORA_W + LORA_A:])
    lw = -DECAY_SCALE * jax.nn.sigmoid(w0 + _bdot(hw, ww2))
    aa = jax.nn.sigmoid(a0 + _bdot(ha, wa2))
    g = _bdot(hg, wg2)
    kk = k * k_k
    norm = jnp.sqrt(_head_sum(kk * kk, seg))
    kk = kk / jnp.maximum(norm, 1e-12)
    k = k * (1.0 + (aa - 1.0) * k_a)
    rows = slice(o, o + n)
    r_out[rows, :] = r.astype(r_out.dtype)
    k_out[rows, :] = k.astype(k_out.dtype)
    v_out[rows, :] = v.astype(v_out.dtype)
    kk_out[rows, :] = kk.astype(kk_out.dtype)
    b_out[rows, :] = (kk * aa).astype(b_out.dtype)
    lw_out[rows, :] = lw
    g_out[rows, :] = g.astype(g_out.dtype)
    prev2 = pbuf[a - 2:a - 2 + n, 4 * R:6 * R]
    u0 = cur[:, 4 * R:5 * R] * cur[:, 5 * R:6 * R]
    u1 = prev[:, 4 * R:5 * R] * prev[:, 5 * R:6 * R]
    u2 = prev2[:, :R] * prev2[:, R:]
    conv = cw[2:3, :] * u0 + cw[1:2, :] * u1 + cw[0:1, :] * u2
    cb_out[rows, :] = (cur[:, 3 * R:4 * R] * conv).astype(cb_out.dtype)


def _projmix_kernel(tiles_per_seq, group, x_ref, g_ref, w_ref, init_ref, vec_ref, ww2_ref, wa2_ref,
                    wg2_ref, cw_ref, seg_ref, gt_out, r_out, k_out, v_out, kk_out, b_out, lw_out,
                    g_out, cb_out, tail_out, pbuf_a, pbuf_b):
    i = pl.program_id(0)
    tm = x_ref.shape[0]

    @pl.when(i == 0)
    def _():
        pbuf_a[...] = jnp.zeros_like(pbuf_a)
        pbuf_b[...] = jnp.zeros_like(pbuf_b)

    def step(pcur, pprv):
        first = (i % tiles_per_seq) == 0
        pcur[0:MARGIN, :] = jnp.where(first, init_ref[...], pprv[tm:tm + MARGIN, :])
        tail_out[...] = pprv[tm:tm + MARGIN, :]

        x = x_ref[...]
        xb = (x * lax.rsqrt(jnp.mean(x * x, axis=-1, keepdims=True) + NORM_EPS)
              * g_ref[...]).astype(bf16)
        outs = (r_out, k_out, v_out, kk_out, b_out, lw_out, g_out, cb_out)
        vec, cw, seg = vec_ref[...], cw_ref[...], seg_ref[...]
        ww2, wa2, wg2 = ww2_ref[...], wa2_ref[...], wg2_ref[...]

        def project(c):
            res = jnp.dot(xb, w_ref[:, c:c + 512], preferred_element_type=f32)
            if c < PW:
                pcur[MARGIN:MARGIN + tm, c:c + 512] = res
            else:
                gt_out[:, c - PW:c - PW + 512] = res.astype(gt_out.dtype)
            return res[0:8, :] * 0.0

        cols = list(range(0, NPROJ, 512))
        groups = list(range(0, tm, group))
        per = -(-len(cols) // len(groups))
        tie = jnp.zeros((8, R), f32)
        for n, o in enumerate(groups):
            for c in cols[n * per:(n + 1) * per]:
                tie = project(c)
            _mix_rows(pprv, MARGIN + o, group, vec + tie, cw + tie, seg, ww2, wa2, wg2, outs, o)

    @pl.when(i % 2 == 0)
    def _():
        step(pbuf_a, pbuf_b)

    @pl.when(i % 2 == 1)
    def _():
        step(pbuf_b, pbuf_a)


def _projmix(x2d, seq_len, init, p, tm):
    t = x2d.shape[0]
    n = t // tm
    group = min(tm, 64)

    def whole(shape):
        return pl.BlockSpec(shape, lambda i: (0,) * len(shape))

    lag = pl.BlockSpec((tm, R), lambda i: (jnp.maximum(i - 1, 0), 0))
    return pl.pallas_call(
        functools.partial(_projmix_kernel, seq_len // tm, group),
        name="projmix",
        grid=(n + 1,),
        in_specs=[pl.BlockSpec((tm, D), lambda i: (jnp.minimum(i, n - 1), 0)),
                  whole((1, D)),
                  pl.BlockSpec((D, NPROJ), lambda i: (0, 0), pipeline_mode=pl.Buffered(1)),
                  whole((MARGIN, PW)), whole((8, R)), whole((LORA_W, R)), whole((LORA_A, R)),
                  whole((LORA_G, R)), whole((8, R)), whole((R, R))],
        out_specs=[pl.BlockSpec((tm, 2 * D), lambda i: (jnp.minimum(i, n - 1), 0))] + [lag] * 8
                  + [whole((MARGIN, PW))],
        out_shape=[jax.ShapeDtypeStruct((t, 2 * D), bf16)]
                  + [jax.ShapeDtypeStruct((t, R), f32 if m == 5 else bf16) for m in range(8)]
                  + [jax.ShapeDtypeStruct((MARGIN, PW), f32)],
        scratch_shapes=[pltpu.VMEM((MARGIN + tm, PW), f32)] * 2,
        compiler_params=_cparams(("arbitrary",)),
    )(x2d, p["ln1"], p["w_in"], init, p["vec_mix"], p["ww2"], p["wa2"], p["wg2"], p["cw"], p["seg"])


def _wkv_kernel(r_ref, k_ref, v_ref, kk_ref, b_ref, lw_ref, s0_ref, y_ref, sT_ref, s_scr):
    c = pl.program_id(1)
    nbw = r_ref.shape[0]

    @pl.when(c == 0)
    def _():
        for j in range(nbw):
            s_scr[j] = s0_ref[...]

    C = CHUNK
    W2 = 2 * HS
    ri = lax.broadcasted_iota(jnp.int32, (C, C), 0)
    ci = lax.broadcasted_iota(jnp.int32, (C, C), 1)
    tri = jnp.where(ri >= ci, 1.0, 0.0).astype(bf16)
    rp = lax.broadcasted_iota(jnp.int32, (C, W2), 0)
    cp = lax.broadcasted_iota(jnp.int32, (C, W2), 1)
    lo = cp < HS
    lo1 = lax.broadcasted_iota(jnp.int32, (1, W2), 1) < HS
    sp = jnp.where(lo, cp, cp - HS)
    strict = rp > sp
    incl = rp >= sp
    eye = jnp.where(rp == sp, 1.0, 0.0).astype(f32)
    rq = lax.broadcasted_iota(jnp.int32, (W2, W2), 0)
    cq = lax.broadcasted_iota(jnp.int32, (W2, W2), 1)
    diag_blocks = (rq < HS) == (cq < HS)
    dot = functools.partial(jnp.dot, preferred_element_type=f32)

    def first(z):
        return jnp.where(lo1, z, jnp.zeros_like(z))

    def second(z):
        return jnp.where(lo1, jnp.zeros_like(z), z)

    def bdiag(z):
        return jnp.concatenate([first(z), second(z)], axis=0)

    def adiag(z):
        return jnp.concatenate([second(z), first(z)], axis=0)

    at, rt, bt, kt, bh, kh, vv, g_tot = [], [], [], [], [], [], [], []
    for j in range(nbw):
        lw = lw_ref[j]
        h, m, l = _split3(lw)
        cum = dot(tri, h) + dot(tri, m) + dot(tri, l)
        tot = cum[C - 1:C, :]
        k = k_ref[j].astype(f32)
        b = b_ref[j].astype(f32)
        g_inv = jnp.exp(-cum)
        tail = jnp.exp(tot - cum)
        rt.append(r_ref[j].astype(f32) * jnp.exp(cum))
        kt.append(k * g_inv)
        bt.append(b * g_inv)
        at.append(-kk_ref[j].astype(f32) * jnp.exp(cum - lw))
        bh.append(b * tail)
        kh.append(k * tail)
        vv.append(v_ref[j].astype(f32))
        g_tot.append(jnp.exp(tot))

    chains = [(j, p) for j in range(nbw) for p in range(NH // 2)]
    cs = range(len(chains))
    sl = lambda p: slice(p * W2, (p + 1) * W2)
    s_old = [s_scr[j, p] for j, p in chains]
    lhs = [jnp.concatenate([at[j][:, sl(p)], rt[j][:, sl(p)]], axis=0) for j, p in chains]
    rhs_n = [jnp.concatenate([bt[j][:, sl(p)], kt[j][:, sl(p)]], axis=0).astype(bf16) for j, p in chains]
    rhs_s = [jnp.concatenate([kt[j][:, sl(p)], bt[j][:, sl(p)]], axis=0).astype(bf16) for j, p in chains]
    lhs_b = [x.astype(bf16) for x in lhs]
    am1 = [_bdot_nt(first(lhs[n]), rhs_n[n]) for n in cs]
    am2 = [_bdot_nt(second(lhs[n]), rhs_s[n]) for n in cs]
    a_ab = [jnp.where(strict, jnp.where(lo, am1[n][:C], am2[n][:C]), 0.0) for n in cs]
    a_ak = [jnp.where(strict, jnp.where(lo, am2[n][:C], am1[n][:C]), 0.0) for n in cs]
    a_r1 = [jnp.where(incl, am1[n][C:], 0.0) for n in cs]
    a_r2 = [jnp.where(incl, am2[n][C:], 0.0) for n in cs]
    pb = [a_ab[n].astype(bf16) for n in cs]
    pw = [dot(pb[n], bdiag(pb[n])) for n in cs]
    x = [eye + a_ab[n] for n in cs]
    for _ in range(4):
        pb = [pw[n].astype(bf16) for n in cs]
        both = [dot(jnp.concatenate([pb[n], x[n].astype(bf16)], axis=0), bdiag(pb[n])) for n in cs]
        pw = [both[n][:C] for n in cs]
        x = [x[n] + both[n][C:] for n in cs]
    x = [x[n] + dot(x[n].astype(bf16), bdiag(pw[n].astype(bf16))) for n in cs]
    vp = [vv[j][:, sl(p)] for j, p in chains]
    sh = [_bdot_nt(lhs_b[n], s_old[n]) for n in cs]
    av = [dot(a_ak[n].astype(bf16), adiag(vp[n].astype(bf16))) for n in cs]
    u = [dot(x[n].astype(bf16), bdiag((sh[n][:C] + av[n]).astype(bf16))) for n in cs]
    ub = [u[n].astype(bf16) for n in cs]
    vb = [vp[n].astype(bf16) for n in cs]
    mix4 = [jnp.concatenate([first(ub[n]), first(vb[n]), second(vb[n]), second(ub[n])], axis=0)
            for n in cs]
    ys = [sh[n][C:] + dot(jnp.concatenate([a_r1[n], a_r2[n]], axis=1).astype(bf16), mix4[n])
          for n in cs]
    uv = [jnp.concatenate([ub[n], vb[n]], axis=0) for n in cs]
    bk = [jnp.concatenate([bh[j][:, sl(p)], kh[j][:, sl(p)]], axis=0) for j, p in chains]
    s_new = [s_old[n] * g_tot[j][:, sl(p)] + jnp.where(diag_blocks, _bdot_tn(uv[n], bk[n]), 0.0)
             for n, (j, p) in enumerate(chains)]
    for n, (j, p) in enumerate(chains):
        s_scr[j, p] = s_new[n]
        y_ref[j, :, sl(p)] = ys[n]

    @pl.when(c == pl.num_programs(1) - 1)
    def _():
        sT_ref[...] = s_scr[...]


def _wkv(r, k, v, kk, b, lw, s0, nbw):
    nb, seq, _ = r.shape
    assert nb % nbw == 0 and seq % CHUNK == 0
    blk = pl.BlockSpec((nbw, CHUNK, R), lambda i, c: (i, c, 0))
    st = (NH // 2, 2 * HS, 2 * HS)
    return pl.pallas_call(
        _wkv_kernel,
        name="wkv",
        grid=(nb // nbw, seq // CHUNK),
        in_specs=[blk] * 6 + [pl.BlockSpec(st, lambda i, c: (0, 0, 0))],
        out_specs=[blk, pl.BlockSpec((nbw,) + st, lambda i, c: (i, 0, 0, 0))],
        out_shape=[jax.ShapeDtypeStruct((nb, seq, R), f32),
                   jax.ShapeDtypeStruct((nb,) + st, f32)],
        scratch_shapes=[pltpu.VMEM((nbw,) + st, f32)],
        compiler_params=_cparams(("parallel", "arbitrary")),
    )(r, k, v, kk, b, lw, s0)


def _merge_kernel(y_ref, r_ref, k_ref, v_ref, g_ref, cb_ref, gt_ref, x_ref, vec_ref, ln2_ref,
                  seg_ref, worw_ref, woc_ref, wo_ref, wr_ref, br_ref,
                  h_out, x2_out, lg_out):
    seg = seg_ref[...]
    vec = vec_ref[...]
    lnx_g, lnx_b, r_k = vec[0:1, :], vec[1:2, :], vec[2:3, :]
    tm = y_ref.shape[0]
    rows = [slice(s0, s0 + MERGE_SUB) for s0 in range(0, tm, MERGE_SUB)]
    gs = range(len(rows))
    y = [y_ref[sl, :] for sl in rows]
    mean = [_head_sum(y[n], seg) * (1.0 / HS) for n in gs]
    yc = [y[n] - mean[n] for n in gs]
    var = [_head_sum(yc[n] * yc[n], seg) * (1.0 / HS) for n in gs]
    yn = [yc[n] * lax.rsqrt(var[n] + LNX_EPS) * lnx_g + lnx_b for n in gs]
    rk = [_head_sum(r_ref[sl, :].astype(f32) * k_ref[sl, :].astype(f32) * r_k, seg) for sl in rows]
    z = [(yn[n] + rk[n] * v_ref[rows[n], :].astype(f32)) * g_ref[rows[n], :].astype(f32) for n in gs]
    y_a = [_bdot(z[n], worw_ref[...]) for n in gs]
    y_b = [_bdot(cb_ref[sl, :], woc_ref[...]) for sl in rows]
    merged = [jax.nn.sigmoid(gt_ref[rows[n], :D].astype(f32)) * y_a[n]
              + jax.nn.sigmoid(gt_ref[rows[n], D:].astype(f32)) * y_b[n] for n in gs]
    hres = [x_ref[rows[n], :] + _bdot(merged[n], wo_ref[...]) for n in gs]
    x2 = [hres[n] * lax.rsqrt(jnp.mean(hres[n] * hres[n], axis=-1, keepdims=True) + NORM_EPS)
          * ln2_ref[...] for n in gs]
    wh, wm, _ = _split3(wr_ref[...])
    nt = functools.partial(lax.dot_general, dimension_numbers=(((1,), (1,)), ((), ())),
                           preferred_element_type=f32)
    for n, sl in enumerate(rows):
        h_out[sl, :] = hres[n]
        x2_out[sl, :] = x2[n].astype(x2_out.dtype)
        xh, xm, _ = _split3(x2[n])
        lg_out[:, sl] = nt(wh, xh) + nt(wh, xm) + nt(wm, xh) + br_ref[...]


def _merge(y, r, k, v, g, cb, gt, x2d, vec, ln2, seg, worw, woc, wo, wr_t, br, tm):
    t = y.shape[0]

    def rows(n):
        return pl.BlockSpec((tm, n), lambda i: (i, 0))

    def whole(shape):
        return pl.BlockSpec(shape, lambda i: (0,) * len(shape))

    return pl.pallas_call(
        _merge_kernel,
        name="merge",
        grid=(t // tm,),
        in_specs=[rows(R)] * 6 + [rows(2 * D), rows(D), whole((8, R)), whole((1, D)),
                                   whole((R, R)), whole((R, D)), whole((R, D)), whole((D, D)),
                                   whole((NE, D)), whole((NE, 1))],
        out_specs=[rows(D), rows(D), pl.BlockSpec((NE, tm), lambda i: (0, i))],
        out_shape=[jax.ShapeDtypeStruct((t, D), f32), jax.ShapeDtypeStruct((t, D), bf16),
                   jax.ShapeDtypeStruct((NE, t), f32)],
        compiler_params=_cparams(("parallel",)),
    )(y, r, k, v, g, cb, gt, x2d, vec, ln2, seg, worw, woc, wo, wr_t, br)


def _route_kernel(lg_ref, e_out, w_out, pos_out, cnt_out):
    lg = lg_ref[...]
    tr = lg.shape[1]
    erow = lax.broadcasted_iota(jnp.int32, lg.shape, 0)
    work = lg
    hits, vals, idxs = [], [], []
    for _ in range(TOPK):
        m = jnp.max(work, axis=0, keepdims=True)
        idx = jnp.min(jnp.where(work == m, erow, NE), axis=0, keepdims=True)
        hit = erow == idx
        hits.append(hit)
        vals.append(m)
        idxs.append(idx)
        work = jnp.where(hit, -jnp.inf, work)
    ex = [jnp.exp(vk - vals[0]) for vk in vals]
    den = ex[0] + ex[1] + ex[2] + ex[3]
    multi = jnp.where(hits[0] | hits[1] | hits[2] | hits[3], 1.0, 0.0)
    ti = lax.broadcasted_iota(jnp.int32, (tr, tr), 0)
    tj = lax.broadcasted_iota(jnp.int32, (tr, tr), 1)
    before = jnp.where(ti < tj, 1.0, 0.0).astype(bf16)
    excl = jnp.dot(multi.astype(bf16), before, preferred_element_type=f32)
    total = jnp.sum(multi, axis=1, keepdims=True)
    units = jnp.floor((total + (ALIGN - 1)) * (1.0 / ALIGN))
    ei = lax.broadcasted_iota(jnp.int32, (NE, NE), 0)
    ej = lax.broadcasted_iota(jnp.int32, (NE, NE), 1)
    below = jnp.where(ej < ei, 1.0, 0.0).astype(bf16)
    off = ALIGN * jnp.dot(below, jnp.broadcast_to(units, (NE, 128)).astype(bf16),
                          preferred_element_type=f32)[:, 0:1]
    for kk in range(TOPK):
        e_out[kk:kk + 1, :] = idxs[kk]
        w_out[kk:kk + 1, :] = ex[kk] / den
        pos_out[kk:kk + 1, :] = jnp.sum(jnp.where(hits[kk], excl + off, 0.0), axis=0,
                                        keepdims=True).astype(jnp.int32)
    cnt_out[...] = jnp.broadcast_to(total, cnt_out.shape).astype(jnp.int32)


def _route(logits_t):
    t = logits_t.shape[1]
    sel = pl.BlockSpec((TOPK, TT), lambda i: (0, i))
    return pl.pallas_call(
        _route_kernel,
        name="route",
        grid=(t // TT,),
        in_specs=[pl.BlockSpec((NE, TT), lambda i: (0, i))],
        out_specs=[sel, sel, sel, pl.BlockSpec((None, NE, 128), lambda i: (i, 0, 0))],
        out_shape=[jax.ShapeDtypeStruct((TOPK, t), jnp.int32),
                   jax.ShapeDtypeStruct((TOPK, t), f32),
                   jax.ShapeDtypeStruct((TOPK, t), jnp.int32),
                   jax.ShapeDtypeStruct((t // TT, NE, 128), jnp.int32)],
        compiler_params=_cparams(("parallel",)),
    )(logits_t)


def _run_copies(tile, cnt_ref, make_copy, act):
    def per_expert(e, carry):
        cnt = pl.multiple_of(cnt_ref[tile * NE + e], ALIGN)

        @pl.when(cnt > 0)
        def _():
            act(make_copy(e, cnt))
        return carry
    lax.fori_loop(0, NE, per_expert, 0)


def _dispatch_kernel(cnt_ref, run_ref, off_ref, tot_ref, tail_ref, pos_ref, x2_ref, xs_ref,
                     sbuf, zbuf, sem, zsem):
    i = pl.program_id(0)
    last = pl.num_programs(0) - 1
    cur = i % 2

    @pl.when(i == 0)
    def _():
        zbuf[...] = jnp.zeros_like(zbuf)

        def zero_copy(start):
            return pltpu.make_async_copy(zbuf, xs_ref.at[pl.ds(pl.multiple_of(start, EBLK), EBLK)], zsem)

        for e in range(NE):
            @pl.when(tail_ref[e] >= 0)
            def _():
                zero_copy(tail_ref[e]).start()
        for e in range(NE):
            @pl.when(tail_ref[e] >= 0)
            def _():
                zero_copy(tail_ref[e]).wait()
        n_blk = xs_ref.shape[0] // EBLK
        lax.fori_loop(tail_ref[NE], n_blk, lambda blk, c: (zero_copy(blk * EBLK).start(), c)[1], 0)
        lax.fori_loop(tail_ref[NE], n_blk, lambda blk, c: (zero_copy(blk * EBLK).wait(), c)[1], 0)

    def slab(tile, half):
        def make(e, rows):
            src = pl.multiple_of(off_ref[tile * NE + e], ALIGN)
            dst = pl.multiple_of(run_ref[tile * NE + e], ALIGN)
            return pltpu.make_async_copy(sbuf.at[half, pl.ds(src, rows)],
                                         xs_ref.at[pl.ds(dst, rows)], sem.at[half])
        return make

    def drain(tile, half):
        n = pl.multiple_of(tot_ref[tile], ALIGN)
        pltpu.make_async_copy(sbuf.at[half, pl.ds(0, n)], xs_ref.at[pl.ds(0, n)], sem.at[half]).wait()

    @pl.when(i >= 2)
    def _():
        drain(i - 2, cur)

    pos = pos_ref[...]
    xt = x2_ref[...]
    for c in range(0, SB, SCHUNK):
        slot = lax.broadcasted_iota(jnp.int32, (SCHUNK, TT), 0) + c
        place = jnp.zeros((SCHUNK, TT), f32)
        for kk in range(TOPK):
            place = jnp.where(slot == pos[kk:kk + 1, :], 1.0, place)
        sbuf[cur, c:c + SCHUNK] = jnp.dot(place.astype(bf16), xt,
                                          preferred_element_type=f32).astype(bf16)
    _run_copies(i, cnt_ref, slab(i, cur), lambda cp: cp.start())

    @pl.when(i == last)
    def _():
        @pl.when(i >= 1)
        def _():
            drain(i - 1, 1 - cur)
        drain(i, cur)


def _dispatch(cnt16, run_start, off16, tot, tail, pos, x2, n_rows):
    t = x2.shape[0]
    return pl.pallas_call(
        _dispatch_kernel,
        name="dispatch",
        grid_spec=pltpu.PrefetchScalarGridSpec(
            num_scalar_prefetch=5, grid=(t // TT,),
            in_specs=[pl.BlockSpec((TOPK, TT), lambda i, *_: (0, i)),
                      pl.BlockSpec((TT, D), lambda i, *_: (i, 0))],
            out_specs=pl.BlockSpec(memory_space=pl.ANY),
            scratch_shapes=[pltpu.VMEM((2, SB, D), bf16), pltpu.VMEM((EBLK, D), bf16),
                            pltpu.SemaphoreType.DMA((2,)), pltpu.SemaphoreType.DMA(())]),
        out_shape=jax.ShapeDtypeStruct((n_rows, D), bf16),
        compiler_params=_cparams(("arbitrary",)),
    )(cnt16, run_start, off16, tot, tail, pos, x2)


def _expert_kernel(be_ref, nu_ref, fx_ref, xs_ref, wg_ref, bg_ref, wu_ref, bu_ref, wd_ref, bd_ref,
                   ys_ref, wgb, wub, wdb):
    del be_ref
    i = pl.program_id(0)

    @pl.when(fx_ref[i] == 1)
    def _():
        wgb[...] = wg_ref[...].astype(bf16)
        wub[...] = wu_ref[...].astype(bf16)
        wdb[...] = wd_ref[...].astype(bf16)

    @pl.when(i < nu_ref[0])
    def _():
        xb = xs_ref[...]
        gt = jnp.minimum(jnp.dot(xb, wgb[...], preferred_element_type=f32) + bg_ref[...], LIMIT)
        up = jnp.clip(jnp.dot(xb, wub[...], preferred_element_type=f32) + bu_ref[...],
                      -LIMIT, LIMIT)
        act = (up + 1.0) * (gt * jax.nn.sigmoid(gt * ALPHA))
        ys_ref[...] = (jnp.dot(act.astype(bf16), wdb[...], preferred_element_type=f32)
                       + bd_ref[...]).astype(ys_ref.dtype)

    @pl.when(i >= nu_ref[0])
    def _():
        ys_ref[...] = jnp.zeros_like(ys_ref)


def _experts(blk_e, n_used, first, xs, wg, bg, wu, bu, wd, bd):
    p = xs.shape[0]
    wspec = pl.BlockSpec((None, D, D), lambda i, be, nu, fx: (be[i], 0, 0))
    bspec = pl.BlockSpec((None, 1, D), lambda i, be, nu, fx: (be[i], 0, 0))
    xrows = pl.BlockSpec((EBLK, D), lambda i, be, nu, fx: (jnp.where(i < nu[0], i, 0), 0))
    yrows = pl.BlockSpec((EBLK, D), lambda i, be, nu, fx: (i, 0))
    return pl.pallas_call(
        _expert_kernel,
        name="experts",
        grid_spec=pltpu.PrefetchScalarGridSpec(
            num_scalar_prefetch=3, grid=(p // EBLK,),
            in_specs=[xrows, wspec, bspec, wspec, bspec, wspec, bspec],
            out_specs=yrows,
            scratch_shapes=[pltpu.VMEM((D, D), bf16)] * 3),
        out_shape=jax.ShapeDtypeStruct((p, D), bf16),
        compiler_params=_cparams(("arbitrary",)),
    )(blk_e, n_used, first, xs, wg, bg, wu, bu, wd, bd)


def _combine_kernel(cnt_ref, run_ref, off_ref, tot_ref, ys_ref, pos_ref, w_ref, h_ref, g_ref, o_ref,
                    rbuf, sem):
    i = pl.program_id(0)
    cur = i % 2

    def fetch(tile, half):
        def slab(e, rows):
            src = pl.multiple_of(run_ref[tile * NE + e], ALIGN)
            dst = pl.multiple_of(off_ref[tile * NE + e], ALIGN)
            return pltpu.make_async_copy(ys_ref.at[pl.ds(src, rows)],
                                         rbuf.at[half, pl.ds(dst, rows)], sem.at[half])
        return slab

    @pl.when(i == 0)
    def _():
        rbuf[...] = jnp.zeros_like(rbuf)
        _run_copies(0, cnt_ref, fetch(0, 0), lambda cp: cp.start())

    @pl.when(i + 1 < pl.num_programs(0))
    def _():
        _run_copies(i + 1, cnt_ref, fetch(i + 1, 1 - cur), lambda cp: cp.start())

    n = pl.multiple_of(tot_ref[i], ALIGN)
    pltpu.make_async_copy(ys_ref.at[pl.ds(0, n)], rbuf.at[cur, pl.ds(0, n)], sem.at[cur]).wait()

    pos = pos_ref[...]
    w = w_ref[...]
    acc = h_ref[...]
    for c in range(0, SB, SCHUNK):
        slot = lax.broadcasted_iota(jnp.int32, (TT, SCHUNK), 1) + c
        gate = jnp.zeros((TT, SCHUNK), f32)
        for kk in range(TOPK):
            gate = jnp.where(slot == pos[:, kk:kk + 1], w[:, kk:kk + 1], gate)
        acc = acc + jnp.dot(gate.astype(bf16), rbuf[cur, c:c + SCHUNK], preferred_element_type=f32)
    o_ref[...] = acc * lax.rsqrt(jnp.mean(acc * acc, axis=-1, keepdims=True) + NORM_EPS) * g_ref[...]


def _combine(cnt16, run_start, off16, tot, ys, pos_t, gates_t, hres, lnf):
    t = hres.shape[0]
    return pl.pallas_call(
        _combine_kernel,
        name="combine",
        grid_spec=pltpu.PrefetchScalarGridSpec(
            num_scalar_prefetch=4, grid=(t // TT,),
            in_specs=[pl.BlockSpec(memory_space=pl.ANY),
                      pl.BlockSpec((TT, TOPK), lambda i, *_: (i, 0)),
                      pl.BlockSpec((TT, TOPK), lambda i, *_: (i, 0)),
                      pl.BlockSpec((TT, D), lambda i, *_: (i, 0)),
                      pl.BlockSpec((1, D), lambda i, *_: (0, 0))],
            out_specs=pl.BlockSpec((TT, D), lambda i, *_: (i, 0)),
            scratch_shapes=[pltpu.VMEM((2, SB, D), bf16), pltpu.SemaphoreType.DMA((2,))]),
        out_shape=jax.ShapeDtypeStruct((t, D), f32),
        compiler_params=_cparams(("arbitrary",)),
    )(cnt16, run_start, off16, tot, ys, pos_t, gates_t, hres, lnf)


def _row_tile(n, want):
    t = min(n, want)
    assert n % t == 0
    return t


def kernel(x, meta_tokens, ln1_g, w_in, mu_r, mu_k, mu_v, mu_w, mu_a, mu_g, w0, w_w1, w_w2, a0, w_a1, w_a2, w_g1, w_g2, k_k, k_a, r_k, lnx_g, lnx_b, w_o_rwkv, conv_w, w_o_conv, w_o, ln2_g, w_router, b_router, w_e_gate, b_e_gate, w_e_up, b_e_up, w_e_down, b_e_down, lnf_g):
    nb, seq, _ = x.shape
    t = nb * seq
    assert ln1_g.shape[0] == 1, "single layer"

    muw, mua, mug = mu_w[0][:, None], mu_a[0][:, None], mu_g[0][:, None]
    lora_cur = jnp.concatenate([(1 - muw) * w_w1[0], (1 - mua) * w_a1[0], (1 - mug) * w_g1[0]], axis=1)
    lora_prev = jnp.concatenate([muw * w_w1[0], mua * w_a1[0], mug * w_g1[0]], axis=1)
    zrow = jnp.zeros((1, R), f32)
    p = {
        "ln1": ln1_g[0][None, :],
        "w_in": jnp.concatenate([w_in[0][:, :6 * R], lora_cur, lora_prev, w_in[0][:, 6 * R:]],
                                axis=1).astype(bf16),
        "vec_mix": jnp.concatenate([mu_r, mu_k, mu_v, w0, a0, k_k, k_a, zrow], axis=0),
        "ww2": w_w2[0].astype(bf16), "wa2": w_a2[0].astype(bf16), "wg2": w_g2[0].astype(bf16),
        "cw": jnp.concatenate([conv_w[0], jnp.zeros((5, R), f32)], axis=0),
        "seg": (jnp.arange(R)[:, None] // HS == jnp.arange(R)[None, :] // HS).astype(bf16),
    }
    vec_merge = jnp.concatenate([lnx_g, lnx_b, r_k, jnp.zeros((5, R), f32)], axis=0)

    meta_out = _projmix(meta_tokens.astype(f32), NMETA, jnp.zeros((MARGIN, PW), f32), p, NMETA)
    pad = lambda z: jnp.pad(z, ((CHUNK - NMETA, 0), (0, 0)))[None]
    r_m, k_m, v_m, kk_m, b_m, lw_m = (pad(z) for z in meta_out[1:7])
    _, s_meta = _wkv(r_m, k_m, v_m, kk_m, b_m, lw_m, jnp.zeros((NH // 2, 2 * HS, 2 * HS), f32), 1)

    x2d = x.reshape(t, D)
    gt, r, k, v, kk, b, lw, g, cb, _ = _projmix(x2d, seq, meta_out[9], p, _row_tile(seq, 512))
    as3 = lambda z: z.reshape(nb, seq, R)
    y, _ = _wkv(as3(r), as3(k), as3(v), as3(kk), as3(b), as3(lw), s_meta[0], WKV_SEQS)
    hres, x2, logits_t = _merge(
        y.reshape(t, R), r, k, v, g, cb, gt, x2d, vec_merge, ln2_g[0][None, :], p["seg"],
        w_o_rwkv[0].astype(bf16), w_o_conv[0].astype(bf16), w_o[0].astype(bf16),
        w_router[0].T, b_router[0][:, None], _row_tile(t, 512))

    assert t % TT == 0
    nt = t // TT
    _, gates, pos, cnt = _route(logits_t)
    cnt = cnt[:, :, 0]
    cnt16 = ((cnt + ALIGN - 1) // ALIGN) * ALIGN
    base16 = jnp.cumsum(cnt16, axis=0) - cnt16
    tot16 = jnp.sum(cnt16, axis=0)
    padded = ((tot16 + EBLK - 1) // EBLK) * EBLK
    pend = jnp.cumsum(padded)
    pstart = pend - padded
    run_start = pstart[None, :] + base16
    off16 = jnp.cumsum(cnt16, axis=1) - cnt16
    n_rows = -(-(t * TOPK + nt * NE * (ALIGN - 1) + NE * EBLK) // EBLK) * EBLK
    n_blk = n_rows // EBLK
    blk_start = jnp.arange(n_blk, dtype=jnp.int32) * EBLK
    blk_e = jnp.minimum(jnp.sum(pend[None, :] <= blk_start[:, None], axis=1), NE - 1).astype(jnp.int32)
    n_used = (pend[NE - 1:] // EBLK).astype(jnp.int32)
    tail = jnp.concatenate([jnp.where(padded > 0, pend - EBLK, -1), n_used]).astype(jnp.int32)
    flat = lambda z: z.reshape(-1).astype(jnp.int32)
    tot = jnp.sum(cnt16, axis=1).astype(jnp.int32)
    xs = _dispatch(flat(cnt16), flat(run_start), flat(off16), tot, tail, pos, x2, n_rows)
    first = jnp.concatenate([jnp.ones((1,), jnp.int32), (blk_e[1:] != blk_e[:-1]).astype(jnp.int32)])
    ys = _experts(blk_e, n_used, first, xs,
                  w_e_gate[0], b_e_gate[0][:, None, :],
                  w_e_up[0], b_e_up[0][:, None, :],
                  w_e_down[0], b_e_down[0][:, None, :])
    out = _combine(flat(cnt16), flat(run_start), flat(off16), tot, ys, pos.T, gates.T, hres, lnf_g[None, :])
    return out.reshape(nb, seq, D)
```

```python
import functools

import jax
import jax.numpy as jnp
from jax import lax
from jax.experimental import pallas as pl
from jax.experimental.pallas import tpu as pltpu

D = 1024
R = 512
NH = 8
HS = 64
NE = 32
TOPK = 4
NMETA = 16
CHUNK = 64
WKV_SEQS = 4
EBLK = 512
TT = 512
ALIGN = 8
SB = 4 * TT + 256
DH = D // 2
MERGE_SUB = 256
SCHUNK = 768
NORM_EPS = 1e-5
LNX_EPS = 64e-5
ALPHA = 1.702
LIMIT = 7.0
DECAY_SCALE = 0.6065306597126334
LORA_W, LORA_A, LORA_G = 64, 64, 128
LORA = LORA_W + LORA_A + LORA_G
NPROJ = 3 * R + 3 * R + 2 * D + 2 * LORA
VMEM_LIMIT = 56 * 1024 * 1024

f32 = jnp.float32
bf16 = jnp.bfloat16


def _bdot(a, b):
    return jnp.dot(a.astype(bf16), b.astype(bf16), preferred_element_type=f32)


def _bdot_nt(a, b):
    return lax.dot_general(a.astype(bf16), b.astype(bf16), (((1,), (1,)), ((), ())),
                           preferred_element_type=f32)


def _bdot_tn(a, b):
    return lax.dot_general(a.astype(bf16), b.astype(bf16), (((0,), (0,)), ((), ())),
                           preferred_element_type=f32)


def _split3(x):
    h = x.astype(bf16)
    r1 = x - h.astype(f32)
    m = r1.astype(bf16)
    l = (r1 - m.astype(f32)).astype(bf16)
    return h, m, l


def _head_sum(x, seg):
    return jnp.dot(x.astype(bf16), seg, preferred_element_type=f32)


def _pack_pairs(x):
    a = lax.bitcast_convert_type(x[:, :DH], jnp.uint32)
    b = lax.bitcast_convert_type(x[:, DH:], jnp.uint32)
    return (a & jnp.uint32(0xFFFF0000)) | (b >> 16)


def _unpack_pairs(p):
    lo = lax.bitcast_convert_type(p & jnp.uint32(0xFFFF0000), f32)
    hi = lax.bitcast_convert_type(p << 16, f32)
    return lo.astype(bf16), hi.astype(bf16)


def _cparams(sem):
    return pltpu.CompilerParams(dimension_semantics=sem, vmem_limit_bytes=VMEM_LIMIT)


MARGIN = 16
PW = 3 * R + 3 * R + 2 * LORA


def _mix_rows(pbuf, a, n, vec, cw, seg, ww2, wa2, wg2, outs, o):
    r_out, k_out, v_out, kk_out, b_out, lw_out, g_out, cb_out = outs
    mu_r, mu_k, mu_v = vec[0:1, :], vec[1:2, :], vec[2:3, :]
    w0, a0, k_k, k_a = vec[3:4, :], vec[4:5, :], vec[5:6, :], vec[6:7, :]
    cur = pbuf[a:a + n, :]
    prev = pbuf[a - 1:a - 1 + n, :]
    r = cur[:, :R] + (prev[:, :R] - cur[:, :R]) * mu_r
    k = cur[:, R:2 * R] + (prev[:, R:2 * R] - cur[:, R:2 * R]) * mu_k
    v = cur[:, 2 * R:3 * R] + (prev[:, 2 * R:3 * R] - cur[:, 2 * R:3 * R]) * mu_v
    lo = 6 * R
    mixed = cur[:, lo:lo + LORA] + prev[:, lo + LORA:lo + 2 * LORA]
    hw = jnp.tanh(mixed[:, :LORA_W])
    ha = mixed[:, LORA_W:LORA_W + LORA_A]
    hg = jax.nn.sigmoid(mixed[:, LORA_W + LORA_A:])
    lw = -DECAY_SCALE * jax.nn.sigmoid(w0 + _bdot(hw, ww2))
    aa = jax.nn.sigmoid(a0 + _bdot(ha, wa2))
    g = _bdot(hg, wg2)
    kk = k * k_k
    norm = jnp.sqrt(_head_sum(kk * kk, seg))
    kk = kk / jnp.maximum(norm, 1e-12)
    k = k * (1.0 + (aa - 1.0) * k_a)
    rows = slice(o, o + n)
    r_out[rows, :] = r.astype(r_out.dtype)
    k_out[rows, :] = k.astype(k_out.dtype)
    v_out[rows, :] = v.astype(v_out.dtype)
    kk_out[rows, :] = kk.astype(kk_out.dtype)
    b_out[rows, :] = (kk * aa).astype(b_out.dtype)
    lw_out[rows, :] = lw
    g_out[rows, :] = g.astype(g_out.dtype)
    prev2 = pbuf[a - 2:a - 2 + n, 4 * R:6 * R]
    u0 = cur[:, 4 * R:5 * R] * cur[:, 5 * R:6 * R]
    u1 = prev[:, 4 * R:5 * R] * prev[:, 5 * R:6 * R]
    u2 = prev2[:, :R] * prev2[:, R:]
    conv = cw[2:3, :] * u0 + cw[1:2, :] * u1 + cw[0:1, :] * u2
    cb_out[rows, :] = (cur[:, 3 * R:4 * R] * conv).astype(cb_out.dtype)


def _projmix_kernel(tiles_per_seq, group, x_ref, g_ref, w_ref, init_ref, vec_ref, ww2_ref, wa2_ref,
                    wg2_ref, cw_ref, seg_ref, gt_out, r_out, k_out, v_out, kk_out, b_out, lw_out,
                    g_out, cb_out, tail_out, pbuf_a, pbuf_b):
    i = pl.program_id(0)
    tm = x_ref.shape[0]

    @pl.when(i == 0)
    def _():
        pbuf_a[...] = jnp.zeros_like(pbuf_a)
        pbuf_b[...] = jnp.zeros_like(pbuf_b)

    def step(pcur, pprv):
        first = (i % tiles_per_seq) == 0
        pcur[0:MARGIN, :] = jnp.where(first, init_ref[...], pprv[tm:tm + MARGIN, :])
        tail_out[...] = pprv[tm:tm + MARGIN, :]

        x = x_ref[...]
        xb = (x * lax.rsqrt(jnp.mean(x * x, axis=-1, keepdims=True) + NORM_EPS)
              * g_ref[...]).astype(bf16)
        outs = (r_out, k_out, v_out, kk_out, b_out, lw_out, g_out, cb_out)
        vec, cw, seg = vec_ref[...], cw_ref[...], seg_ref[...]
        ww2, wa2, wg2 = ww2_ref[...], wa2_ref[...], wg2_ref[...]

        def project(c):
            res = jnp.dot(xb, w_ref[:, c:c + 512], preferred_element_type=f32)
            if c < PW:
                pcur[MARGIN:MARGIN + tm, c:c + 512] = res
            else:
                gt_out[:, c - PW:c - PW + 512] = res.astype(gt_out.dtype)
            return res[0:8, :] * 0.0

        cols = list(range(0, NPROJ, 512))
        groups = list(range(0, tm, group))
        per = -(-len(cols) // len(groups))
        tie = jnp.zeros((8, R), f32)
        for n, o in enumerate(groups):
            for c in cols[n * per:(n + 1) * per]:
                tie = project(c)
            _mix_rows(pprv, MARGIN + o, group, vec + tie, cw + tie, seg, ww2, wa2, wg2, outs, o)

    @pl.when(i % 2 == 0)
    def _():
        step(pbuf_a, pbuf_b)

    @pl.when(i % 2 == 1)
    def _():
        step(pbuf_b, pbuf_a)


def _projmix(x2d, seq_len, init, p, tm):
    t = x2d.shape[0]
    n = t // tm
    group = min(tm, 64)

    def whole(shape):
        return pl.BlockSpec(shape, lambda i: (0,) * len(shape))

    lag = pl.BlockSpec((tm, R), lambda i: (jnp.maximum(i - 1, 0), 0))
    return pl.pallas_call(
        functools.partial(_projmix_kernel, seq_len // tm, group),
        name="projmix",
        grid=(n + 1,),
        in_specs=[pl.BlockSpec((tm, D), lambda i: (jnp.minimum(i, n - 1), 0)),
                  whole((1, D)),
                  pl.BlockSpec((D, NPROJ), lambda i: (0, 0), pipeline_mode=pl.Buffered(1)),
                  whole((MARGIN, PW)), whole((8, R)), whole((LORA_W, R)), whole((LORA_A, R)),
                  whole((LORA_G, R)), whole((8, R)), whole((R, R))],
        out_specs=[pl.BlockSpec((tm, 2 * D), lambda i: (jnp.minimum(i, n - 1), 0))] + [lag] * 8
                  + [whole((MARGIN, PW))],
        out_shape=[jax.ShapeDtypeStruct((t, 2 * D), bf16)]
                  + [jax.ShapeDtypeStruct((t, R), f32 if m == 5 else bf16) for m in range(8)]
                  + [jax.ShapeDtypeStruct((MARGIN, PW), f32)],
        scratch_shapes=[pltpu.VMEM((MARGIN + tm, PW), f32)] * 2,
        compiler_params=_cparams(("arbitrary",)),
    )(x2d, p["ln1"], p["w_in"], init, p["vec_mix"], p["ww2"], p["wa2"], p["wg2"], p["cw"], p["seg"])


def _wkv_kernel(r_ref, k_ref, v_ref, kk_ref, b_ref, lw_ref, s0_ref, y_ref, sT_ref, s_scr):
    c = pl.program_id(1)
    nbw = r_ref.shape[0]

    @pl.when(c == 0)
    def _():
        for j in range(nbw):
            s_scr[j] = s0_ref[...]

    C = CHUNK
    W2 = 2 * HS
    ri = lax.broadcasted_iota(jnp.int32, (C, C), 0)
    ci = lax.broadcasted_iota(jnp.int32, (C, C), 1)
    tri = jnp.where(ri >= ci, 1.0, 0.0).astype(bf16)
    rp = lax.broadcasted_iota(jnp.int32, (C, W2), 0)
    cp = lax.broadcasted_iota(jnp.int32, (C, W2), 1)
    lo = cp < HS
    lo1 = lax.broadcasted_iota(jnp.int32, (1, W2), 1) < HS
    sp = jnp.where(lo, cp, cp - HS)
    strict = rp > sp
    incl = rp >= sp
    eye = jnp.where(rp == sp, 1.0, 0.0).astype(f32)
    rq = lax.broadcasted_iota(jnp.int32, (W2, W2), 0)
    cq = lax.broadcasted_iota(jnp.int32, (W2, W2), 1)
    diag_blocks = (rq < HS) == (cq < HS)
    dot = functools.partial(jnp.dot, preferred_element_type=f32)

    def first(z):
        return jnp.where(lo1, z, jnp.zeros_like(z))

    def second(z):
        return jnp.where(lo1, jnp.zeros_like(z), z)

    def bdiag(z):
        return jnp.concatenate([first(z), second(z)], axis=0)

    def adiag(z):
        return jnp.concatenate([second(z), first(z)], axis=0)

    at, rt, bt, kt, bh, kh, vv, g_tot = [], [], [], [], [], [], [], []
    for j in range(nbw):
        lw = lw_ref[j]
        h, m, l = _split3(lw)
        cum = dot(tri, h) + dot(tri, m) + dot(tri, l)
        tot = cum[C - 1:C, :]
        k = k_ref[j].astype(f32)
        b = b_ref[j].astype(f32)
        g_inv = jnp.exp(-cum)
        tail = jnp.exp(tot - cum)
        rt.append(r_ref[j].astype(f32) * jnp.exp(cum))
        kt.append(k * g_inv)
        bt.append(b * g_inv)
        at.append(-kk_ref[j].astype(f32) * jnp.exp(cum - lw))
        bh.append(b * tail)
        kh.append(k * tail)
        vv.append(v_ref[j].astype(f32))
        g_tot.append(jnp.exp(tot))

    chains = [(j, p) for j in range(nbw) for p in range(NH // 2)]
    cs = range(len(chains))
    sl = lambda p: slice(p * W2, (p + 1) * W2)
    s_old = [s_scr[j, p] for j, p in chains]
    lhs = [jnp.concatenate([at[j][:, sl(p)], rt[j][:, sl(p)]], axis=0) for j, p in chains]
    rhs_n = [jnp.concatenate([bt[j][:, sl(p)], kt[j][:, sl(p)]], axis=0).astype(bf16) for j, p in chains]
    rhs_s = [jnp.concatenate([kt[j][:, sl(p)], bt[j][:, sl(p)]], axis=0).astype(bf16) for j, p in chains]
    lhs_b = [x.astype(bf16) for x in lhs]
    am1 = [_bdot_nt(first(lhs[n]), rhs_n[n]) for n in cs]
    am2 = [_bdot_nt(second(lhs[n]), rhs_s[n]) for n in cs]
    a_ab = [jnp.where(strict, jnp.where(lo, am1[n][:C], am2[n][:C]), 0.0) for n in cs]
    a_ak = [jnp.where(strict, jnp.where(lo, am2[n][:C], am1[n][:C]), 0.0) for n in cs]
    a_r1 = [jnp.where(incl, am1[n][C:], 0.0) for n in cs]
    a_r2 = [jnp.where(incl, am2[n][C:], 0.0) for n in cs]
    pb = [a_ab[n].astype(bf16) for n in cs]
    pw = [dot(pb[n], bdiag(pb[n])) for n in cs]
    x = [eye + a_ab[n] for n in cs]
    for _ in range(4):
        pb = [pw[n].astype(bf16) for n in cs]
        both = [dot(jnp.concatenate([pb[n], x[n].astype(bf16)], axis=0), bdiag(pb[n])) for n in cs]
        pw = [both[n][:C] for n in cs]
        x = [x[n] + both[n][C:] for n in cs]
    x = [x[n] + dot(x[n].astype(bf16), bdiag(pw[n].astype(bf16))) for n in cs]
    vp = [vv[j][:, sl(p)] for j, p in chains]
    sh = [_bdot_nt(lhs_b[n], s_old[n]) for n in cs]
    av = [dot(a_ak[n].astype(bf16), adiag(vp[n].astype(bf16))) for n in cs]
    u = [dot(x[n].astype(bf16), bdiag((sh[n][:C] + av[n]).astype(bf16))) for n in cs]
    ub = [u[n].astype(bf16) for n in cs]
    vb = [vp[n].astype(bf16) for n in cs]
    mix4 = [jnp.concatenate([first(ub[n]), first(vb[n]), second(vb[n]), second(ub[n])], axis=0)
            for n in cs]
    ys = [sh[n][C:] + dot(jnp.concatenate([a_r1[n], a_r2[n]], axis=1).astype(bf16), mix4[n])
          for n in cs]
    uv = [jnp.concatenate([ub[n], vb[n]], axis=0) for n in cs]
    bk = [jnp.concatenate([bh[j][:, sl(p)], kh[j][:, sl(p)]], axis=0) for j, p in chains]
    s_new = [s_old[n] * g_tot[j][:, sl(p)] + jnp.where(diag_blocks, _bdot_tn(uv[n], bk[n]), 0.0)
             for n, (j, p) in enumerate(chains)]
    for n, (j, p) in enumerate(chains):
        s_scr[j, p] = s_new[n]
        y_ref[j, :, sl(p)] = ys[n]

    @pl.when(c == pl.num_programs(1) - 1)
    def _():
        sT_ref[...] = s_scr[...]


def _wkv(r, k, v, kk, b, lw, s0, nbw):
    nb, seq, _ = r.shape
    assert nb % nbw == 0 and seq % CHUNK == 0
    blk = pl.BlockSpec((nbw, CHUNK, R), lambda i, c: (i, c, 0))
    st = (NH // 2, 2 * HS, 2 * HS)
    return pl.pallas_call(
        _wkv_kernel,
        name="wkv",
        grid=(nb // nbw, seq // CHUNK),
        in_specs=[blk] * 6 + [pl.BlockSpec(st, lambda i, c: (0, 0, 0))],
        out_specs=[blk, pl.BlockSpec((nbw,) + st, lambda i, c: (i, 0, 0, 0))],
        out_shape=[jax.ShapeDtypeStruct((nb, seq, R), f32),
                   jax.ShapeDtypeStruct((nb,) + st, f32)],
        scratch_shapes=[pltpu.VMEM((nbw,) + st, f32)],
        compiler_params=_cparams(("parallel", "arbitrary")),
    )(r, k, v, kk, b, lw, s0)


def _merge_kernel(y_ref, r_ref, k_ref, v_ref, g_ref, cb_ref, gt_ref, x_ref, vec_ref, ln2_ref,
                  seg_ref, worw_ref, woc_ref, wo_ref, wr_ref, br_ref,
                  h_out, x2_out, lg_out):
    seg = seg_ref[...]
    vec = vec_ref[...]
    lnx_g, lnx_b, r_k = vec[0:1, :], vec[1:2, :], vec[2:3, :]
    tm = y_ref.shape[0]
    rows = [slice(s0, s0 + MERGE_SUB) for s0 in range(0, tm, MERGE_SUB)]
    gs = range(len(rows))
    y = [y_ref[sl, :] for sl in rows]
    mean = [_head_sum(y[n], seg) * (1.0 / HS) for n in gs]
    yc = [y[n] - mean[n] for n in gs]
    var = [_head_sum(yc[n] * yc[n], seg) * (1.0 / HS) for n in gs]
    yn = [yc[n] * lax.rsqrt(var[n] + LNX_EPS) * lnx_g + lnx_b for n in gs]
    rk = [_head_sum(r_ref[sl, :].astype(f32) * k_ref[sl, :].astype(f32) * r_k, seg) for sl in rows]
    z = [(yn[n] + rk[n] * v_ref[rows[n], :].astype(f32)) * g_ref[rows[n], :].astype(f32) for n in gs]
    y_a = [_bdot(z[n], worw_ref[...]) for n in gs]
    y_b = [_bdot(cb_ref[sl, :], woc_ref[...]) for sl in rows]
    merged = [jax.nn.sigmoid(gt_ref[rows[n], :D].astype(f32)) * y_a[n]
              + jax.nn.sigmoid(gt_ref[rows[n], D:].astype(f32)) * y_b[n] for n in gs]
    hres = [x_ref[rows[n], :] + _bdot(merged[n], wo_ref[...]) for n in gs]
    x2 = [hres[n] * lax.rsqrt(jnp.mean(hres[n] * hres[n], axis=-1, keepdims=True) + NORM_EPS)
          * ln2_ref[...] for n in gs]
    wh, wm, _ = _split3(wr_ref[...])
    nt = functools.partial(lax.dot_general, dimension_numbers=(((1,), (1,)), ((), ())),
                           preferred_element_type=f32)
    for n, sl in enumerate(rows):
        h_out[sl, :] = hres[n]
        x2_out[sl, :] = x2[n].astype(x2_out.dtype)
        xh, xm, _ = _split3(x2[n])
        lg_out[:, sl] = nt(wh, xh) + nt(wh, xm) + nt(wm, xh) + br_ref[...]


def _merge(y, r, k, v, g, cb, gt, x2d, vec, ln2, seg, worw, woc, wo, wr_t, br, tm):
    t = y.shape[0]

    def rows(n):
        return pl.BlockSpec((tm, n), lambda i: (i, 0))

    def whole(shape):
        return pl.BlockSpec(shape, lambda i: (0,) * len(shape))

    return pl.pallas_call(
        _merge_kernel,
        name="merge",
        grid=(t // tm,),
        in_specs=[rows(R)] * 6 + [rows(2 * D), rows(D), whole((8, R)), whole((1, D)),
                                   whole((R, R)), whole((R, D)), whole((R, D)), whole((D, D)),
                                   whole((NE, D)), whole((NE, 1))],
        out_specs=[rows(D), rows(D), pl.BlockSpec((NE, tm), lambda i: (0, i))],
        out_shape=[jax.ShapeDtypeStruct((t, D), f32), jax.ShapeDtypeStruct((t, D), bf16),
                   jax.ShapeDtypeStruct((NE, t), f32)],
        compiler_params=_cparams(("parallel",)),
    )(y, r, k, v, g, cb, gt, x2d, vec, ln2, seg, worw, woc, wo, wr_t, br)


def _route_kernel(lg_ref, e_out, w_out, pos_out, cnt_out):
    lg = lg_ref[...]
    tr = lg.shape[1]
    erow = lax.broadcasted_iota(jnp.int32, lg.shape, 0)
    work = lg
    hits, vals, idxs = [], [], []
    for _ in range(TOPK):
        m = jnp.max(work, axis=0, keepdims=True)
        idx = jnp.min(jnp.where(work == m, erow, NE), axis=0, keepdims=True)
        hit = erow == idx
        hits.append(hit)
        vals.append(m)
        idxs.append(idx)
        work = jnp.where(hit, -jnp.inf, work)
    ex = [jnp.exp(vk - vals[0]) for vk in vals]
    den = ex[0] + ex[1] + ex[2] + ex[3]
    multi = jnp.where(hits[0] | hits[1] | hits[2] | hits[3], 1.0, 0.0)
    ti = lax.broadcasted_iota(jnp.int32, (tr, tr), 0)
    tj = lax.broadcasted_iota(jnp.int32, (tr, tr), 1)
    before = jnp.where(ti < tj, 1.0, 0.0).astype(bf16)
    excl = jnp.dot(multi.astype(bf16), before, preferred_element_type=f32)
    total = jnp.sum(multi, axis=1, keepdims=True)
    units = jnp.floor((total + (ALIGN - 1)) * (1.0 / ALIGN))
    ei = lax.broadcasted_iota(jnp.int32, (NE, NE), 0)
    ej = lax.broadcasted_iota(jnp.int32, (NE, NE), 1)
    below = jnp.where(ej < ei, 1.0, 0.0).astype(bf16)
    off = ALIGN * jnp.dot(below, jnp.broadcast_to(units, (NE, 128)).astype(bf16),
                          preferred_element_type=f32)[:, 0:1]
    for kk in range(TOPK):
        e_out[kk:kk + 1, :] = idxs[kk]
        w_out[kk:kk + 1, :] = ex[kk] / den
        pos_out[kk:kk + 1, :] = jnp.sum(jnp.where(hits[kk], excl + off, 0.0), axis=0,
                                        keepdims=True).astype(jnp.int32)
    cnt_out[...] = jnp.broadcast_to(total, cnt_out.shape).astype(jnp.int32)


def _route(logits_t):
    t = logits_t.shape[1]
    sel = pl.BlockSpec((TOPK, TT), lambda i: (0, i))
    return pl.pallas_call(
        _route_kernel,
        name="route",
        grid=(t // TT,),
        in_specs=[pl.BlockSpec((NE, TT), lambda i: (0, i))],
        out_specs=[sel, sel, sel, pl.BlockSpec((None, NE, 128), lambda i: (i, 0, 0))],
        out_shape=[jax.ShapeDtypeStruct((TOPK, t), jnp.int32),
                   jax.ShapeDtypeStruct((TOPK, t), f32),
                   jax.ShapeDtypeStruct((TOPK, t), jnp.int32),
                   jax.ShapeDtypeStruct((t // TT, NE, 128), jnp.int32)],
        compiler_params=_cparams(("parallel",)),
    )(logits_t)


def _run_copies(tile, cnt_ref, make_copy, act):
    def per_expert(e, carry):
        cnt = pl.multiple_of(cnt_ref[tile * NE + e], ALIGN)

        @pl.when(cnt > 0)
        def _():
            act(make_copy(e, cnt))
        return carry
    lax.fori_loop(0, NE, per_expert, 0)


def _dispatch_kernel(cnt_ref, run_ref, off_ref, tot_ref, tail_ref, pos_ref, x2_ref, xs_ref,
                     sbuf, zbuf, sem, zsem):
    i = pl.program_id(0)
    last = pl.num_programs(0) - 1
    cur = i % 2

    @pl.when(i == 0)
    def _():
        zbuf[...] = jnp.zeros_like(zbuf)

        def zero_copy(start):
            return pltpu.make_async_copy(zbuf, xs_ref.at[pl.ds(pl.multiple_of(start, EBLK), EBLK)], zsem)

        for e in range(NE):
            @pl.when(tail_ref[e] >= 0)
            def _():
                zero_copy(tail_ref[e]).start()
        for e in range(NE):
            @pl.when(tail_ref[e] >= 0)
            def _():
                zero_copy(tail_ref[e]).wait()
        n_blk = xs_ref.shape[0] // EBLK
        lax.fori_loop(tail_ref[NE], n_blk, lambda blk, c: (zero_copy(blk * EBLK).start(), c)[1], 0)
        lax.fori_loop(tail_ref[NE], n_blk, lambda blk, c: (zero_copy(blk * EBLK).wait(), c)[1], 0)

    def slab(tile, half):
        def make(e, rows):
            src = pl.multiple_of(off_ref[tile * NE + e], ALIGN)
            dst = pl.multiple_of(run_ref[tile * NE + e], ALIGN)
            return pltpu.make_async_copy(sbuf.at[half, pl.ds(src, rows)],
                                         xs_ref.at[pl.ds(dst, rows)], sem.at[half])
        return make

    def drain(tile, half):
        n = pl.multiple_of(tot_ref[tile], ALIGN)
        pltpu.make_async_copy(sbuf.at[half, pl.ds(0, n)], xs_ref.at[pl.ds(0, n)], sem.at[half]).wait()

    @pl.when(i >= 2)
    def _():
        drain(i - 2, cur)

    pos = pos_ref[...]
    xt = x2_ref[...]
    for c in range(0, SB, SCHUNK):
        slot = lax.broadcasted_iota(jnp.int32, (SCHUNK, TT), 0) + c
        place = jnp.zeros((SCHUNK, TT), f32)
        for kk in range(TOPK):
            place = jnp.where(slot == pos[kk:kk + 1, :], 1.0, place)
        sbuf[cur, c:c + SCHUNK] = _pack_pairs(jnp.dot(place.astype(bf16), xt,
                                                      preferred_element_type=f32))
    _run_copies(i, cnt_ref, slab(i, cur), lambda cp: cp.start())

    @pl.when(i == last)
    def _():
        @pl.when(i >= 1)
        def _():
            drain(i - 1, 1 - cur)
        drain(i, cur)


def _dispatch(cnt16, run_start, off16, tot, tail, pos, x2, n_rows):
    t = x2.shape[0]
    return pl.pallas_call(
        _dispatch_kernel,
        name="dispatch",
        grid_spec=pltpu.PrefetchScalarGridSpec(
            num_scalar_prefetch=5, grid=(t // TT,),
            in_specs=[pl.BlockSpec((TOPK, TT), lambda i, *_: (0, i)),
                      pl.BlockSpec((TT, D), lambda i, *_: (i, 0))],
            out_specs=pl.BlockSpec(memory_space=pl.ANY),
            scratch_shapes=[pltpu.VMEM((2, SB, DH), jnp.uint32), pltpu.VMEM((EBLK, DH), jnp.uint32),
                            pltpu.SemaphoreType.DMA((2,)), pltpu.SemaphoreType.DMA(())]),
        out_shape=jax.ShapeDtypeStruct((n_rows, DH), jnp.uint32),
        compiler_params=_cparams(("arbitrary",)),
    )(cnt16, run_start, off16, tot, tail, pos, x2)


def _expert_kernel(be_ref, nu_ref, fx_ref, xs_ref, wg_ref, bg_ref, wu_ref, bu_ref, wd_ref, bd_ref,
                   ys_ref, wgb, wub, wdb):
    del be_ref
    i = pl.program_id(0)

    @pl.when(fx_ref[i] == 1)
    def _():
        wgb[...] = wg_ref[...].astype(bf16)
        wub[...] = wu_ref[...].astype(bf16)
        wdb[...] = wd_ref[...].astype(bf16)

    @pl.when(i < nu_ref[0])
    def _():
        xlo, xhi = _unpack_pairs(xs_ref[...])
        dot = functools.partial(jnp.dot, preferred_element_type=f32)
        gt = jnp.minimum(dot(xlo, wgb[:DH, :]) + dot(xhi, wgb[DH:, :]) + bg_ref[...], LIMIT)
        up = jnp.clip(dot(xlo, wub[:DH, :]) + dot(xhi, wub[DH:, :]) + bu_ref[...], -LIMIT, LIMIT)
        act = (up + 1.0) * (gt * jax.nn.sigmoid(gt * ALPHA))
        y = dot(act.astype(bf16), wdb[...]) + bd_ref[...]
        ys_ref[...] = _pack_pairs(y.astype(bf16).astype(f32))

    @pl.when(i >= nu_ref[0])
    def _():
        ys_ref[...] = jnp.zeros_like(ys_ref)


def _experts(blk_e, n_used, first, xs, wg, bg, wu, bu, wd, bd):
    p = xs.shape[0]
    wspec = pl.BlockSpec((None, D, D), lambda i, be, nu, fx: (be[i], 0, 0))
    bspec = pl.BlockSpec((None, 1, D), lambda i, be, nu, fx: (be[i], 0, 0))
    xrows = pl.BlockSpec((EBLK, DH), lambda i, be, nu, fx: (jnp.where(i < nu[0], i, 0), 0))
    yrows = pl.BlockSpec((EBLK, DH), lambda i, be, nu, fx: (i, 0))
    return pl.pallas_call(
        _expert_kernel,
        name="experts",
        grid_spec=pltpu.PrefetchScalarGridSpec(
            num_scalar_prefetch=3, grid=(p // EBLK,),
            in_specs=[xrows, wspec, bspec, wspec, bspec, wspec, bspec],
            out_specs=yrows,
            scratch_shapes=[pltpu.VMEM((D, D), bf16)] * 3),
        out_shape=jax.ShapeDtypeStruct((p, DH), jnp.uint32),
        compiler_params=_cparams(("arbitrary",)),
    )(blk_e, n_used, first, xs, wg, bg, wu, bu, wd, bd)


def _combine_kernel(cnt_ref, run_ref, off_ref, tot_ref, ys_ref, pos_ref, w_ref, h_ref, g_ref, o_ref,
                    rbuf, sem):
    i = pl.program_id(0)
    cur = i % 2

    def fetch(tile, half):
        def slab(e, rows):
            src = pl.multiple_of(run_ref[tile * NE + e], ALIGN)
            dst = pl.multiple_of(off_ref[tile * NE + e], ALIGN)
            return pltpu.make_async_copy(ys_ref.at[pl.ds(src, rows)],
                                         rbuf.at[half, pl.ds(dst, rows)], sem.at[half])
        return slab

    @pl.when(i == 0)
    def _():
        rbuf[...] = jnp.zeros_like(rbuf)
        _run_copies(0, cnt_ref, fetch(0, 0), lambda cp: cp.start())

    @pl.when(i + 1 < pl.num_programs(0))
    def _():
        _run_copies(i + 1, cnt_ref, fetch(i + 1, 1 - cur), lambda cp: cp.start())

    n = pl.multiple_of(tot_ref[i], ALIGN)
    pltpu.make_async_copy(ys_ref.at[pl.ds(0, n)], rbuf.at[cur, pl.ds(0, n)], sem.at[cur]).wait()

    pos = pos_ref[...]
    w = w_ref[...]
    moe_lo = jnp.zeros((TT, DH), f32)
    moe_hi = jnp.zeros((TT, DH), f32)
    for c in range(0, SB, SCHUNK):
        slot = lax.broadcasted_iota(jnp.int32, (TT, SCHUNK), 1) + c
        gate = jnp.zeros((TT, SCHUNK), f32)
        for kk in range(TOPK):
            gate = jnp.where(slot == pos[:, kk:kk + 1], w[:, kk:kk + 1], gate)
        gb = gate.astype(bf16)
        lo, hi = _unpack_pairs(rbuf[cur, c:c + SCHUNK])
        moe_lo = moe_lo + jnp.dot(gb, lo, preferred_element_type=f32)
        moe_hi = moe_hi + jnp.dot(gb, hi, preferred_element_type=f32)
    acc = h_ref[...] + jnp.concatenate([moe_lo, moe_hi], axis=1)
    o_ref[...] = acc * lax.rsqrt(jnp.mean(acc * acc, axis=-1, keepdims=True) + NORM_EPS) * g_ref[...]


def _combine(cnt16, run_start, off16, tot, ys, pos_t, gates_t, hres, lnf):
    t = hres.shape[0]
    return pl.pallas_call(
        _combine_kernel,
        name="combine",
        grid_spec=pltpu.PrefetchScalarGridSpec(
            num_scalar_prefetch=4, grid=(t // TT,),
            in_specs=[pl.BlockSpec(memory_space=pl.ANY),
                      pl.BlockSpec((TT, TOPK), lambda i, *_: (i, 0)),
                      pl.BlockSpec((TT, TOPK), lambda i, *_: (i, 0)),
                      pl.BlockSpec((TT, D), lambda i, *_: (i, 0)),
                      pl.BlockSpec((1, D), lambda i, *_: (0, 0))],
            out_specs=pl.BlockSpec((TT, D), lambda i, *_: (i, 0)),
            scratch_shapes=[pltpu.VMEM((2, SB, DH), jnp.uint32), pltpu.SemaphoreType.DMA((2,))]),
        out_shape=jax.ShapeDtypeStruct((t, D), f32),
        compiler_params=_cparams(("arbitrary",)),
    )(cnt16, run_start, off16, tot, ys, pos_t, gates_t, hres, lnf)


def _row_tile(n, want):
    t = min(n, want)
    assert n % t == 0
    return t


def kernel(x, meta_tokens, ln1_g, w_in, mu_r, mu_k, mu_v, mu_w, mu_a, mu_g, w0, w_w1, w_w2, a0, w_a1, w_a2, w_g1, w_g2, k_k, k_a, r_k, lnx_g, lnx_b, w_o_rwkv, conv_w, w_o_conv, w_o, ln2_g, w_router, b_router, w_e_gate, b_e_gate, w_e_up, b_e_up, w_e_down, b_e_down, lnf_g):
    nb, seq, _ = x.shape
    t = nb * seq
    assert ln1_g.shape[0] == 1, "single layer"

    muw, mua, mug = mu_w[0][:, None], mu_a[0][:, None], mu_g[0][:, None]
    lora_cur = jnp.concatenate([(1 - muw) * w_w1[0], (1 - mua) * w_a1[0], (1 - mug) * w_g1[0]], axis=1)
    lora_prev = jnp.concatenate([muw * w_w1[0], mua * w_a1[0], mug * w_g1[0]], axis=1)
    zrow = jnp.zeros((1, R), f32)
    p = {
        "ln1": ln1_g[0][None, :],
        "w_in": jnp.concatenate([w_in[0][:, :6 * R], lora_cur, lora_prev, w_in[0][:, 6 * R:]],
                                axis=1).astype(bf16),
        "vec_mix": jnp.concatenate([mu_r, mu_k, mu_v, w0, a0, k_k, k_a, zrow], axis=0),
        "ww2": w_w2[0].astype(bf16), "wa2": w_a2[0].astype(bf16), "wg2": w_g2[0].astype(bf16),
        "cw": jnp.concatenate([conv_w[0], jnp.zeros((5, R), f32)], axis=0),
        "seg": (jnp.arange(R)[:, None] // HS == jnp.arange(R)[None, :] // HS).astype(bf16),
    }
    vec_merge = jnp.concatenate([lnx_g, lnx_b, r_k, jnp.zeros((5, R), f32)], axis=0)

    meta_out = _projmix(meta_tokens.astype(f32), NMETA, jnp.zeros((MARGIN, PW), f32), p, NMETA)
    pad = lambda z: jnp.pad(z, ((CHUNK - NMETA, 0), (0, 0)))[None]
    r_m, k_m, v_m, kk_m, b_m, lw_m = (pad(z) for z in meta_out[1:7])
    _, s_meta = _wkv(r_m, k_m, v_m, kk_m, b_m, lw_m, jnp.zeros((NH // 2, 2 * HS, 2 * HS), f32), 1)

    x2d = x.reshape(t, D)
    gt, r, k, v, kk, b, lw, g, cb, _ = _projmix(x2d, seq, meta_out[9], p, _row_tile(seq, 512))
    as3 = lambda z: z.reshape(nb, seq, R)
    y, _ = _wkv(as3(r), as3(k), as3(v), as3(kk), as3(b), as3(lw), s_meta[0], WKV_SEQS)
    hres, x2, logits_t = _merge(
        y.reshape(t, R), r, k, v, g, cb, gt, x2d, vec_merge, ln2_g[0][None, :], p["seg"],
        w_o_rwkv[0].astype(bf16), w_o_conv[0].astype(bf16), w_o[0].astype(bf16),
        w_router[0].T, b_router[0][:, None], _row_tile(t, 512))

    assert t % TT == 0
    nt = t // TT
    _, gates, pos, cnt = _route(logits_t)
    cnt = cnt[:, :, 0]
    cnt16 = ((cnt + ALIGN - 1) // ALIGN) * ALIGN
    base16 = jnp.cumsum(cnt16, axis=0) - cnt16
    tot16 = jnp.sum(cnt16, axis=0)
    padded = ((tot16 + EBLK - 1) // EBLK) * EBLK
    pend = jnp.cumsum(padded)
    pstart = pend - padded
    run_start = pstart[None, :] + base16
    off16 = jnp.cumsum(cnt16, axis=1) - cnt16
    n_rows = -(-(t * TOPK + nt * NE * (ALIGN - 1) + NE * EBLK) // EBLK) * EBLK
    n_blk = n_rows // EBLK
    blk_start = jnp.arange(n_blk, dtype=jnp.int32) * EBLK
    blk_e = jnp.minimum(jnp.sum(pend[None, :] <= blk_start[:, None], axis=1), NE - 1).astype(jnp.int32)
    n_used = (pend[NE - 1:] // EBLK).astype(jnp.int32)
    tail = jnp.concatenate([jnp.where(padded > 0, pend - EBLK, -1), n_used]).astype(jnp.int32)
    flat = lambda z: z.reshape(-1).astype(jnp.int32)
    tot = jnp.sum(cnt16, axis=1).astype(jnp.int32)
    xs = _dispatch(flat(cnt16), flat(run_start), flat(off16), tot, tail, pos, x2, n_rows)
    first = jnp.concatenate([jnp.ones((1,), jnp.int32), (blk_e[1:] != blk_e[:-1]).astype(jnp.int32)])
    ys = _experts(blk_e, n_used, first, xs,
                  w_e_gate[0], b_e_gate[0][:, None, :],
                  w_e_up[0], b_e_up[0][:, None, :],
                  w_e_down[0], b_e_down[0][:, None, :])
    out = _combine(flat(cnt16), flat(run_start), flat(off16), tot, ys, pos.T, gates.T, hres, lnf_g[None, :])
    return out.reshape(nb, seq, D)
```

```python
import functools

import jax
import jax.numpy as jnp
from jax import lax
from jax.experimental import pallas as pl
from jax.experimental.pallas import tpu as pltpu

D = 1024
R = 512
NH = 8
HS = 64
NE = 32
TOPK = 4
NMETA = 16
CHUNK = 64
WKV_SEQS = 4
EBLK = 1024
EHALF = 512
ESUB = 256
TT = 512
ALIGN = 8
SB = 4 * TT + 256
DH = D // 2
MERGE_SUB = 256
SCHUNK = 768
NORM_EPS = 1e-5
LNX_EPS = 64e-5
ALPHA = 1.702
LIMIT = 7.0
DECAY_SCALE = 0.6065306597126334
LORA_W, LORA_A, LORA_G = 64, 64, 128
LORA = LORA_W + LORA_A + LORA_G
NPROJ = 3 * R + 3 * R + 2 * D + 2 * LORA
VMEM_LIMIT = 56 * 1024 * 1024

f32 = jnp.float32
bf16 = jnp.bfloat16


def _bdot(a, b):
    return jnp.dot(a.astype(bf16), b.astype(bf16), preferred_element_type=f32)


def _bdot_nt(a, b):
    return lax.dot_general(a.astype(bf16), b.astype(bf16), (((1,), (1,)), ((), ())),
                           preferred_element_type=f32)


def _bdot_tn(a, b):
    return lax.dot_general(a.astype(bf16), b.astype(bf16), (((0,), (0,)), ((), ())),
                           preferred_element_type=f32)


def _split3(x):
    h = x.astype(bf16)
    r1 = x - h.astype(f32)
    m = r1.astype(bf16)
    l = (r1 - m.astype(f32)).astype(bf16)
    return h, m, l


def _head_sum(x, seg):
    return jnp.dot(x.astype(bf16), seg, preferred_element_type=f32)


def _pack_pairs(x):
    a = lax.bitcast_convert_type(x[:, :DH], jnp.uint32)
    b = lax.bitcast_convert_type(x[:, DH:], jnp.uint32)
    return (a & jnp.uint32(0xFFFF0000)) | (b >> 16)


def _unpack_pairs(p):
    lo = lax.bitcast_convert_type(p & jnp.uint32(0xFFFF0000), f32)
    hi = lax.bitcast_convert_type(p << 16, f32)
    return lo.astype(bf16), hi.astype(bf16)


def _cparams(sem):
    return pltpu.CompilerParams(dimension_semantics=sem, vmem_limit_bytes=VMEM_LIMIT)


MARGIN = 16
PW = 3 * R + 3 * R + 2 * LORA


def _mix_rows(pbuf, a, n, vec, cw, seg, ww2, wa2, wg2, outs, o):
    r_out, k_out, v_out, kk_out, b_out, lw_out, g_out, cb_out = outs
    mu_r, mu_k, mu_v = vec[0:1, :], vec[1:2, :], vec[2:3, :]
    w0, a0, k_k, k_a = vec[3:4, :], vec[4:5, :], vec[5:6, :], vec[6:7, :]
    cur = pbuf[a:a + n, :]
    prev = pbuf[a - 1:a - 1 + n, :]
    r = cur[:, :R] + (prev[:, :R] - cur[:, :R]) * mu_r
    k = cur[:, R:2 * R] + (prev[:, R:2 * R] - cur[:, R:2 * R]) * mu_k
    v = cur[:, 2 * R:3 * R] + (prev[:, 2 * R:3 * R] - cur[:, 2 * R:3 * R]) * mu_v
    lo = 6 * R
    mixed = cur[:, lo:lo + LORA] + prev[:, lo + LORA:lo + 2 * LORA]
    hw = jnp.tanh(mixed[:, :LORA_W])
    ha = mixed[:, LORA_W:LORA_W + LORA_A]
    hg = jax.nn.sigmoid(mixed[:, LORA_W + LORA_A:])
    lw = -DECAY_SCALE * jax.nn.sigmoid(w0 + _bdot(hw, ww2))
    aa = jax.nn.sigmoid(a0 + _bdot(ha, wa2))
    g = _bdot(hg, wg2)
    kk = k * k_k
    norm = jnp.sqrt(_head_sum(kk * kk, seg))
    kk = kk / jnp.maximum(norm, 1e-12)
    k = k * (1.0 + (aa - 1.0) * k_a)
    rows = slice(o, o + n)
    r_out[rows, :] = r.astype(r_out.dtype)
    k_out[rows, :] = k.astype(k_out.dtype)
    v_out[rows, :] = v.astype(v_out.dtype)
    kk_out[rows, :] = kk.astype(kk_out.dtype)
    b_out[rows, :] = (kk * aa).astype(b_out.dtype)
    lw_out[rows, :] = lw
    g_out[rows, :] = g.astype(g_out.dtype)
    prev2 = pbuf[a - 2:a - 2 + n, 4 * R:6 * R]
    u0 = cur[:, 4 * R:5 * R] * cur[:, 5 * R:6 * R]
    u1 = prev[:, 4 * R:5 * R] * prev[:, 5 * R:6 * R]
    u2 = prev2[:, :R] * prev2[:, R:]
    conv = cw[2:3, :] * u0 + cw[1:2, :] * u1 + cw[0:1, :] * u2
    cb_out[rows, :] = (cur[:, 3 * R:4 * R] * conv).astype(cb_out.dtype)


def _projmix_kernel(tiles_per_seq, group, x_ref, g_ref, w_ref, init_ref, vec_ref, ww2_ref, wa2_ref,
                    wg2_ref, cw_ref, seg_ref, gt_out, r_out, k_out, v_out, kk_out, b_out, lw_out,
                    g_out, cb_out, tail_out, pbuf_a, pbuf_b):
    i = pl.program_id(0)
    tm = x_ref.shape[0]

    @pl.when(i == 0)
    def _():
        pbuf_a[...] = jnp.zeros_like(pbuf_a)
        pbuf_b[...] = jnp.zeros_like(pbuf_b)

    def step(pcur, pprv):
        first = (i % tiles_per_seq) == 0
        pcur[0:MARGIN, :] = jnp.where(first, init_ref[...], pprv[tm:tm + MARGIN, :])
        tail_out[...] = pprv[tm:tm + MARGIN, :]

        x = x_ref[...]
        xb = (x * lax.rsqrt(jnp.mean(x * x, axis=-1, keepdims=True) + NORM_EPS)
              * g_ref[...]).astype(bf16)
        outs = (r_out, k_out, v_out, kk_out, b_out, lw_out, g_out, cb_out)
        vec, cw, seg = vec_ref[...], cw_ref[...], seg_ref[...]
        ww2, wa2, wg2 = ww2_ref[...], wa2_ref[...], wg2_ref[...]

        def project(c):
            res = jnp.dot(xb, w_ref[:, c:c + 512], preferred_element_type=f32)
            if c < PW:
                pcur[MARGIN:MARGIN + tm, c:c + 512] = res
            else:
                gt_out[:, c - PW:c - PW + 512] = res.astype(gt_out.dtype)
            return res[0:8, :] * 0.0

        cols = list(range(0, NPROJ, 512))
        groups = list(range(0, tm, group))
        per = -(-len(cols) // len(groups))
        tie = jnp.zeros((8, R), f32)
        for n, o in enumerate(groups):
            for c in cols[n * per:(n + 1) * per]:
                tie = project(c)
            _mix_rows(pprv, MARGIN + o, group, vec + tie, cw + tie, seg, ww2, wa2, wg2, outs, o)

    @pl.when(i % 2 == 0)
    def _():
        step(pbuf_a, pbuf_b)

    @pl.when(i % 2 == 1)
    def _():
        step(pbuf_b, pbuf_a)


def _projmix(x2d, seq_len, init, p, tm):
    t = x2d.shape[0]
    n = t // tm
    group = min(tm, 64)

    def whole(shape):
        return pl.BlockSpec(shape, lambda i: (0,) * len(shape))

    lag = pl.BlockSpec((tm, R), lambda i: (jnp.maximum(i - 1, 0), 0))
    return pl.pallas_call(
        functools.partial(_projmix_kernel, seq_len // tm, group),
        name="projmix",
        grid=(n + 1,),
        in_specs=[pl.BlockSpec((tm, D), lambda i: (jnp.minimum(i, n - 1), 0)),
                  whole((1, D)),
                  pl.BlockSpec((D, NPROJ), lambda i: (0, 0), pipeline_mode=pl.Buffered(1)),
                  whole((MARGIN, PW)), whole((8, R)), whole((LORA_W, R)), whole((LORA_A, R)),
                  whole((LORA_G, R)), whole((8, R)), whole((R, R))],
        out_specs=[pl.BlockSpec((tm, 2 * D), lambda i: (jnp.minimum(i, n - 1), 0))] + [lag] * 8
                  + [whole((MARGIN, PW))],
        out_shape=[jax.ShapeDtypeStruct((t, 2 * D), bf16)]
                  + [jax.ShapeDtypeStruct((t, R), f32 if m == 5 else bf16) for m in range(8)]
                  + [jax.ShapeDtypeStruct((MARGIN, PW), f32)],
        scratch_shapes=[pltpu.VMEM((MARGIN + tm, PW), f32)] * 2,
        compiler_params=_cparams(("arbitrary",)),
    )(x2d, p["ln1"], p["w_in"], init, p["vec_mix"], p["ww2"], p["wa2"], p["wg2"], p["cw"], p["seg"])


def _wkv_kernel(r_ref, k_ref, v_ref, kk_ref, b_ref, lw_ref, s0_ref, y_ref, sT_ref, s_scr):
    c = pl.program_id(1)
    nbw = r_ref.shape[0]

    @pl.when(c == 0)
    def _():
        for j in range(nbw):
            s_scr[j] = s0_ref[...]

    C = CHUNK
    W2 = 2 * HS
    ri = lax.broadcasted_iota(jnp.int32, (C, C), 0)
    ci = lax.broadcasted_iota(jnp.int32, (C, C), 1)
    tri = jnp.where(ri >= ci, 1.0, 0.0).astype(bf16)
    rp = lax.broadcasted_iota(jnp.int32, (C, W2), 0)
    cp = lax.broadcasted_iota(jnp.int32, (C, W2), 1)
    lo = cp < HS
    lo1 = lax.broadcasted_iota(jnp.int32, (1, W2), 1) < HS
    sp = jnp.where(lo, cp, cp - HS)
    strict = rp > sp
    incl = rp >= sp
    eye = jnp.where(rp == sp, 1.0, 0.0).astype(f32)
    rq = lax.broadcasted_iota(jnp.int32, (W2, W2), 0)
    cq = lax.broadcasted_iota(jnp.int32, (W2, W2), 1)
    diag_blocks = (rq < HS) == (cq < HS)
    dot = functools.partial(jnp.dot, preferred_element_type=f32)

    def first(z):
        return jnp.where(lo1, z, jnp.zeros_like(z))

    def second(z):
        return jnp.where(lo1, jnp.zeros_like(z), z)

    def bdiag(z):
        return jnp.concatenate([first(z), second(z)], axis=0)

    def adiag(z):
        return jnp.concatenate([second(z), first(z)], axis=0)

    at, rt, bt, kt, bh, kh, vv, g_tot = [], [], [], [], [], [], [], []
    for j in range(nbw):
        lw = lw_ref[j]
        h, m, l = _split3(lw)
        cum = dot(tri, h) + dot(tri, m) + dot(tri, l)
        tot = cum[C - 1:C, :]
        k = k_ref[j].astype(f32)
        b = b_ref[j].astype(f32)
        g_inv = jnp.exp(-cum)
        tail = jnp.exp(tot - cum)
        rt.append(r_ref[j].astype(f32) * jnp.exp(cum))
        kt.append(k * g_inv)
        bt.append(b * g_inv)
        at.append(-kk_ref[j].astype(f32) * jnp.exp(cum - lw))
        bh.append(b * tail)
        kh.append(k * tail)
        vv.append(v_ref[j].astype(f32))
        g_tot.append(jnp.exp(tot))

    chains = [(j, p) for j in range(nbw) for p in range(NH // 2)]
    cs = range(len(chains))
    sl = lambda p: slice(p * W2, (p + 1) * W2)
    s_old = [s_scr[j, p] for j, p in chains]
    lhs = [jnp.concatenate([at[j][:, sl(p)], rt[j][:, sl(p)]], axis=0) for j, p in chains]
    rhs_n = [jnp.concatenate([bt[j][:, sl(p)], kt[j][:, sl(p)]], axis=0).astype(bf16) for j, p in chains]
    rhs_s = [jnp.concatenate([kt[j][:, sl(p)], bt[j][:, sl(p)]], axis=0).astype(bf16) for j, p in chains]
    lhs_b = [x.astype(bf16) for x in lhs]
    am1 = [_bdot_nt(first(lhs[n]), rhs_n[n]) for n in cs]
    am2 = [_bdot_nt(second(lhs[n]), rhs_s[n]) for n in cs]
    a_ab = [jnp.where(strict, jnp.where(lo, am1[n][:C], am2[n][:C]), 0.0) for n in cs]
    a_ak = [jnp.where(strict, jnp.where(lo, am2[n][:C], am1[n][:C]), 0.0) for n in cs]
    a_r1 = [jnp.where(incl, am1[n][C:], 0.0) for n in cs]
    a_r2 = [jnp.where(incl, am2[n][C:], 0.0) for n in cs]
    pb = [a_ab[n].astype(bf16) for n in cs]
    pw = [dot(pb[n], bdiag(pb[n])) for n in cs]
    x = [eye + a_ab[n] for n in cs]
    for _ in range(4):
        pb = [pw[n].astype(bf16) for n in cs]
        both = [dot(jnp.concatenate([pb[n], x[n].astype(bf16)], axis=0), bdiag(pb[n])) for n in cs]
        pw = [both[n][:C] for n in cs]
        x = [x[n] + both[n][C:] for n in cs]
    x = [x[n] + dot(x[n].astype(bf16), bdiag(pw[n].astype(bf16))) for n in cs]
    vp = [vv[j][:, sl(p)] for j, p in chains]
    sh = [_bdot_nt(lhs_b[n], s_old[n]) for n in cs]
    av = [dot(a_ak[n].astype(bf16), adiag(vp[n].astype(bf16))) for n in cs]
    u = [dot(x[n].astype(bf16), bdiag((sh[n][:C] + av[n]).astype(bf16))) for n in cs]
    ub = [u[n].astype(bf16) for n in cs]
    vb = [vp[n].astype(bf16) for n in cs]
    mix4 = [jnp.concatenate([first(ub[n]), first(vb[n]), second(vb[n]), second(ub[n])], axis=0)
            for n in cs]
    ys = [sh[n][C:] + dot(jnp.concatenate([a_r1[n], a_r2[n]], axis=1).astype(bf16), mix4[n])
          for n in cs]
    uv = [jnp.concatenate([ub[n], vb[n]], axis=0) for n in cs]
    bk = [jnp.concatenate([bh[j][:, sl(p)], kh[j][:, sl(p)]], axis=0) for j, p in chains]
    s_new = [s_old[n] * g_tot[j][:, sl(p)] + jnp.where(diag_blocks, _bdot_tn(uv[n], bk[n]), 0.0)
             for n, (j, p) in enumerate(chains)]
    for n, (j, p) in enumerate(chains):
        s_scr[j, p] = s_new[n]
        y_ref[j, :, sl(p)] = ys[n]

    @pl.when(c == pl.num_programs(1) - 1)
    def _():
        sT_ref[...] = s_scr[...]


def _wkv(r, k, v, kk, b, lw, s0, nbw):
    nb, seq, _ = r.shape
    assert nb % nbw == 0 and seq % CHUNK == 0
    blk = pl.BlockSpec((nbw, CHUNK, R), lambda i, c: (i, c, 0))
    st = (NH // 2, 2 * HS, 2 * HS)
    return pl.pallas_call(
        _wkv_kernel,
        name="wkv",
        grid=(nb // nbw, seq // CHUNK),
        in_specs=[blk] * 6 + [pl.BlockSpec(st, lambda i, c: (0, 0, 0))],
        out_specs=[blk, pl.BlockSpec((nbw,) + st, lambda i, c: (i, 0, 0, 0))],
        out_shape=[jax.ShapeDtypeStruct((nb, seq, R), f32),
                   jax.ShapeDtypeStruct((nb,) + st, f32)],
        scratch_shapes=[pltpu.VMEM((nbw,) + st, f32)],
        compiler_params=_cparams(("parallel", "arbitrary")),
    )(r, k, v, kk, b, lw, s0)


def _merge_kernel(y_ref, r_ref, k_ref, v_ref, g_ref, cb_ref, gt_ref, x_ref, vec_ref, ln2_ref,
                  seg_ref, worw_ref, woc_ref, wo_ref, wr_ref, br_ref,
                  h_out, x2_out, lg_out):
    seg = seg_ref[...]
    vec = vec_ref[...]
    lnx_g, lnx_b, r_k = vec[0:1, :], vec[1:2, :], vec[2:3, :]
    tm = y_ref.shape[0]
    rows = [slice(s0, s0 + MERGE_SUB) for s0 in range(0, tm, MERGE_SUB)]
    gs = range(len(rows))
    y = [y_ref[sl, :] for sl in rows]
    mean = [_head_sum(y[n], seg) * (1.0 / HS) for n in gs]
    yc = [y[n] - mean[n] for n in gs]
    var = [_head_sum(yc[n] * yc[n], seg) * (1.0 / HS) for n in gs]
    yn = [yc[n] * lax.rsqrt(var[n] + LNX_EPS) * lnx_g + lnx_b for n in gs]
    rk = [_head_sum(r_ref[sl, :].astype(f32) * k_ref[sl, :].astype(f32) * r_k, seg) for sl in rows]
    z = [(yn[n] + rk[n] * v_ref[rows[n], :].astype(f32)) * g_ref[rows[n], :].astype(f32) for n in gs]
    y_a = [_bdot(z[n], worw_ref[...]) for n in gs]
    y_b = [_bdot(cb_ref[sl, :], woc_ref[...]) for sl in rows]
    merged = [jax.nn.sigmoid(gt_ref[rows[n], :D].astype(f32)) * y_a[n]
              + jax.nn.sigmoid(gt_ref[rows[n], D:].astype(f32)) * y_b[n] for n in gs]
    hres = [x_ref[rows[n], :] + _bdot(merged[n], wo_ref[...]) for n in gs]
    x2 = [hres[n] * lax.rsqrt(jnp.mean(hres[n] * hres[n], axis=-1, keepdims=True) + NORM_EPS)
          * ln2_ref[...] for n in gs]
    wh, wm, _ = _split3(wr_ref[...])
    nt = functools.partial(lax.dot_general, dimension_numbers=(((1,), (1,)), ((), ())),
                           preferred_element_type=f32)
    for n, sl in enumerate(rows):
        h_out[sl, :] = hres[n]
        x2_out[sl, :] = x2[n].astype(x2_out.dtype)
        xh, xm, _ = _split3(x2[n])
        lg_out[:, sl] = nt(wh, xh) + nt(wh, xm) + nt(wm, xh) + br_ref[...]


def _merge(y, r, k, v, g, cb, gt, x2d, vec, ln2, seg, worw, woc, wo, wr_t, br, tm):
    t = y.shape[0]

    def rows(n):
        return pl.BlockSpec((tm, n), lambda i: (i, 0))

    def whole(shape):
        return pl.BlockSpec(shape, lambda i: (0,) * len(shape))

    return pl.pallas_call(
        _merge_kernel,
        name="merge",
        grid=(t // tm,),
        in_specs=[rows(R)] * 6 + [rows(2 * D), rows(D), whole((8, R)), whole((1, D)),
                                   whole((R, R)), whole((R, D)), whole((R, D)), whole((D, D)),
                                   whole((NE, D)), whole((NE, 1))],
        out_specs=[rows(D), rows(D), pl.BlockSpec((NE, tm), lambda i: (0, i))],
        out_shape=[jax.ShapeDtypeStruct((t, D), f32), jax.ShapeDtypeStruct((t, D), bf16),
                   jax.ShapeDtypeStruct((NE, t), f32)],
        compiler_params=_cparams(("parallel",)),
    )(y, r, k, v, g, cb, gt, x2d, vec, ln2, seg, worw, woc, wo, wr_t, br)


def _route_kernel(lg_ref, e_out, w_out, pos_out, cnt_out):
    lg = lg_ref[...]
    tr = lg.shape[1]
    erow = lax.broadcasted_iota(jnp.int32, lg.shape, 0)
    work = lg
    hits, vals, idxs = [], [], []
    for _ in range(TOPK):
        m = jnp.max(work, axis=0, keepdims=True)
        idx = jnp.min(jnp.where(work == m, erow, NE), axis=0, keepdims=True)
        hit = erow == idx
        hits.append(hit)
        vals.append(m)
        idxs.append(idx)
        work = jnp.where(hit, -jnp.inf, work)
    ex = [jnp.exp(vk - vals[0]) for vk in vals]
    den = ex[0] + ex[1] + ex[2] + ex[3]
    multi = jnp.where(hits[0] | hits[1] | hits[2] | hits[3], 1.0, 0.0)
    ti = lax.broadcasted_iota(jnp.int32, (tr, tr), 0)
    tj = lax.broadcasted_iota(jnp.int32, (tr, tr), 1)
    before = jnp.where(ti < tj, 1.0, 0.0).astype(bf16)
    excl = jnp.dot(multi.astype(bf16), before, preferred_element_type=f32)
    total = jnp.sum(multi, axis=1, keepdims=True)
    units = jnp.floor((total + (ALIGN - 1)) * (1.0 / ALIGN))
    ei = lax.broadcasted_iota(jnp.int32, (NE, NE), 0)
    ej = lax.broadcasted_iota(jnp.int32, (NE, NE), 1)
    below = jnp.where(ej < ei, 1.0, 0.0).astype(bf16)
    off = ALIGN * jnp.dot(below, jnp.broadcast_to(units, (NE, 128)).astype(bf16),
                          preferred_element_type=f32)[:, 0:1]
    for kk in range(TOPK):
        e_out[kk:kk + 1, :] = idxs[kk]
        w_out[kk:kk + 1, :] = ex[kk] / den
        pos_out[kk:kk + 1, :] = jnp.sum(jnp.where(hits[kk], excl + off, 0.0), axis=0,
                                        keepdims=True).astype(jnp.int32)
    cnt_out[...] = jnp.broadcast_to(total, cnt_out.shape).astype(jnp.int32)


def _route(logits_t):
    t = logits_t.shape[1]
    sel = pl.BlockSpec((TOPK, TT), lambda i: (0, i))
    return pl.pallas_call(
        _route_kernel,
        name="route",
        grid=(t // TT,),
        in_specs=[pl.BlockSpec((NE, TT), lambda i: (0, i))],
        out_specs=[sel, sel, sel, pl.BlockSpec((None, NE, 128), lambda i: (i, 0, 0))],
        out_shape=[jax.ShapeDtypeStruct((TOPK, t), jnp.int32),
                   jax.ShapeDtypeStruct((TOPK, t), f32),
                   jax.ShapeDtypeStruct((TOPK, t), jnp.int32),
                   jax.ShapeDtypeStruct((t // TT, NE, 128), jnp.int32)],
        compiler_params=_cparams(("parallel",)),
    )(logits_t)


def _run_copies(tile, cnt_ref, make_copy, act):
    def per_expert(e, carry):
        cnt = pl.multiple_of(cnt_ref[tile * NE + e], ALIGN)

        @pl.when(cnt > 0)
        def _():
            act(make_copy(e, cnt))
        return carry
    lax.fori_loop(0, NE, per_expert, 0)


def _dispatch_kernel(cnt_ref, run_ref, off_ref, tot_ref, tail_ref, pos_ref, x2_ref, xs_ref,
                     sbuf, zbuf, sem, zsem):
    i = pl.program_id(0)
    last = pl.num_programs(0) - 1
    cur = i % 2

    @pl.when(i == 0)
    def _():
        zbuf[...] = jnp.zeros_like(zbuf)

        def zero_copy(start):
            return pltpu.make_async_copy(zbuf, xs_ref.at[pl.ds(pl.multiple_of(start, EBLK), EBLK)], zsem)

        for e in range(NE):
            @pl.when(tail_ref[e] >= 0)
            def _():
                zero_copy(tail_ref[e]).start()
        for e in range(NE):
            @pl.when(tail_ref[e] >= 0)
            def _():
                zero_copy(tail_ref[e]).wait()
        n_blk = xs_ref.shape[0] // EBLK
        lax.fori_loop(tail_ref[NE], n_blk, lambda blk, c: (zero_copy(blk * EBLK).start(), c)[1], 0)
        lax.fori_loop(tail_ref[NE], n_blk, lambda blk, c: (zero_copy(blk * EBLK).wait(), c)[1], 0)

    def slab(tile, half):
        def make(e, rows):
            src = pl.multiple_of(off_ref[tile * NE + e], ALIGN)
            dst = pl.multiple_of(run_ref[tile * NE + e], ALIGN)
            return pltpu.make_async_copy(sbuf.at[half, pl.ds(src, rows)],
                                         xs_ref.at[pl.ds(dst, rows)], sem.at[half])
        return make

    def drain(tile, half):
        n = pl.multiple_of(tot_ref[tile], ALIGN)
        pltpu.make_async_copy(sbuf.at[half, pl.ds(0, n)], xs_ref.at[pl.ds(0, n)], sem.at[half]).wait()

    @pl.when(i >= 2)
    def _():
        drain(i - 2, cur)

    pos = pos_ref[...]
    xt = x2_ref[...]
    for c in range(0, SB, SCHUNK):
        slot = lax.broadcasted_iota(jnp.int32, (SCHUNK, TT), 0) + c
        place = jnp.zeros((SCHUNK, TT), f32)
        for kk in range(TOPK):
            place = jnp.where(slot == pos[kk:kk + 1, :], 1.0, place)
        sbuf[cur, c:c + SCHUNK] = _pack_pairs(jnp.dot(place.astype(bf16), xt,
                                                      preferred_element_type=f32))
    _run_copies(i, cnt_ref, slab(i, cur), lambda cp: cp.start())

    @pl.when(i == last)
    def _():
        @pl.when(i >= 1)
        def _():
            drain(i - 1, 1 - cur)
        drain(i, cur)


def _dispatch(cnt16, run_start, off16, tot, tail, pos, x2, n_rows):
    t = x2.shape[0]
    return pl.pallas_call(
        _dispatch_kernel,
        name="dispatch",
        grid_spec=pltpu.PrefetchScalarGridSpec(
            num_scalar_prefetch=5, grid=(t // TT,),
            in_specs=[pl.BlockSpec((TOPK, TT), lambda i, *_: (0, i)),
                      pl.BlockSpec((TT, D), lambda i, *_: (i, 0))],
            out_specs=pl.BlockSpec(memory_space=pl.ANY),
            scratch_shapes=[pltpu.VMEM((2, SB, DH), jnp.uint32), pltpu.VMEM((EBLK, DH), jnp.uint32),
                            pltpu.SemaphoreType.DMA((2,)), pltpu.SemaphoreType.DMA(())]),
        out_shape=jax.ShapeDtypeStruct((n_rows, DH), jnp.uint32),
        compiler_params=_cparams(("arbitrary",)),
    )(cnt16, run_start, off16, tot, tail, pos, x2)


def _expert_kernel(be_ref, nv_ref, fx_ref, xs_ref, wg_ref, bg_ref, wu_ref, bu_ref, wd_ref, bd_ref,
                   ys_ref, wgb, wub, wdb):
    del be_ref
    i = pl.program_id(0)
    nv = nv_ref[i]

    @pl.when(fx_ref[i] == 1)
    def _():
        wgb[...] = wg_ref[...].astype(bf16)
        wub[...] = wu_ref[...].astype(bf16)
        wdb[...] = wd_ref[...].astype(bf16)

    def swiglu(r0, n):
        xlo, xhi = _unpack_pairs(xs_ref[r0:r0 + n, :])
        dot = functools.partial(jnp.dot, preferred_element_type=f32)
        gt = jnp.minimum(dot(xlo, wgb[:DH, :]) + dot(xhi, wgb[DH:, :]) + bg_ref[...], LIMIT)
        up = jnp.clip(dot(xlo, wub[:DH, :]) + dot(xhi, wub[DH:, :]) + bu_ref[...], -LIMIT, LIMIT)
        act = (up + 1.0) * (gt * jax.nn.sigmoid(gt * ALPHA))
        y = dot(act.astype(bf16), wdb[...]) + bd_ref[...]
        ys_ref[r0:r0 + n, :] = _pack_pairs(y.astype(bf16).astype(f32))

    def skip(r0, n):
        ys_ref[r0:r0 + n, :] = jnp.zeros((n, DH), jnp.uint32)

    for h0 in range(0, EBLK, EHALF):
        @pl.when(nv >= h0 + EHALF)
        def _():
            swiglu(h0, EHALF)

        @pl.when(nv < h0 + EHALF)
        def _():
            for q0 in range(h0, h0 + EHALF, ESUB):
                @pl.when(nv > q0)
                def _():
                    swiglu(q0, ESUB)

                @pl.when(nv <= q0)
                def _():
                    skip(q0, ESUB)


def _experts(blk_e, nvalid, first, xs, wg, bg, wu, bu, wd, bd):
    p = xs.shape[0]
    wspec = pl.BlockSpec((None, D, D), lambda i, be, nv, fx: (be[i], 0, 0))
    bspec = pl.BlockSpec((None, 1, D), lambda i, be, nv, fx: (be[i], 0, 0))
    xrows = pl.BlockSpec((EBLK, DH), lambda i, be, nv, fx: (jnp.where(nv[i] > 0, i, 0), 0))
    yrows = pl.BlockSpec((EBLK, DH), lambda i, be, nv, fx: (i, 0))
    return pl.pallas_call(
        _expert_kernel,
        name="experts",
        grid_spec=pltpu.PrefetchScalarGridSpec(
            num_scalar_prefetch=3, grid=(p // EBLK,),
            in_specs=[xrows, wspec, bspec, wspec, bspec, wspec, bspec],
            out_specs=yrows,
            scratch_shapes=[pltpu.VMEM((D, D), bf16)] * 3),
        out_shape=jax.ShapeDtypeStruct((p, DH), jnp.uint32),
        compiler_params=_cparams(("arbitrary",)),
    )(blk_e, nvalid, first, xs, wg, bg, wu, bu, wd, bd)


def _combine_kernel(cnt_ref, run_ref, off_ref, tot_ref, ys_ref, pos_ref, w_ref, h_ref, g_ref, o_ref,
                    rbuf, sem):
    i = pl.program_id(0)
    cur = i % 2

    def fetch(tile, half):
        def slab(e, rows):
            src = pl.multiple_of(run_ref[tile * NE + e], ALIGN)
            dst = pl.multiple_of(off_ref[tile * NE + e], ALIGN)
            return pltpu.make_async_copy(ys_ref.at[pl.ds(src, rows)],
                                         rbuf.at[half, pl.ds(dst, rows)], sem.at[half])
        return slab

    @pl.when(i == 0)
    def _():
        rbuf[...] = jnp.zeros_like(rbuf)
        _run_copies(0, cnt_ref, fetch(0, 0), lambda cp: cp.start())

    @pl.when(i + 1 < pl.num_programs(0))
    def _():
        _run_copies(i + 1, cnt_ref, fetch(i + 1, 1 - cur), lambda cp: cp.start())

    n = pl.multiple_of(tot_ref[i], ALIGN)
    pltpu.make_async_copy(ys_ref.at[pl.ds(0, n)], rbuf.at[cur, pl.ds(0, n)], sem.at[cur]).wait()

    pos = pos_ref[...]
    w = w_ref[...]
    moe_lo = jnp.zeros((TT, DH), f32)
    moe_hi = jnp.zeros((TT, DH), f32)
    for c in range(0, SB, SCHUNK):
        slot = lax.broadcasted_iota(jnp.int32, (TT, SCHUNK), 1) + c
        gate = jnp.zeros((TT, SCHUNK), f32)
        for kk in range(TOPK):
            gate = jnp.where(slot == pos[:, kk:kk + 1], w[:, kk:kk + 1], gate)
        gb = gate.astype(bf16)
        lo, hi = _unpack_pairs(rbuf[cur, c:c + SCHUNK])
        moe_lo = moe_lo + jnp.dot(gb, lo, preferred_element_type=f32)
        moe_hi = moe_hi + jnp.dot(gb, hi, preferred_element_type=f32)
    acc = h_ref[...] + jnp.concatenate([moe_lo, moe_hi], axis=1)
    o_ref[...] = acc * lax.rsqrt(jnp.mean(acc * acc, axis=-1, keepdims=True) + NORM_EPS) * g_ref[...]


def _combine(cnt16, run_start, off16, tot, ys, pos_t, gates_t, hres, lnf):
    t = hres.shape[0]
    return pl.pallas_call(
        _combine_kernel,
        name="combine",
        grid_spec=pltpu.PrefetchScalarGridSpec(
            num_scalar_prefetch=4, grid=(t // TT,),
            in_specs=[pl.BlockSpec(memory_space=pl.ANY),
                      pl.BlockSpec((TT, TOPK), lambda i, *_: (i, 0)),
                      pl.BlockSpec((TT, TOPK), lambda i, *_: (i, 0)),
                      pl.BlockSpec((TT, D), lambda i, *_: (i, 0)),
                      pl.BlockSpec((1, D), lambda i, *_: (0, 0))],
            out_specs=pl.BlockSpec((TT, D), lambda i, *_: (i, 0)),
            scratch_shapes=[pltpu.VMEM((2, SB, DH), jnp.uint32), pltpu.SemaphoreType.DMA((2,))]),
        out_shape=jax.ShapeDtypeStruct((t, D), f32),
        compiler_params=_cparams(("arbitrary",)),
    )(cnt16, run_start, off16, tot, ys, pos_t, gates_t, hres, lnf)


def _row_tile(n, want):
    t = min(n, want)
    assert n % t == 0
    return t


def kernel(x, meta_tokens, ln1_g, w_in, mu_r, mu_k, mu_v, mu_w, mu_a, mu_g, w0, w_w1, w_w2, a0, w_a1, w_a2, w_g1, w_g2, k_k, k_a, r_k, lnx_g, lnx_b, w_o_rwkv, conv_w, w_o_conv, w_o, ln2_g, w_router, b_router, w_e_gate, b_e_gate, w_e_up, b_e_up, w_e_down, b_e_down, lnf_g):
    nb, seq, _ = x.shape
    t = nb * seq
    assert ln1_g.shape[0] == 1, "single layer"

    muw, mua, mug = mu_w[0][:, None], mu_a[0][:, None], mu_g[0][:, None]
    lora_cur = jnp.concatenate([(1 - muw) * w_w1[0], (1 - mua) * w_a1[0], (1 - mug) * w_g1[0]], axis=1)
    lora_prev = jnp.concatenate([muw * w_w1[0], mua * w_a1[0], mug * w_g1[0]], axis=1)
    zrow = jnp.zeros((1, R), f32)
    p = {
        "ln1": ln1_g[0][None, :],
        "w_in": jnp.concatenate([w_in[0][:, :6 * R], lora_cur, lora_prev, w_in[0][:, 6 * R:]],
                                axis=1).astype(bf16),
        "vec_mix": jnp.concatenate([mu_r, mu_k, mu_v, w0, a0, k_k, k_a, zrow], axis=0),
        "ww2": w_w2[0].astype(bf16), "wa2": w_a2[0].astype(bf16), "wg2": w_g2[0].astype(bf16),
        "cw": jnp.concatenate([conv_w[0], jnp.zeros((5, R), f32)], axis=0),
        "seg": (jnp.arange(R)[:, None] // HS == jnp.arange(R)[None, :] // HS).astype(bf16),
    }
    vec_merge = jnp.concatenate([lnx_g, lnx_b, r_k, jnp.zeros((5, R), f32)], axis=0)

    meta_out = _projmix(meta_tokens.astype(f32), NMETA, jnp.zeros((MARGIN, PW), f32), p, NMETA)
    pad = lambda z: jnp.pad(z, ((CHUNK - NMETA, 0), (0, 0)))[None]
    r_m, k_m, v_m, kk_m, b_m, lw_m = (pad(z) for z in meta_out[1:7])
    _, s_meta = _wkv(r_m, k_m, v_m, kk_m, b_m, lw_m, jnp.zeros((NH // 2, 2 * HS, 2 * HS), f32), 1)

    x2d = x.reshape(t, D)
    gt, r, k, v, kk, b, lw, g, cb, _ = _projmix(x2d, seq, meta_out[9], p, _row_tile(seq, 512))
    as3 = lambda z: z.reshape(nb, seq, R)
    y, _ = _wkv(as3(r), as3(k), as3(v), as3(kk), as3(b), as3(lw), s_meta[0], WKV_SEQS)
    hres, x2, logits_t = _merge(
        y.reshape(t, R), r, k, v, g, cb, gt, x2d, vec_merge, ln2_g[0][None, :], p["seg"],
        w_o_rwkv[0].astype(bf16), w_o_conv[0].astype(bf16), w_o[0].astype(bf16),
        w_router[0].T, b_router[0][:, None], _row_tile(t, 512))

    assert t % TT == 0
    nt = t // TT
    _, gates, pos, cnt = _route(logits_t)
    cnt = cnt[:, :, 0]
    cnt16 = ((cnt + ALIGN - 1) // ALIGN) * ALIGN
    base16 = jnp.cumsum(cnt16, axis=0) - cnt16
    tot16 = jnp.sum(cnt16, axis=0)
    padded = ((tot16 + EBLK - 1) // EBLK) * EBLK
    pend = jnp.cumsum(padded)
    pstart = pend - padded
    run_start = pstart[None, :] + base16
    off16 = jnp.cumsum(cnt16, axis=1) - cnt16
    n_rows = -(-(t * TOPK + nt * NE * (ALIGN - 1) + NE * EBLK) // EBLK) * EBLK
    n_blk = n_rows // EBLK
    blk_start = jnp.arange(n_blk, dtype=jnp.int32) * EBLK
    blk_e = jnp.minimum(jnp.sum(pend[None, :] <= blk_start[:, None], axis=1), NE - 1).astype(jnp.int32)
    n_used = (pend[NE - 1:] // EBLK).astype(jnp.int32)
    tail = jnp.concatenate([jnp.where(padded > 0, pend - EBLK, -1), n_used]).astype(jnp.int32)
    flat = lambda z: z.reshape(-1).astype(jnp.int32)
    tot = jnp.sum(cnt16, axis=1).astype(jnp.int32)
    xs = _dispatch(flat(cnt16), flat(run_start), flat(off16), tot, tail, pos, x2, n_rows)
    first = jnp.concatenate([jnp.ones((1,), jnp.int32), (blk_e[1:] != blk_e[:-1]).astype(jnp.int32)])
    nvalid = jnp.clip(tot16[blk_e] - (blk_start - pstart[blk_e]), 0, EBLK)
    nvalid = jnp.where(blk_start < pend[NE - 1], nvalid, 0).astype(jnp.int32)
    ys = _experts(blk_e, nvalid, first, xs,
                  w_e_gate[0], b_e_gate[0][:, None, :],
                  w_e_up[0], b_e_up[0][:, None, :],
                  w_e_down[0], b_e_down[0][:, None, :])
    out = _combine(flat(cnt16), flat(run_start), flat(off16), tot, ys, pos.T, gates.T, hres, lnf_g[None, :])
    return out.reshape(nb, seq, D)
```

```python
import functools

import jax
import jax.numpy as jnp
from jax import lax
from jax.experimental import pallas as pl
from jax.experimental.pallas import tpu as pltpu

D = 1024
R = 512
NH = 8
HS = 64
NE = 32
TOPK = 4
NMETA = 16
CHUNK = 64
WKV_SEQS = 4
EBLK = 512
TT = 512
ALIGN = 8
SB = 4 * TT + 256
DH = D // 2
MERGE_SUB = 256
SCHUNK = 768
NORM_EPS = 1e-5
LNX_EPS = 64e-5
ALPHA = 1.702
LIMIT = 7.0
DECAY_SCALE = 0.6065306597126334
LORA_W, LORA_A, LORA_G = 64, 64, 128
LORA = LORA_W + LORA_A + LORA_G
NPROJ = 3 * R + 3 * R + 2 * D + 2 * LORA
VMEM_LIMIT = 56 * 1024 * 1024

f32 = jnp.float32
bf16 = jnp.bfloat16


def _bdot(a, b):
    return jnp.dot(a.astype(bf16), b.astype(bf16), preferred_element_type=f32)


def _bdot_nt(a, b):
    return lax.dot_general(a.astype(bf16), b.astype(bf16), (((1,), (1,)), ((), ())),
                           preferred_element_type=f32)


def _bdot_tn(a, b):
    return lax.dot_general(a.astype(bf16), b.astype(bf16), (((0,), (0,)), ((), ())),
                           preferred_element_type=f32)


def _split3(x):
    h = x.astype(bf16)
    r1 = x - h.astype(f32)
    m = r1.astype(bf16)
    l = (r1 - m.astype(f32)).astype(bf16)
    return h, m, l


def _head_sum(x, seg):
    return jnp.dot(x.astype(bf16), seg, preferred_element_type=f32)


def _pack_pairs(x):
    a = lax.bitcast_convert_type(x[:, :DH], jnp.uint32)
    b = lax.bitcast_convert_type(x[:, DH:], jnp.uint32)
    return (a & jnp.uint32(0xFFFF0000)) | (b >> 16)


def _unpack_pairs(p):
    lo = lax.bitcast_convert_type(p & jnp.uint32(0xFFFF0000), f32)
    hi = lax.bitcast_convert_type(p << 16, f32)
    return lo.astype(bf16), hi.astype(bf16)


def _cparams(sem):
    return pltpu.CompilerParams(dimension_semantics=sem, vmem_limit_bytes=VMEM_LIMIT)


MARGIN = 16
PW = 3 * R + 3 * R + 2 * LORA


def _mix_rows(pbuf, a, n, vec, cw, seg, ww2, wa2, wg2, outs, o):
    r_out, k_out, v_out, kk_out, b_out, lw_out, g_out, cb_out = outs
    mu_r, mu_k, mu_v = vec[0:1, :], vec[1:2, :], vec[2:3, :]
    w0, a0, k_k, k_a = vec[3:4, :], vec[4:5, :], vec[5:6, :], vec[6:7, :]
    cur = pbuf[a:a + n, :]
    prev = pbuf[a - 1:a - 1 + n, :]
    r = cur[:, :R] + (prev[:, :R] - cur[:, :R]) * mu_r
    k = cur[:, R:2 * R] + (prev[:, R:2 * R] - cur[:, R:2 * R]) * mu_k
    v = cur[:, 2 * R:3 * R] + (prev[:, 2 * R:3 * R] - cur[:, 2 * R:3 * R]) * mu_v
    lo = 6 * R
    mixed = cur[:, lo:lo + LORA] + prev[:, lo + LORA:lo + 2 * LORA]
    hw = jnp.tanh(mixed[:, :LORA_W])
    ha = mixed[:, LORA_W:LORA_W + LORA_A]
    hg = jax.nn.sigmoid(mixed[:, LORA_W + LORA_A:])
    lw = -DECAY_SCALE * jax.nn.sigmoid(w0 + _bdot(hw, ww2))
    aa = jax.nn.sigmoid(a0 + _bdot(ha, wa2))
    g = _bdot(hg, wg2)
    kk = k * k_k
    norm = jnp.sqrt(_head_sum(kk * kk, seg))
    kk = kk / jnp.maximum(norm, 1e-12)
    k = k * (1.0 + (aa - 1.0) * k_a)
    rows = slice(o, o + n)
    r_out[rows, :] = r.astype(r_out.dtype)
    k_out[rows, :] = k.astype(k_out.dtype)
    v_out[rows, :] = v.astype(v_out.dtype)
    kk_out[rows, :] = kk.astype(kk_out.dtype)
    b_out[rows, :] = (kk * aa).astype(b_out.dtype)
    lw_out[rows, :] = lw
    g_out[rows, :] = g.astype(g_out.dtype)
    prev2 = pbuf[a - 2:a - 2 + n, 4 * R:6 * R]
    u0 = cur[:, 4 * R:5 * R] * cur[:, 5 * R:6 * R]
    u1 = prev[:, 4 * R:5 * R] * prev[:, 5 * R:6 * R]
    u2 = prev2[:, :R] * prev2[:, R:]
    conv = cw[2:3, :] * u0 + cw[1:2, :] * u1 + cw[0:1, :] * u2
    cb_out[rows, :] = (cur[:, 3 * R:4 * R] * conv).astype(cb_out.dtype)


def _projmix_kernel(tiles_per_seq, group, x_ref, g_ref, w_ref, init_ref, vec_ref, ww2_ref, wa2_ref,
                    wg2_ref, cw_ref, seg_ref, gt_out, r_out, k_out, v_out, kk_out, b_out, lw_out,
                    g_out, cb_out, tail_out, pbuf_a, pbuf_b):
    i = pl.program_id(0)
    tm = x_ref.shape[0]

    @pl.when(i == 0)
    def _():
        pbuf_a[...] = jnp.zeros_like(pbuf_a)
        pbuf_b[...] = jnp.zeros_like(pbuf_b)

    def step(pcur, pprv):
        first = (i % tiles_per_seq) == 0
        pcur[0:MARGIN, :] = jnp.where(first, init_ref[...], pprv[tm:tm + MARGIN, :])
        tail_out[...] = pprv[tm:tm + MARGIN, :]

        x = x_ref[...]
        xb = (x * lax.rsqrt(jnp.mean(x * x, axis=-1, keepdims=True) + NORM_EPS)
              * g_ref[...]).astype(bf16)
        outs = (r_out, k_out, v_out, kk_out, b_out, lw_out, g_out, cb_out)
        vec, cw, seg = vec_ref[...], cw_ref[...], seg_ref[...]
        ww2, wa2, wg2 = ww2_ref[...], wa2_ref[...], wg2_ref[...]

        def project(c):
            res = jnp.dot(xb, w_ref[:, c:c + 512], preferred_element_type=f32)
            if c < PW:
                pcur[MARGIN:MARGIN + tm, c:c + 512] = res
            else:
                gt_out[:, c - PW:c - PW + 512] = res.astype(gt_out.dtype)
            return res[0:8, :] * 0.0

        cols = list(range(0, NPROJ, 512))
        groups = list(range(0, tm, group))
        per = -(-len(cols) // len(groups))
        tie = jnp.zeros((8, R), f32)
        for n, o in enumerate(groups):
            for c in cols[n * per:(n + 1) * per]:
                tie = project(c)
            _mix_rows(pprv, MARGIN + o, group, vec + tie, cw + tie, seg, ww2, wa2, wg2, outs, o)

    @pl.when(i % 2 == 0)
    def _():
        step(pbuf_a, pbuf_b)

    @pl.when(i % 2 == 1)
    def _():
        step(pbuf_b, pbuf_a)


def _projmix(x2d, seq_len, init, p, tm):
    t = x2d.shape[0]
    n = t // tm
    group = min(tm, 64)

    def whole(shape):
        return pl.BlockSpec(shape, lambda i: (0,) * len(shape))

    lag = pl.BlockSpec((tm, R), lambda i: (jnp.maximum(i - 1, 0), 0))
    return pl.pallas_call(
        functools.partial(_projmix_kernel, seq_len // tm, group),
        name="projmix",
        grid=(n + 1,),
        in_specs=[pl.BlockSpec((tm, D), lambda i: (jnp.minimum(i, n - 1), 0)),
                  whole((1, D)),
                  pl.BlockSpec((D, NPROJ), lambda i: (0, 0), pipeline_mode=pl.Buffered(1)),
                  whole((MARGIN, PW)), whole((8, R)), whole((LORA_W, R)), whole((LORA_A, R)),
                  whole((LORA_G, R)), whole((8, R)), whole((R, R))],
        out_specs=[pl.BlockSpec((tm, 2 * D), lambda i: (jnp.minimum(i, n - 1), 0))] + [lag] * 8
                  + [whole((MARGIN, PW))],
        out_shape=[jax.ShapeDtypeStruct((t, 2 * D), bf16)]
                  + [jax.ShapeDtypeStruct((t, R), f32 if m == 5 else bf16) for m in range(8)]
                  + [jax.ShapeDtypeStruct((MARGIN, PW), f32)],
        scratch_shapes=[pltpu.VMEM((MARGIN + tm, PW), f32)] * 2,
        compiler_params=_cparams(("arbitrary",)),
    )(x2d, p["ln1"], p["w_in"], init, p["vec_mix"], p["ww2"], p["wa2"], p["wg2"], p["cw"], p["seg"])


def _wkv_kernel(r_ref, k_ref, v_ref, kk_ref, b_ref, lw_ref, s0_ref, y_ref, sT_ref, s_scr):
    c = pl.program_id(1)
    nbw = r_ref.shape[0]

    @pl.when(c == 0)
    def _():
        for j in range(nbw):
            s_scr[j] = s0_ref[...]

    C = CHUNK
    W2 = 2 * HS
    ri = lax.broadcasted_iota(jnp.int32, (C, C), 0)
    ci = lax.broadcasted_iota(jnp.int32, (C, C), 1)
    tri = jnp.where(ri >= ci, 1.0, 0.0).astype(bf16)
    rp = lax.broadcasted_iota(jnp.int32, (C, W2), 0)
    cp = lax.broadcasted_iota(jnp.int32, (C, W2), 1)
    lo = cp < HS
    lo1 = lax.broadcasted_iota(jnp.int32, (1, W2), 1) < HS
    sp = jnp.where(lo, cp, cp - HS)
    strict = rp > sp
    incl = rp >= sp
    eye = jnp.where(rp == sp, 1.0, 0.0).astype(f32)
    rq = lax.broadcasted_iota(jnp.int32, (W2, W2), 0)
    cq = lax.broadcasted_iota(jnp.int32, (W2, W2), 1)
    diag_blocks = (rq < HS) == (cq < HS)
    dot = functools.partial(jnp.dot, preferred_element_type=f32)

    def first(z):
        return jnp.where(lo1, z, jnp.zeros_like(z))

    def second(z):
        return jnp.where(lo1, jnp.zeros_like(z), z)

    def bdiag(z):
        return jnp.concatenate([first(z), second(z)], axis=0)

    def adiag(z):
        return jnp.concatenate([second(z), first(z)], axis=0)

    at, rt, bt, kt, bh, kh, vv, g_tot = [], [], [], [], [], [], [], []
    for j in range(nbw):
        lw = lw_ref[j]
        h, m, l = _split3(lw)
        cum = dot(tri, h) + dot(tri, m) + dot(tri, l)
        tot = cum[C - 1:C, :]
        k = k_ref[j].astype(f32)
        b = b_ref[j].astype(f32)
        g_inv = jnp.exp(-cum)
        tail = jnp.exp(tot - cum)
        rt.append(r_ref[j].astype(f32) * jnp.exp(cum))
        kt.append(k * g_inv)
        bt.append(b * g_inv)
        at.append(-kk_ref[j].astype(f32) * jnp.exp(cum - lw))
        bh.append(b * tail)
        kh.append(k * tail)
        vv.append(v_ref[j].astype(f32))
        g_tot.append(jnp.exp(tot))

    chains = [(j, p) for j in range(nbw) for p in range(NH // 2)]
    cs = range(len(chains))
    sl = lambda p: slice(p * W2, (p + 1) * W2)
    s_old = [s_scr[j, p] for j, p in chains]
    lhs = [jnp.concatenate([at[j][:, sl(p)], rt[j][:, sl(p)]], axis=0) for j, p in chains]
    rhs_n = [jnp.concatenate([bt[j][:, sl(p)], kt[j][:, sl(p)]], axis=0).astype(bf16) for j, p in chains]
    rhs_s = [jnp.concatenate([kt[j][:, sl(p)], bt[j][:, sl(p)]], axis=0).astype(bf16) for j, p in chains]
    lhs_b = [x.astype(bf16) for x in lhs]
    am1 = [_bdot_nt(first(lhs[n]), rhs_n[n]) for n in cs]
    am2 = [_bdot_nt(second(lhs[n]), rhs_s[n]) for n in cs]
    a_ab = [jnp.where(strict, jnp.where(lo, am1[n][:C], am2[n][:C]), 0.0) for n in cs]
    a_ak = [jnp.where(strict, jnp.where(lo, am2[n][:C], am1[n][:C]), 0.0) for n in cs]
    a_r1 = [jnp.where(incl, am1[n][C:], 0.0) for n in cs]
    a_r2 = [jnp.where(incl, am2[n][C:], 0.0) for n in cs]
    pb = [a_ab[n].astype(bf16) for n in cs]
    pw = [dot(pb[n], bdiag(pb[n])) for n in cs]
    x = [eye + a_ab[n] for n in cs]
    for _ in range(4):
        pb = [pw[n].astype(bf16) for n in cs]
        both = [dot(jnp.concatenate([pb[n], x[n].astype(bf16)], axis=0), bdiag(pb[n])) for n in cs]
        pw = [both[n][:C] for n in cs]
        x = [x[n] + both[n][C:] for n in cs]
    x = [x[n] + dot(x[n].astype(bf16), bdiag(pw[n].astype(bf16))) for n in cs]
    vp = [vv[j][:, sl(p)] for j, p in chains]
    sh = [_bdot_nt(lhs_b[n], s_old[n]) for n in cs]
    av = [dot(a_ak[n].astype(bf16), adiag(vp[n].astype(bf16))) for n in cs]
    u = [dot(x[n].astype(bf16), bdiag((sh[n][:C] + av[n]).astype(bf16))) for n in cs]
    ub = [u[n].astype(bf16) for n in cs]
    vb = [vp[n].astype(bf16) for n in cs]
    mix4 = [jnp.concatenate([first(ub[n]), first(vb[n]), second(vb[n]), second(ub[n])], axis=0)
            for n in cs]
    ys = [sh[n][C:] + dot(jnp.concatenate([a_r1[n], a_r2[n]], axis=1).astype(bf16), mix4[n])
          for n in cs]
    uv = [jnp.concatenate([ub[n], vb[n]], axis=0) for n in cs]
    bk = [jnp.concatenate([bh[j][:, sl(p)], kh[j][:, sl(p)]], axis=0) for j, p in chains]
    s_new = [s_old[n] * g_tot[j][:, sl(p)] + jnp.where(diag_blocks, _bdot_tn(uv[n], bk[n]), 0.0)
             for n, (j, p) in enumerate(chains)]
    for n, (j, p) in enumerate(chains):
        s_scr[j, p] = s_new[n]
        y_ref[j, :, sl(p)] = ys[n]

    @pl.when(c == pl.num_programs(1) - 1)
    def _():
        sT_ref[...] = s_scr[...]


def _wkv(r, k, v, kk, b, lw, s0, nbw):
    nb, seq, _ = r.shape
    assert nb % nbw == 0 and seq % CHUNK == 0
    blk = pl.BlockSpec((nbw, CHUNK, R), lambda i, c: (i, c, 0))
    st = (NH // 2, 2 * HS, 2 * HS)
    return pl.pallas_call(
        _wkv_kernel,
        name="wkv",
        grid=(nb // nbw, seq // CHUNK),
        in_specs=[blk] * 6 + [pl.BlockSpec(st, lambda i, c: (0, 0, 0))],
        out_specs=[blk, pl.BlockSpec((nbw,) + st, lambda i, c: (i, 0, 0, 0))],
        out_shape=[jax.ShapeDtypeStruct((nb, seq, R), f32),
                   jax.ShapeDtypeStruct((nb,) + st, f32)],
        scratch_shapes=[pltpu.VMEM((nbw,) + st, f32)],
        compiler_params=_cparams(("parallel", "arbitrary")),
    )(r, k, v, kk, b, lw, s0)


def _merge_kernel(y_ref, r_ref, k_ref, v_ref, g_ref, cb_ref, gt_ref, x_ref, vec_ref, ln2_ref,
                  seg_ref, worw_ref, woc_ref, wo_ref, wr_ref, br_ref,
                  h_out, x2_out, lg_out):
    seg = seg_ref[...]
    vec = vec_ref[...]
    lnx_g, lnx_b, r_k = vec[0:1, :], vec[1:2, :], vec[2:3, :]
    tm = y_ref.shape[0]
    rows = [slice(s0, s0 + MERGE_SUB) for s0 in range(0, tm, MERGE_SUB)]
    gs = range(len(rows))
    y = [y_ref[sl, :] for sl in rows]
    mean = [_head_sum(y[n], seg) * (1.0 / HS) for n in gs]
    yc = [y[n] - mean[n] for n in gs]
    var = [_head_sum(yc[n] * yc[n], seg) * (1.0 / HS) for n in gs]
    yn = [yc[n] * lax.rsqrt(var[n] + LNX_EPS) * lnx_g + lnx_b for n in gs]
    rk = [_head_sum(r_ref[sl, :].astype(f32) * k_ref[sl, :].astype(f32) * r_k, seg) for sl in rows]
    z = [(yn[n] + rk[n] * v_ref[rows[n], :].astype(f32)) * g_ref[rows[n], :].astype(f32) for n in gs]
    y_a = [_bdot(z[n], worw_ref[...]) for n in gs]
    y_b = [_bdot(cb_ref[sl, :], woc_ref[...]) for sl in rows]
    merged = [jax.nn.sigmoid(gt_ref[rows[n], :D].astype(f32)) * y_a[n]
              + jax.nn.sigmoid(gt_ref[rows[n], D:].astype(f32)) * y_b[n] for n in gs]
    hres = [x_ref[rows[n], :] + _bdot(merged[n], wo_ref[...]) for n in gs]
    x2 = [hres[n] * lax.rsqrt(jnp.mean(hres[n] * hres[n], axis=-1, keepdims=True) + NORM_EPS)
          * ln2_ref[...] for n in gs]
    wh, wm, _ = _split3(wr_ref[...])
    nt = functools.partial(lax.dot_general, dimension_numbers=(((1,), (1,)), ((), ())),
                           preferred_element_type=f32)
    for n, sl in enumerate(rows):
        h_out[sl, :] = hres[n]
        x2_out[sl, :] = x2[n].astype(x2_out.dtype)
        xh, xm, _ = _split3(x2[n])
        lg_out[:, sl] = nt(wh, xh) + nt(wh, xm) + nt(wm, xh) + br_ref[...]


def _merge(y, r, k, v, g, cb, gt, x2d, vec, ln2, seg, worw, woc, wo, wr_t, br, tm):
    t = y.shape[0]

    def rows(n):
        return pl.BlockSpec((tm, n), lambda i: (i, 0))

    def whole(shape):
        return pl.BlockSpec(shape, lambda i: (0,) * len(shape))

    return pl.pallas_call(
        _merge_kernel,
        name="merge",
        grid=(t // tm,),
        in_specs=[rows(R)] * 6 + [rows(2 * D), rows(D), whole((8, R)), whole((1, D)),
                                   whole((R, R)), whole((R, D)), whole((R, D)), whole((D, D)),
                                   whole((NE, D)), whole((NE, 1))],
        out_specs=[rows(D), rows(D), pl.BlockSpec((NE, tm), lambda i: (0, i))],
        out_shape=[jax.ShapeDtypeStruct((t, D), f32), jax.ShapeDtypeStruct((t, D), bf16),
                   jax.ShapeDtypeStruct((NE, t), f32)],
        compiler_params=_cparams(("parallel",)),
    )(y, r, k, v, g, cb, gt, x2d, vec, ln2, seg, worw, woc, wo, wr_t, br)


def _route_kernel(lg_ref, e_out, w_out, pos_out, cnt_out):
    lg = lg_ref[...]
    tr = lg.shape[1]
    erow = lax.broadcasted_iota(jnp.int32, lg.shape, 0)
    work = lg
    hits, vals, idxs = [], [], []
    for _ in range(TOPK):
        m = jnp.max(work, axis=0, keepdims=True)
        idx = jnp.min(jnp.where(work == m, erow, NE), axis=0, keepdims=True)
        hit = erow == idx
        hits.append(hit)
        vals.append(m)
        idxs.append(idx)
        work = jnp.where(hit, -jnp.inf, work)
    ex = [jnp.exp(vk - vals[0]) for vk in vals]
    den = ex[0] + ex[1] + ex[2] + ex[3]
    multi = jnp.where(hits[0] | hits[1] | hits[2] | hits[3], 1.0, 0.0)
    ti = lax.broadcasted_iota(jnp.int32, (tr, tr), 0)
    tj = lax.broadcasted_iota(jnp.int32, (tr, tr), 1)
    before = jnp.where(ti < tj, 1.0, 0.0).astype(bf16)
    excl = jnp.dot(multi.astype(bf16), before, preferred_element_type=f32)
    total = jnp.sum(multi, axis=1, keepdims=True)
    units = jnp.floor((total + (ALIGN - 1)) * (1.0 / ALIGN))
    ei = lax.broadcasted_iota(jnp.int32, (NE, NE), 0)
    ej = lax.broadcasted_iota(jnp.int32, (NE, NE), 1)
    below = jnp.where(ej < ei, 1.0, 0.0).astype(bf16)
    off = ALIGN * jnp.dot(below, jnp.broadcast_to(units, (NE, 128)).astype(bf16),
                          preferred_element_type=f32)[:, 0:1]
    for kk in range(TOPK):
        e_out[kk:kk + 1, :] = idxs[kk]
        w_out[kk:kk + 1, :] = ex[kk] / den
        pos_out[kk:kk + 1, :] = jnp.sum(jnp.where(hits[kk], excl + off, 0.0), axis=0,
                                        keepdims=True).astype(jnp.int32)
    cnt_out[...] = jnp.broadcast_to(total, cnt_out.shape).astype(jnp.int32)


def _route(logits_t):
    t = logits_t.shape[1]
    sel = pl.BlockSpec((TOPK, TT), lambda i: (0, i))
    return pl.pallas_call(
        _route_kernel,
        name="route",
        grid=(t // TT,),
        in_specs=[pl.BlockSpec((NE, TT), lambda i: (0, i))],
        out_specs=[sel, sel, sel, pl.BlockSpec((None, NE, 128), lambda i: (i, 0, 0))],
        out_shape=[jax.ShapeDtypeStruct((TOPK, t), jnp.int32),
                   jax.ShapeDtypeStruct((TOPK, t), f32),
                   jax.ShapeDtypeStruct((TOPK, t), jnp.int32),
                   jax.ShapeDtypeStruct((t // TT, NE, 128), jnp.int32)],
        compiler_params=_cparams(("parallel",)),
    )(logits_t)


def _run_copies(tile, cnt_ref, make_copy, act):
    def per_expert(e, carry):
        cnt = pl.multiple_of(cnt_ref[tile * NE + e], ALIGN)

        @pl.when(cnt > 0)
        def _():
            act(make_copy(e, cnt))
        return carry
    lax.fori_loop(0, NE, per_expert, 0)


def _dispatch_kernel(cnt_ref, run_ref, off_ref, tot_ref, tail_ref, pos_ref, x2_ref, xs_ref,
                     sbuf, zbuf, sem, zsem):
    i = pl.program_id(0)
    last = pl.num_programs(0) - 1
    cur = i % 2

    @pl.when(i == 0)
    def _():
        zbuf[...] = jnp.zeros_like(zbuf)

        def zero_copy(start):
            return pltpu.make_async_copy(zbuf, xs_ref.at[pl.ds(pl.multiple_of(start, EBLK), EBLK)], zsem)

        for e in range(NE):
            @pl.when(tail_ref[e] >= 0)
            def _():
                zero_copy(tail_ref[e]).start()
        for e in range(NE):
            @pl.when(tail_ref[e] >= 0)
            def _():
                zero_copy(tail_ref[e]).wait()
        n_blk = xs_ref.shape[0] // EBLK
        lax.fori_loop(tail_ref[NE], n_blk, lambda blk, c: (zero_copy(blk * EBLK).start(), c)[1], 0)
        lax.fori_loop(tail_ref[NE], n_blk, lambda blk, c: (zero_copy(blk * EBLK).wait(), c)[1], 0)

    def slab(tile, half):
        def make(e, rows):
            src = pl.multiple_of(off_ref[tile * NE + e], ALIGN)
            dst = pl.multiple_of(run_ref[tile * NE + e], ALIGN)
            return pltpu.make_async_copy(sbuf.at[half, pl.ds(src, rows)],
                                         xs_ref.at[pl.ds(dst, rows)], sem.at[half])
        return make

    def drain(tile, half):
        n = pl.multiple_of(tot_ref[tile], ALIGN)
        pltpu.make_async_copy(sbuf.at[half, pl.ds(0, n)], xs_ref.at[pl.ds(0, n)], sem.at[half]).wait()

    @pl.when(i >= 2)
    def _():
        drain(i - 2, cur)

    pos = pos_ref[...]
    xt = x2_ref[...]
    for c in range(0, SB, SCHUNK):
        slot = lax.broadcasted_iota(jnp.int32, (SCHUNK, TT), 0) + c
        place = jnp.zeros((SCHUNK, TT), f32)
        for kk in range(TOPK):
            place = jnp.where(slot == pos[kk:kk + 1, :], 1.0, place)
        sbuf[cur, c:c + SCHUNK] = _pack_pairs(jnp.dot(place.astype(bf16), xt,
                                                      preferred_element_type=f32))
    _run_copies(i, cnt_ref, slab(i, cur), lambda cp: cp.start())

    @pl.when(i == last)
    def _():
        @pl.when(i >= 1)
        def _():
            drain(i - 1, 1 - cur)
        drain(i, cur)


def _dispatch(cnt16, run_start, off16, tot, tail, pos, x2, n_rows):
    t = x2.shape[0]
    return pl.pallas_call(
        _dispatch_kernel,
        name="dispatch",
        grid_spec=pltpu.PrefetchScalarGridSpec(
            num_scalar_prefetch=5, grid=(t // TT,),
            in_specs=[pl.BlockSpec((TOPK, TT), lambda i, *_: (0, i)),
                      pl.BlockSpec((TT, D), lambda i, *_: (i, 0))],
            out_specs=pl.BlockSpec(memory_space=pl.ANY),
            scratch_shapes=[pltpu.VMEM((2, SB, DH), jnp.uint32), pltpu.VMEM((EBLK, DH), jnp.uint32),
                            pltpu.SemaphoreType.DMA((2,)), pltpu.SemaphoreType.DMA(())]),
        out_shape=jax.ShapeDtypeStruct((n_rows, DH), jnp.uint32),
        compiler_params=_cparams(("arbitrary",)),
    )(cnt16, run_start, off16, tot, tail, pos, x2)


def _expert_kernel(be_ref, nu_ref, fx_ref, nx_ref, xs_ref, wg_ref, bg_ref, wu_ref, bu_ref, wd_ref, bd_ref,
                   ys_ref, wf32, wgb, wub, wdb, wsem):
    i = pl.program_id(0)

    def fetch(e):
        return [pltpu.make_async_copy(w.at[e], wf32.at[j], wsem)
                for j, w in enumerate((wg_ref, wu_ref, wd_ref))]

    @pl.when(fx_ref[i] == 1)
    def _():
        @pl.when(i == 0)
        def _():
            for cp in fetch(be_ref[0]):
                cp.start()
        for cp in fetch(be_ref[i]):
            cp.wait()
        wgb[...] = wf32[0].astype(bf16)
        wub[...] = wf32[1].astype(bf16)
        wdb[...] = wf32[2].astype(bf16)

        @pl.when(nx_ref[i] >= 0)
        def _():
            for cp in fetch(nx_ref[i]):
                cp.start()

    @pl.when(i < nu_ref[0])
    def _():
        xlo, xhi = _unpack_pairs(xs_ref[...])
        dot = functools.partial(jnp.dot, preferred_element_type=f32)
        gt = jnp.minimum(dot(xlo, wgb[:DH, :]) + dot(xhi, wgb[DH:, :]) + bg_ref[...], LIMIT)
        up = jnp.clip(dot(xlo, wub[:DH, :]) + dot(xhi, wub[DH:, :]) + bu_ref[...], -LIMIT, LIMIT)
        act = (up + 1.0) * (gt * jax.nn.sigmoid(gt * ALPHA))
        y = dot(act.astype(bf16), wdb[...]) + bd_ref[...]
        ys_ref[...] = _pack_pairs(y.astype(bf16).astype(f32))

    @pl.when(i >= nu_ref[0])
    def _():
        ys_ref[...] = jnp.zeros_like(ys_ref)


def _experts(blk_e, n_used, first, nxt_e, xs, wg, bg, wu, bu, wd, bd):
    p = xs.shape[0]
    bspec = pl.BlockSpec((None, 1, D), lambda i, be, nu, fx, nx: (be[i], 0, 0))
    xrows = pl.BlockSpec((EBLK, DH), lambda i, be, nu, fx, nx: (jnp.where(i < nu[0], i, 0), 0))
    yrows = pl.BlockSpec((EBLK, DH), lambda i, be, nu, fx, nx: (i, 0))
    hbm = pl.BlockSpec(memory_space=pl.ANY)
    return pl.pallas_call(
        _expert_kernel,
        name="experts",
        grid_spec=pltpu.PrefetchScalarGridSpec(
            num_scalar_prefetch=4, grid=(p // EBLK,),
            in_specs=[xrows, hbm, bspec, hbm, bspec, hbm, bspec],
            out_specs=yrows,
            scratch_shapes=[pltpu.VMEM((3, D, D), f32)] + [pltpu.VMEM((D, D), bf16)] * 3
                           + [pltpu.SemaphoreType.DMA(())]),
        out_shape=jax.ShapeDtypeStruct((p, DH), jnp.uint32),
        compiler_params=_cparams(("arbitrary",)),
    )(blk_e, n_used, first, nxt_e, xs, wg, bg, wu, bu, wd, bd)


def _combine_kernel(cnt_ref, run_ref, off_ref, tot_ref, ys_ref, pos_ref, w_ref, h_ref, g_ref, o_ref,
                    rbuf, sem):
    i = pl.program_id(0)
    cur = i % 2

    def fetch(tile, half):
        def slab(e, rows):
            src = pl.multiple_of(run_ref[tile * NE + e], ALIGN)
            dst = pl.multiple_of(off_ref[tile * NE + e], ALIGN)
            return pltpu.make_async_copy(ys_ref.at[pl.ds(src, rows)],
                                         rbuf.at[half, pl.ds(dst, rows)], sem.at[half])
        return slab

    @pl.when(i == 0)
    def _():
        rbuf[...] = jnp.zeros_like(rbuf)
        _run_copies(0, cnt_ref, fetch(0, 0), lambda cp: cp.start())

    @pl.when(i + 1 < pl.num_programs(0))
    def _():
        _run_copies(i + 1, cnt_ref, fetch(i + 1, 1 - cur), lambda cp: cp.start())

    n = pl.multiple_of(tot_ref[i], ALIGN)
    pltpu.make_async_copy(ys_ref.at[pl.ds(0, n)], rbuf.at[cur, pl.ds(0, n)], sem.at[cur]).wait()

    pos = pos_ref[...]
    w = w_ref[...]
    moe_lo = jnp.zeros((TT, DH), f32)
    moe_hi = jnp.zeros((TT, DH), f32)
    for c in range(0, SB, SCHUNK):
        slot = lax.broadcasted_iota(jnp.int32, (TT, SCHUNK), 1) + c
        gate = jnp.zeros((TT, SCHUNK), f32)
        for kk in range(TOPK):
            gate = jnp.where(slot == pos[:, kk:kk + 1], w[:, kk:kk + 1], gate)
        gb = gate.astype(bf16)
        lo, hi = _unpack_pairs(rbuf[cur, c:c + SCHUNK])
        moe_lo = moe_lo + jnp.dot(gb, lo, preferred_element_type=f32)
        moe_hi = moe_hi + jnp.dot(gb, hi, preferred_element_type=f32)
    acc = h_ref[...] + jnp.concatenate([moe_lo, moe_hi], axis=1)
    o_ref[...] = acc * lax.rsqrt(jnp.mean(acc * acc, axis=-1, keepdims=True) + NORM_EPS) * g_ref[...]


def _combine(cnt16, run_start, off16, tot, ys, pos_t, gates_t, hres, lnf):
    t = hres.shape[0]
    return pl.pallas_call(
        _combine_kernel,
        name="combine",
        grid_spec=pltpu.PrefetchScalarGridSpec(
            num_scalar_prefetch=4, grid=(t // TT,),
            in_specs=[pl.BlockSpec(memory_space=pl.ANY),
                      pl.BlockSpec((TT, TOPK), lambda i, *_: (i, 0)),
                      pl.BlockSpec((TT, TOPK), lambda i, *_: (i, 0)),
                      pl.BlockSpec((TT, D), lambda i, *_: (i, 0)),
                      pl.BlockSpec((1, D), lambda i, *_: (0, 0))],
            out_specs=pl.BlockSpec((TT, D), lambda i, *_: (i, 0)),
            scratch_shapes=[pltpu.VMEM((2, SB, DH), jnp.uint32), pltpu.SemaphoreType.DMA((2,))]),
        out_shape=jax.ShapeDtypeStruct((t, D), f32),
        compiler_params=_cparams(("arbitrary",)),
    )(cnt16, run_start, off16, tot, ys, pos_t, gates_t, hres, lnf)


def _row_tile(n, want):
    t = min(n, want)
    assert n % t == 0
    return t


def kernel(x, meta_tokens, ln1_g, w_in, mu_r, mu_k, mu_v, mu_w, mu_a, mu_g, w0, w_w1, w_w2, a0, w_a1, w_a2, w_g1, w_g2, k_k, k_a, r_k, lnx_g, lnx_b, w_o_rwkv, conv_w, w_o_conv, w_o, ln2_g, w_router, b_router, w_e_gate, b_e_gate, w_e_up, b_e_up, w_e_down, b_e_down, lnf_g):
    nb, seq, _ = x.shape
    t = nb * seq
    assert ln1_g.shape[0] == 1, "single layer"

    muw, mua, mug = mu_w[0][:, None], mu_a[0][:, None], mu_g[0][:, None]
    lora_cur = jnp.concatenate([(1 - muw) * w_w1[0], (1 - mua) * w_a1[0], (1 - mug) * w_g1[0]], axis=1)
    lora_prev = jnp.concatenate([muw * w_w1[0], mua * w_a1[0], mug * w_g1[0]], axis=1)
    zrow = jnp.zeros((1, R), f32)
    p = {
        "ln1": ln1_g[0][None, :],
        "w_in": jnp.concatenate([w_in[0][:, :6 * R], lora_cur, lora_prev, w_in[0][:, 6 * R:]],
                                axis=1).astype(bf16),
        "vec_mix": jnp.concatenate([mu_r, mu_k, mu_v, w0, a0, k_k, k_a, zrow], axis=0),
        "ww2": w_w2[0].astype(bf16), "wa2": w_a2[0].astype(bf16), "wg2": w_g2[0].astype(bf16),
        "cw": jnp.concatenate([conv_w[0], jnp.zeros((5, R), f32)], axis=0),
        "seg": (jnp.arange(R)[:, None] // HS == jnp.arange(R)[None, :] // HS).astype(bf16),
    }
    vec_merge = jnp.concatenate([lnx_g, lnx_b, r_k, jnp.zeros((5, R), f32)], axis=0)

    meta_out = _projmix(meta_tokens.astype(f32), NMETA, jnp.zeros((MARGIN, PW), f32), p, NMETA)
    pad = lambda z: jnp.pad(z, ((CHUNK - NMETA, 0), (0, 0)))[None]
    r_m, k_m, v_m, kk_m, b_m, lw_m = (pad(z) for z in meta_out[1:7])
    _, s_meta = _wkv(r_m, k_m, v_m, kk_m, b_m, lw_m, jnp.zeros((NH // 2, 2 * HS, 2 * HS), f32), 1)

    x2d = x.reshape(t, D)
    gt, r, k, v, kk, b, lw, g, cb, _ = _projmix(x2d, seq, meta_out[9], p, _row_tile(seq, 512))
    as3 = lambda z: z.reshape(nb, seq, R)
    y, _ = _wkv(as3(r), as3(k), as3(v), as3(kk), as3(b), as3(lw), s_meta[0], WKV_SEQS)
    hres, x2, logits_t = _merge(
        y.reshape(t, R), r, k, v, g, cb, gt, x2d, vec_merge, ln2_g[0][None, :], p["seg"],
        w_o_rwkv[0].astype(bf16), w_o_conv[0].astype(bf16), w_o[0].astype(bf16),
        w_router[0].T, b_router[0][:, None], _row_tile(t, 512))

    assert t % TT == 0
    nt = t // TT
    _, gates, pos, cnt = _route(logits_t)
    cnt = cnt[:, :, 0]
    cnt16 = ((cnt + ALIGN - 1) // ALIGN) * ALIGN
    base16 = jnp.cumsum(cnt16, axis=0) - cnt16
    tot16 = jnp.sum(cnt16, axis=0)
    padded = ((tot16 + EBLK - 1) // EBLK) * EBLK
    pend = jnp.cumsum(padded)
    pstart = pend - padded
    run_start = pstart[None, :] + base16
    off16 = jnp.cumsum(cnt16, axis=1) - cnt16
    n_rows = -(-(t * TOPK + nt * NE * (ALIGN - 1) + NE * EBLK) // EBLK) * EBLK
    n_blk = n_rows // EBLK
    blk_start = jnp.arange(n_blk, dtype=jnp.int32) * EBLK
    blk_e = jnp.minimum(jnp.sum(pend[None, :] <= blk_start[:, None], axis=1), NE - 1).astype(jnp.int32)
    n_used = (pend[NE - 1:] // EBLK).astype(jnp.int32)
    tail = jnp.concatenate([jnp.where(padded > 0, pend - EBLK, -1), n_used]).astype(jnp.int32)
    flat = lambda z: z.reshape(-1).astype(jnp.int32)
    tot = jnp.sum(cnt16, axis=1).astype(jnp.int32)
    xs = _dispatch(flat(cnt16), flat(run_start), flat(off16), tot, tail, pos, x2, n_rows)
    used = blk_start < pend[NE - 1]
    first = jnp.concatenate([jnp.ones((1,), bool), blk_e[1:] != blk_e[:-1]]) & used
    eidx = jnp.arange(NE, dtype=jnp.int32)
    later = (eidx[None, :] > eidx[:, None]) & (padded[None, :] > 0)
    next_of = jnp.where(jnp.any(later, axis=1), jnp.argmax(later, axis=1), -1).astype(jnp.int32)
    nxt_e = jnp.sum(jnp.where(blk_e[:, None] == eidx[None, :], next_of[None, :], 0), axis=1).astype(jnp.int32)
    ys = _experts(blk_e, n_used, first.astype(jnp.int32), nxt_e, xs,
                  w_e_gate[0], b_e_gate[0][:, None, :],
                  w_e_up[0], b_e_up[0][:, None, :],
                  w_e_down[0], b_e_down[0][:, None, :])
    out = _combine(flat(cnt16), flat(run_start), flat(off16), tot, ys, pos.T, gates.T, hres, lnf_g[None, :])
    return out.reshape(nb, seq, D)
```

```python
import functools

import jax
import jax.numpy as jnp
from jax import lax
from jax.experimental import pallas as pl
from jax.experimental.pallas import tpu as pltpu

D = 1024
R = 512
NH = 8
HS = 64
NE = 32
TOPK = 4
NMETA = 16
CHUNK = 64
WKV_SEQS = 8
EBLK = 512
ESUB = 256
TT = 512
ALIGN = 8
SB = 4 * TT + 256
DH = D // 2
MERGE_SUB = 256
SCHUNK = 768
NORM_EPS = 1e-5
LNX_EPS = 64e-5
ALPHA = 1.702
LIMIT = 7.0
DECAY_SCALE = 0.6065306597126334
LORA_W, LORA_A, LORA_G = 64, 64, 128
LORA = LORA_W + LORA_A + LORA_G
NPROJ = 3 * R + 3 * R + 2 * D + 2 * LORA
VMEM_LIMIT = 56 * 1024 * 1024

f32 = jnp.float32
bf16 = jnp.bfloat16


def _bdot(a, b):
    return jnp.dot(a.astype(bf16), b.astype(bf16), preferred_element_type=f32)


def _bdot_nt(a, b):
    return lax.dot_general(a.astype(bf16), b.astype(bf16), (((1,), (1,)), ((), ())),
                           preferred_element_type=f32)


def _bdot_tn(a, b):
    return lax.dot_general(a.astype(bf16), b.astype(bf16), (((0,), (0,)), ((), ())),
                           preferred_element_type=f32)


def _split3(x):
    h = x.astype(bf16)
    r1 = x - h.astype(f32)
    m = r1.astype(bf16)
    l = (r1 - m.astype(f32)).astype(bf16)
    return h, m, l


def _head_sum(x, seg):
    return jnp.dot(x.astype(bf16), seg, preferred_element_type=f32)


def _pack_pairs(x):
    a = lax.bitcast_convert_type(x[:, :DH], jnp.uint32)
    b = lax.bitcast_convert_type(x[:, DH:], jnp.uint32)
    return (a & jnp.uint32(0xFFFF0000)) | (b >> 16)


def _unpack_pairs(p):
    lo = lax.bitcast_convert_type(p & jnp.uint32(0xFFFF0000), f32)
    hi = lax.bitcast_convert_type(p << 16, f32)
    return lo.astype(bf16), hi.astype(bf16)


def _cparams(sem):
    return pltpu.CompilerParams(dimension_semantics=sem, vmem_limit_bytes=VMEM_LIMIT)


MARGIN = 16
PW = 3 * R + 3 * R + 2 * LORA


def _mix_rows(pbuf, a, n, vec, cw, seg, ww2, wa2, wg2, outs, o):
    r_out, k_out, v_out, kk_out, b_out, lw_out, g_out, cb_out = outs
    mu_r, mu_k, mu_v = vec[0:1, :], vec[1:2, :], vec[2:3, :]
    w0, a0, k_k, k_a = vec[3:4, :], vec[4:5, :], vec[5:6, :], vec[6:7, :]
    cur = pbuf[a:a + n, :]
    prev = pbuf[a - 1:a - 1 + n, :]
    r = cur[:, :R] + (prev[:, :R] - cur[:, :R]) * mu_r
    k = cur[:, R:2 * R] + (prev[:, R:2 * R] - cur[:, R:2 * R]) * mu_k
    v = cur[:, 2 * R:3 * R] + (prev[:, 2 * R:3 * R] - cur[:, 2 * R:3 * R]) * mu_v
    lo = 6 * R
    mixed = cur[:, lo:lo + LORA] + prev[:, lo + LORA:lo + 2 * LORA]
    hw = jnp.tanh(mixed[:, :LORA_W])
    ha = mixed[:, LORA_W:LORA_W + LORA_A]
    hg = jax.nn.sigmoid(mixed[:, LORA_W + LORA_A:])
    lw = -DECAY_SCALE * jax.nn.sigmoid(w0 + _bdot(hw, ww2))
    aa = jax.nn.sigmoid(a0 + _bdot(ha, wa2))
    g = _bdot(hg, wg2)
    kk = k * k_k
    kk = kk * lax.rsqrt(jnp.maximum(_head_sum(kk * kk, seg), 1e-24))
    k = k * (1.0 + (aa - 1.0) * k_a)
    rows = slice(o, o + n)
    r_out[rows, :] = r.astype(r_out.dtype)
    k_out[rows, :] = k.astype(k_out.dtype)
    v_out[rows, :] = v.astype(v_out.dtype)
    kk_out[rows, :] = kk.astype(kk_out.dtype)
    b_out[rows, :] = (kk * aa).astype(b_out.dtype)
    lw_out[rows, :] = lw
    g_out[rows, :] = g.astype(g_out.dtype)
    prev2 = pbuf[a - 2:a - 2 + n, 4 * R:6 * R]
    u0 = cur[:, 4 * R:5 * R] * cur[:, 5 * R:6 * R]
    u1 = prev[:, 4 * R:5 * R] * prev[:, 5 * R:6 * R]
    u2 = prev2[:, :R] * prev2[:, R:]
    conv = cw[2:3, :] * u0 + cw[1:2, :] * u1 + cw[0:1, :] * u2
    cb_out[rows, :] = (cur[:, 3 * R:4 * R] * conv).astype(cb_out.dtype)


def _projmix_kernel(tiles_per_seq, group, x_ref, g_ref, w_ref, init_ref, vec_ref, ww2_ref, wa2_ref,
                    wg2_ref, cw_ref, seg_ref, gt_out, r_out, k_out, v_out, kk_out, b_out, lw_out,
                    g_out, cb_out, tail_out, pbuf_a, pbuf_b):
    i = pl.program_id(0)
    tm = x_ref.shape[0]

    @pl.when(i == 0)
    def _():
        pbuf_a[...] = jnp.zeros_like(pbuf_a)
        pbuf_b[...] = jnp.zeros_like(pbuf_b)

    def step(pcur, pprv):
        first = (i % tiles_per_seq) == 0
        pcur[0:MARGIN, :] = jnp.where(first, init_ref[...], pprv[tm:tm + MARGIN, :])
        tail_out[...] = pprv[tm:tm + MARGIN, :]

        x = x_ref[...]
        xb = (x * lax.rsqrt(jnp.mean(x * x, axis=-1, keepdims=True) + NORM_EPS)
              * g_ref[...]).astype(bf16)
        outs = (r_out, k_out, v_out, kk_out, b_out, lw_out, g_out, cb_out)
        vec, cw, seg = vec_ref[...], cw_ref[...], seg_ref[...]
        ww2, wa2, wg2 = ww2_ref[...], wa2_ref[...], wg2_ref[...]

        def project(c):
            res = jnp.dot(xb, w_ref[:, c:c + 512], preferred_element_type=f32)
            if c < PW:
                pcur[MARGIN:MARGIN + tm, c:c + 512] = res
            else:
                gt_out[:, c - PW:c - PW + 512] = res.astype(gt_out.dtype)
            return res[0:8, :] * 0.0

        cols = list(range(0, NPROJ, 512))
        groups = list(range(0, tm, group))
        per = -(-len(cols) // len(groups))
        tie = jnp.zeros((8, R), f32)
        for n, o in enumerate(groups):
            for c in cols[n * per:(n + 1) * per]:
                tie = project(c)
            _mix_rows(pprv, MARGIN + o, group, vec + tie, cw + tie, seg, ww2, wa2, wg2, outs, o)

    @pl.when(i % 2 == 0)
    def _():
        step(pbuf_a, pbuf_b)

    @pl.when(i % 2 == 1)
    def _():
        step(pbuf_b, pbuf_a)


def _projmix(x2d, seq_len, init, p, tm):
    t = x2d.shape[0]
    n = t // tm
    group = min(tm, 64)

    def whole(shape):
        return pl.BlockSpec(shape, lambda i: (0,) * len(shape))

    lag = pl.BlockSpec((tm, R), lambda i: (jnp.maximum(i - 1, 0), 0))
    return pl.pallas_call(
        functools.partial(_projmix_kernel, seq_len // tm, group),
        name="projmix",
        grid=(n + 1,),
        in_specs=[pl.BlockSpec((tm, D), lambda i: (jnp.minimum(i, n - 1), 0)),
                  whole((1, D)),
                  pl.BlockSpec((D, NPROJ), lambda i: (0, 0), pipeline_mode=pl.Buffered(1)),
                  whole((MARGIN, PW)), whole((8, R)), whole((LORA_W, R)), whole((LORA_A, R)),
                  whole((LORA_G, R)), whole((8, R)), whole((R, R))],
        out_specs=[pl.BlockSpec((tm, 2 * D), lambda i: (jnp.minimum(i, n - 1), 0))] + [lag] * 8
                  + [whole((MARGIN, PW))],
        out_shape=[jax.ShapeDtypeStruct((t, 2 * D), bf16)]
                  + [jax.ShapeDtypeStruct((t, R), f32 if m == 5 else bf16) for m in range(8)]
                  + [jax.ShapeDtypeStruct((MARGIN, PW), f32)],
        scratch_shapes=[pltpu.VMEM((MARGIN + tm, PW), f32)] * 2,
        compiler_params=_cparams(("arbitrary",)),
    )(x2d, p["ln1"], p["w_in"], init, p["vec_mix"], p["ww2"], p["wa2"], p["wg2"], p["cw"], p["seg"])


def _wkv_kernel(r_ref, k_ref, v_ref, kk_ref, b_ref, lw_ref, s0_ref, y_ref, sT_ref, s_scr):
    c = pl.program_id(1)
    nbw = r_ref.shape[0]

    @pl.when(c == 0)
    def _():
        for j in range(nbw):
            s_scr[j] = s0_ref[...]

    C = CHUNK
    W2 = 2 * HS
    ri = lax.broadcasted_iota(jnp.int32, (C, C), 0)
    ci = lax.broadcasted_iota(jnp.int32, (C, C), 1)
    tri = jnp.where(ri >= ci, 1.0, 0.0).astype(bf16)
    rp = lax.broadcasted_iota(jnp.int32, (C, W2), 0)
    cp = lax.broadcasted_iota(jnp.int32, (C, W2), 1)
    lo = cp < HS
    lo1 = lax.broadcasted_iota(jnp.int32, (1, W2), 1) < HS
    sp = jnp.where(lo, cp, cp - HS)
    strict = rp > sp
    incl = rp >= sp
    eye = jnp.where(rp == sp, 1.0, 0.0).astype(f32)
    rq = lax.broadcasted_iota(jnp.int32, (W2, W2), 0)
    cq = lax.broadcasted_iota(jnp.int32, (W2, W2), 1)
    diag_blocks = (rq < HS) == (cq < HS)
    dot = functools.partial(jnp.dot, preferred_element_type=f32)

    def first(z):
        return jnp.where(lo1, z, jnp.zeros_like(z))

    def second(z):
        return jnp.where(lo1, jnp.zeros_like(z), z)

    def bdiag(z):
        return jnp.concatenate([first(z), second(z)], axis=0)

    def adiag(z):
        return jnp.concatenate([second(z), first(z)], axis=0)

    at, rt, bt, kt, bh, kh, vv, g_tot = [], [], [], [], [], [], [], []
    for j in range(nbw):
        lw = lw_ref[j]
        h, m, l = _split3(lw)
        cum = dot(tri, h) + dot(tri, m) + dot(tri, l)
        tot = cum[C - 1:C, :]
        k = k_ref[j].astype(f32)
        b = b_ref[j].astype(f32)
        g_inv = jnp.exp(-cum)
        tail = jnp.exp(tot - cum)
        rt.append(r_ref[j].astype(f32) * jnp.exp(cum))
        kt.append(k * g_inv)
        bt.append(b * g_inv)
        at.append(-kk_ref[j].astype(f32) * jnp.exp(cum - lw))
        bh.append(b * tail)
        kh.append(k * tail)
        vv.append(v_ref[j].astype(f32))
        g_tot.append(jnp.exp(tot))

    chains = [(j, p) for j in range(nbw) for p in range(NH // 2)]
    cs = range(len(chains))
    sl = lambda p: slice(p * W2, (p + 1) * W2)
    s_old = [s_scr[j, p] for j, p in chains]
    lhs = [jnp.concatenate([at[j][:, sl(p)], rt[j][:, sl(p)]], axis=0) for j, p in chains]
    rhs_n = [jnp.concatenate([bt[j][:, sl(p)], kt[j][:, sl(p)]], axis=0).astype(bf16) for j, p in chains]
    rhs_s = [jnp.concatenate([kt[j][:, sl(p)], bt[j][:, sl(p)]], axis=0).astype(bf16) for j, p in chains]
    lhs_b = [x.astype(bf16) for x in lhs]
    am1 = [_bdot_nt(first(lhs[n]), rhs_n[n]) for n in cs]
    am2 = [_bdot_nt(second(lhs[n]), rhs_s[n]) for n in cs]
    a_ab = [jnp.where(strict, jnp.where(lo, am1[n][:C], am2[n][:C]), 0.0) for n in cs]
    a_ak = [jnp.where(strict, jnp.where(lo, am2[n][:C], am1[n][:C]), 0.0) for n in cs]
    a_r1 = [jnp.where(incl, am1[n][C:], 0.0) for n in cs]
    a_r2 = [jnp.where(incl, am2[n][C:], 0.0) for n in cs]
    pb = [a_ab[n].astype(bf16) for n in cs]
    pw = [dot(pb[n], bdiag(pb[n])) for n in cs]
    x = [eye + a_ab[n] for n in cs]
    for _ in range(4):
        pb = [pw[n].astype(bf16) for n in cs]
        both = [dot(jnp.concatenate([pb[n], x[n].astype(bf16)], axis=0), bdiag(pb[n])) for n in cs]
        pw = [both[n][:C] for n in cs]
        x = [x[n] + both[n][C:] for n in cs]
    x = [x[n] + dot(x[n].astype(bf16), bdiag(pw[n].astype(bf16))) for n in cs]
    vp = [vv[j][:, sl(p)] for j, p in chains]
    sh = [_bdot_nt(lhs_b[n], s_old[n]) for n in cs]
    av = [dot(a_ak[n].astype(bf16), adiag(vp[n].astype(bf16))) for n in cs]
    u = [dot(x[n].astype(bf16), bdiag((sh[n][:C] + av[n]).astype(bf16))) for n in cs]
    ub = [u[n].astype(bf16) for n in cs]
    vb = [vp[n].astype(bf16) for n in cs]
    mix4 = [jnp.concatenate([first(ub[n]), first(vb[n]), second(vb[n]), second(ub[n])], axis=0)
            for n in cs]
    ys = [sh[n][C:] + dot(jnp.concatenate([a_r1[n], a_r2[n]], axis=1).astype(bf16), mix4[n])
          for n in cs]
    uv = [jnp.concatenate([ub[n], vb[n]], axis=0) for n in cs]
    bk = [jnp.concatenate([bh[j][:, sl(p)], kh[j][:, sl(p)]], axis=0) for j, p in chains]
    s_new = [s_old[n] * g_tot[j][:, sl(p)] + jnp.where(diag_blocks, _bdot_tn(uv[n], bk[n]), 0.0)
             for n, (j, p) in enumerate(chains)]
    for n, (j, p) in enumerate(chains):
        s_scr[j, p] = s_new[n]
        y_ref[j, :, sl(p)] = ys[n]

    @pl.when(c == pl.num_programs(1) - 1)
    def _():
        sT_ref[...] = s_scr[...]


def _wkv(r, k, v, kk, b, lw, s0, nbw):
    nb, seq, _ = r.shape
    assert nb % nbw == 0 and seq % CHUNK == 0
    blk = pl.BlockSpec((nbw, CHUNK, R), lambda i, c: (i, c, 0))
    st = (NH // 2, 2 * HS, 2 * HS)
    return pl.pallas_call(
        _wkv_kernel,
        name="wkv",
        grid=(nb // nbw, seq // CHUNK),
        in_specs=[blk] * 6 + [pl.BlockSpec(st, lambda i, c: (0, 0, 0))],
        out_specs=[blk, pl.BlockSpec((nbw,) + st, lambda i, c: (i, 0, 0, 0))],
        out_shape=[jax.ShapeDtypeStruct((nb, seq, R), f32),
                   jax.ShapeDtypeStruct((nb,) + st, f32)],
        scratch_shapes=[pltpu.VMEM((nbw,) + st, f32)],
        compiler_params=_cparams(("parallel", "arbitrary")),
    )(r, k, v, kk, b, lw, s0)


def _merge_kernel(y_ref, r_ref, k_ref, v_ref, g_ref, cb_ref, gt_ref, x_ref, vec_ref, ln2_ref,
                  seg_ref, worw_ref, woc_ref, wo_ref, wr_ref, br_ref,
                  h_out, x2_out, lg_out):
    seg = seg_ref[...]
    vec = vec_ref[...]
    lnx_g, lnx_b, r_k = vec[0:1, :], vec[1:2, :], vec[2:3, :]
    tm = y_ref.shape[0]
    rows = [slice(s0, s0 + MERGE_SUB) for s0 in range(0, tm, MERGE_SUB)]
    gs = range(len(rows))
    y = [y_ref[sl, :] for sl in rows]
    mean = [_head_sum(y[n], seg) * (1.0 / HS) for n in gs]
    yc = [y[n] - mean[n] for n in gs]
    var = [_head_sum(yc[n] * yc[n], seg) * (1.0 / HS) for n in gs]
    yn = [yc[n] * lax.rsqrt(var[n] + LNX_EPS) * lnx_g + lnx_b for n in gs]
    rk = [_head_sum(r_ref[sl, :].astype(f32) * k_ref[sl, :].astype(f32) * r_k, seg) for sl in rows]
    z = [(yn[n] + rk[n] * v_ref[rows[n], :].astype(f32)) * g_ref[rows[n], :].astype(f32) for n in gs]
    y_a = [_bdot(z[n], worw_ref[...]) for n in gs]
    y_b = [_bdot(cb_ref[sl, :], woc_ref[...]) for sl in rows]
    merged = [jax.nn.sigmoid(gt_ref[rows[n], :D].astype(f32)) * y_a[n]
              + jax.nn.sigmoid(gt_ref[rows[n], D:].astype(f32)) * y_b[n] for n in gs]
    hres = [x_ref[rows[n], :] + _bdot(merged[n], wo_ref[...]) for n in gs]
    x2 = [hres[n] * lax.rsqrt(jnp.mean(hres[n] * hres[n], axis=-1, keepdims=True) + NORM_EPS)
          * ln2_ref[...] for n in gs]
    wh, wm, _ = _split3(wr_ref[...])
    nt = functools.partial(lax.dot_general, dimension_numbers=(((1,), (1,)), ((), ())),
                           preferred_element_type=f32)
    for n, sl in enumerate(rows):
        h_out[sl, :] = hres[n]
        x2_out[sl, :] = x2[n].astype(x2_out.dtype)
        xh, xm, _ = _split3(x2[n])
        lg_out[:, sl] = nt(wh, xh) + nt(wh, xm) + nt(wm, xh) + br_ref[...]


def _merge(y, r, k, v, g, cb, gt, x2d, vec, ln2, seg, worw, woc, wo, wr_t, br, tm):
    t = y.shape[0]

    def rows(n):
        return pl.BlockSpec((tm, n), lambda i: (i, 0))

    def whole(shape):
        return pl.BlockSpec(shape, lambda i: (0,) * len(shape))

    return pl.pallas_call(
        _merge_kernel,
        name="merge",
        grid=(t // tm,),
        in_specs=[rows(R)] * 6 + [rows(2 * D), rows(D), whole((8, R)), whole((1, D)),
                                   whole((R, R)), whole((R, D)), whole((R, D)), whole((D, D)),
                                   whole((NE, D)), whole((NE, 1))],
        out_specs=[rows(D), rows(D), pl.BlockSpec((NE, tm), lambda i: (0, i))],
        out_shape=[jax.ShapeDtypeStruct((t, D), f32), jax.ShapeDtypeStruct((t, D), bf16),
                   jax.ShapeDtypeStruct((NE, t), f32)],
        compiler_params=_cparams(("parallel",)),
    )(y, r, k, v, g, cb, gt, x2d, vec, ln2, seg, worw, woc, wo, wr_t, br)


def _route_kernel(lg_ref, e_out, w_out, pos_out, cnt_out):
    lg = lg_ref[...]
    tr = lg.shape[1]
    erow = lax.broadcasted_iota(jnp.int32, lg.shape, 0)
    work = lg
    hits, vals, idxs = [], [], []
    for _ in range(TOPK):
        m = jnp.max(work, axis=0, keepdims=True)
        idx = jnp.min(jnp.where(work == m, erow, NE), axis=0, keepdims=True)
        hit = erow == idx
        hits.append(hit)
        vals.append(m)
        idxs.append(idx)
        work = jnp.where(hit, -jnp.inf, work)
    ex = [jnp.exp(vk - vals[0]) for vk in vals]
    den = ex[0] + ex[1] + ex[2] + ex[3]
    multi = jnp.where(hits[0] | hits[1] | hits[2] | hits[3], 1.0, 0.0)
    ti = lax.broadcasted_iota(jnp.int32, (tr, tr), 0)
    tj = lax.broadcasted_iota(jnp.int32, (tr, tr), 1)
    before = jnp.where(ti < tj, 1.0, 0.0).astype(bf16)
    excl = jnp.dot(multi.astype(bf16), before, preferred_element_type=f32)
    total = jnp.sum(multi, axis=1, keepdims=True)
    units = jnp.floor((total + (ALIGN - 1)) * (1.0 / ALIGN))
    ei = lax.broadcasted_iota(jnp.int32, (NE, NE), 0)
    ej = lax.broadcasted_iota(jnp.int32, (NE, NE), 1)
    below = jnp.where(ej < ei, 1.0, 0.0).astype(bf16)
    off = ALIGN * jnp.dot(below, jnp.broadcast_to(units, (NE, 128)).astype(bf16),
                          preferred_element_type=f32)[:, 0:1]
    for kk in range(TOPK):
        e_out[kk:kk + 1, :] = idxs[kk]
        w_out[kk:kk + 1, :] = ex[kk] / den
        pos_out[kk:kk + 1, :] = jnp.sum(jnp.where(hits[kk], excl + off, 0.0), axis=0,
                                        keepdims=True).astype(jnp.int32)
    cnt_out[...] = jnp.broadcast_to(total, cnt_out.shape).astype(jnp.int32)


def _route(logits_t):
    t = logits_t.shape[1]
    sel = pl.BlockSpec((TOPK, TT), lambda i: (0, i))
    return pl.pallas_call(
        _route_kernel,
        name="route",
        grid=(t // TT,),
        in_specs=[pl.BlockSpec((NE, TT), lambda i: (0, i))],
        out_specs=[sel, sel, sel, pl.BlockSpec((None, NE, 128), lambda i: (i, 0, 0))],
        out_shape=[jax.ShapeDtypeStruct((TOPK, t), jnp.int32),
                   jax.ShapeDtypeStruct((TOPK, t), f32),
                   jax.ShapeDtypeStruct((TOPK, t), jnp.int32),
                   jax.ShapeDtypeStruct((t // TT, NE, 128), jnp.int32)],
        compiler_params=_cparams(("parallel",)),
    )(logits_t)


def _run_copies(tile, cnt_ref, make_copy, act):
    def per_expert(e, carry):
        cnt = pl.multiple_of(cnt_ref[tile * NE + e], ALIGN)

        @pl.when(cnt > 0)
        def _():
            act(make_copy(e, cnt))
        return carry
    lax.fori_loop(0, NE, per_expert, 0)


def _dispatch_kernel(cnt_ref, run_ref, off_ref, tot_ref, tail_ref, pos_ref, x2_ref, xs_ref,
                     sbuf, zbuf, sem, zsem):
    i = pl.program_id(0)
    last = pl.num_programs(0) - 1
    cur = i % 2

    @pl.when(i == 0)
    def _():
        zbuf[...] = jnp.zeros_like(zbuf)

        def zero_copy(start):
            return pltpu.make_async_copy(zbuf, xs_ref.at[pl.ds(pl.multiple_of(start, EBLK), EBLK)], zsem)

        for e in range(NE):
            @pl.when(tail_ref[e] >= 0)
            def _():
                zero_copy(tail_ref[e]).start()
        for e in range(NE):
            @pl.when(tail_ref[e] >= 0)
            def _():
                zero_copy(tail_ref[e]).wait()
        n_blk = xs_ref.shape[0] // EBLK
        lax.fori_loop(tail_ref[NE], n_blk, lambda blk, c: (zero_copy(blk * EBLK).start(), c)[1], 0)
        lax.fori_loop(tail_ref[NE], n_blk, lambda blk, c: (zero_copy(blk * EBLK).wait(), c)[1], 0)

    def slab(tile, half):
        def make(e, rows):
            src = pl.multiple_of(off_ref[tile * NE + e], ALIGN)
            dst = pl.multiple_of(run_ref[tile * NE + e], ALIGN)
            return pltpu.make_async_copy(sbuf.at[half, pl.ds(src, rows)],
                                         xs_ref.at[pl.ds(dst, rows)], sem.at[half])
        return make

    def drain(tile, half):
        n = pl.multiple_of(tot_ref[tile], ALIGN)
        pltpu.make_async_copy(sbuf.at[half, pl.ds(0, n)], xs_ref.at[pl.ds(0, n)], sem.at[half]).wait()

    @pl.when(i >= 2)
    def _():
        drain(i - 2, cur)

    pos = pos_ref[...]
    xt = x2_ref[...]
    for c in range(0, SB, SCHUNK):
        slot = lax.broadcasted_iota(jnp.int32, (SCHUNK, TT), 0) + c
        place = jnp.zeros((SCHUNK, TT), f32)
        for kk in range(TOPK):
            place = jnp.where(slot == pos[kk:kk + 1, :], 1.0, place)
        sbuf[cur, c:c + SCHUNK] = _pack_pairs(jnp.dot(place.astype(bf16), xt,
                                                      preferred_element_type=f32))
    _run_copies(i, cnt_ref, slab(i, cur), lambda cp: cp.start())

    @pl.when(i == last)
    def _():
        @pl.when(i >= 1)
        def _():
            drain(i - 1, 1 - cur)
        drain(i, cur)


def _dispatch(cnt16, run_start, off16, tot, tail, pos, x2, n_rows):
    t = x2.shape[0]
    return pl.pallas_call(
        _dispatch_kernel,
        name="dispatch",
        grid_spec=pltpu.PrefetchScalarGridSpec(
            num_scalar_prefetch=5, grid=(t // TT,),
            in_specs=[pl.BlockSpec((TOPK, TT), lambda i, *_: (0, i)),
                      pl.BlockSpec((TT, D), lambda i, *_: (i, 0))],
            out_specs=pl.BlockSpec(memory_space=pl.ANY),
            scratch_shapes=[pltpu.VMEM((2, SB, DH), jnp.uint32), pltpu.VMEM((EBLK, DH), jnp.uint32),
                            pltpu.SemaphoreType.DMA((2,)), pltpu.SemaphoreType.DMA(())]),
        out_shape=jax.ShapeDtypeStruct((n_rows, DH), jnp.uint32),
        compiler_params=_cparams(("arbitrary",)),
    )(cnt16, run_start, off16, tot, tail, pos, x2)


def _expert_kernel(be_ref, nv_ref, fx_ref, nx_ref, xs_ref, wg_ref, bg_ref, wu_ref, bu_ref, wd_ref, bd_ref,
                   ys_ref, wf32, wgb, wub, wdb, wsem):
    i = pl.program_id(0)

    def fetch(e):
        return [pltpu.make_async_copy(w.at[e], wf32.at[j], wsem)
                for j, w in enumerate((wg_ref, wu_ref, wd_ref))]

    @pl.when(fx_ref[i] == 1)
    def _():
        @pl.when(i == 0)
        def _():
            for cp in fetch(be_ref[0]):
                cp.start()
        for cp in fetch(be_ref[i]):
            cp.wait()
        wgb[...] = wf32[0].astype(bf16)
        wub[...] = wf32[1].astype(bf16)
        wdb[...] = wf32[2].astype(bf16)

        @pl.when(nx_ref[i] >= 0)
        def _():
            for cp in fetch(nx_ref[i]):
                cp.start()

    def swiglu(n):
        xlo, xhi = _unpack_pairs(xs_ref[0:n, :])
        dot = functools.partial(jnp.dot, preferred_element_type=f32)
        gt = jnp.minimum(dot(xlo, wgb[:DH, :]) + dot(xhi, wgb[DH:, :]) + bg_ref[...], LIMIT)
        up = jnp.clip(dot(xlo, wub[:DH, :]) + dot(xhi, wub[DH:, :]) + bu_ref[...], -LIMIT, LIMIT)
        act = (up + 1.0) * (gt * jax.nn.sigmoid(gt * ALPHA))
        y = dot(act.astype(bf16), wdb[...]) + bd_ref[...]
        ys_ref[0:n, :] = _pack_pairs(y.astype(bf16).astype(f32))

    nv = nv_ref[i]

    @pl.when(nv > ESUB)
    def _():
        swiglu(EBLK)

    @pl.when((nv > 0) & (nv <= ESUB))
    def _():
        swiglu(ESUB)
        ys_ref[ESUB:, :] = jnp.zeros((EBLK - ESUB, DH), jnp.uint32)

    @pl.when(nv == 0)
    def _():
        ys_ref[...] = jnp.zeros_like(ys_ref)


def _experts(blk_e, nvalid, first, nxt_e, xs, wg, bg, wu, bu, wd, bd):
    p = xs.shape[0]
    bspec = pl.BlockSpec((None, 1, D), lambda i, be, nv, fx, nx: (be[i], 0, 0))
    xrows = pl.BlockSpec((EBLK, DH), lambda i, be, nv, fx, nx: (jnp.where(nv[i] > 0, i, 0), 0))
    yrows = pl.BlockSpec((EBLK, DH), lambda i, be, nv, fx, nx: (i, 0))
    hbm = pl.BlockSpec(memory_space=pl.ANY)
    return pl.pallas_call(
        _expert_kernel,
        name="experts",
        grid_spec=pltpu.PrefetchScalarGridSpec(
            num_scalar_prefetch=4, grid=(p // EBLK,),
            in_specs=[xrows, hbm, bspec, hbm, bspec, hbm, bspec],
            out_specs=yrows,
            scratch_shapes=[pltpu.VMEM((3, D, D), f32)] + [pltpu.VMEM((D, D), bf16)] * 3
                           + [pltpu.SemaphoreType.DMA(())]),
        out_shape=jax.ShapeDtypeStruct((p, DH), jnp.uint32),
        compiler_params=_cparams(("arbitrary",)),
    )(blk_e, nvalid, first, nxt_e, xs, wg, bg, wu, bu, wd, bd)


def _combine_kernel(cnt_ref, run_ref, off_ref, tot_ref, ys_ref, pos_ref, w_ref, h_ref, g_ref, o_ref,
                    rbuf, sem):
    i = pl.program_id(0)
    cur = i % 2

    def fetch(tile, half):
        def slab(e, rows):
            src = pl.multiple_of(run_ref[tile * NE + e], ALIGN)
            dst = pl.multiple_of(off_ref[tile * NE + e], ALIGN)
            return pltpu.make_async_copy(ys_ref.at[pl.ds(src, rows)],
                                         rbuf.at[half, pl.ds(dst, rows)], sem.at[half])
        return slab

    @pl.when(i == 0)
    def _():
        rbuf[...] = jnp.zeros_like(rbuf)
        _run_copies(0, cnt_ref, fetch(0, 0), lambda cp: cp.start())

    @pl.when(i + 1 < pl.num_programs(0))
    def _():
        _run_copies(i + 1, cnt_ref, fetch(i + 1, 1 - cur), lambda cp: cp.start())

    n = pl.multiple_of(tot_ref[i], ALIGN)
    pltpu.make_async_copy(ys_ref.at[pl.ds(0, n)], rbuf.at[cur, pl.ds(0, n)], sem.at[cur]).wait()

    pos = pos_ref[...]
    w = w_ref[...]
    moe_lo = jnp.zeros((TT, DH), f32)
    moe_hi = jnp.zeros((TT, DH), f32)
    for c in range(0, SB, SCHUNK):
        slot = lax.broadcasted_iota(jnp.int32, (TT, SCHUNK), 1) + c
        gate = jnp.zeros((TT, SCHUNK), f32)
        for kk in range(TOPK):
            gate = jnp.where(slot == pos[:, kk:kk + 1], w[:, kk:kk + 1], gate)
        gb = gate.astype(bf16)
        lo, hi = _unpack_pairs(rbuf[cur, c:c + SCHUNK])
        moe_lo = moe_lo + jnp.dot(gb, lo, preferred_element_type=f32)
        moe_hi = moe_hi + jnp.dot(gb, hi, preferred_element_type=f32)
    acc = h_ref[...] + jnp.concatenate([moe_lo, moe_hi], axis=1)
    o_ref[...] = acc * lax.rsqrt(jnp.mean(acc * acc, axis=-1, keepdims=True) + NORM_EPS) * g_ref[...]


def _combine(cnt16, run_start, off16, tot, ys, pos_t, gates_t, hres, lnf):
    t = hres.shape[0]
    return pl.pallas_call(
        _combine_kernel,
        name="combine",
        grid_spec=pltpu.PrefetchScalarGridSpec(
            num_scalar_prefetch=4, grid=(t // TT,),
            in_specs=[pl.BlockSpec(memory_space=pl.ANY),
                      pl.BlockSpec((TT, TOPK), lambda i, *_: (i, 0)),
                      pl.BlockSpec((TT, TOPK), lambda i, *_: (i, 0)),
                      pl.BlockSpec((TT, D), lambda i, *_: (i, 0)),
                      pl.BlockSpec((1, D), lambda i, *_: (0, 0))],
            out_specs=pl.BlockSpec((TT, D), lambda i, *_: (i, 0)),
            scratch_shapes=[pltpu.VMEM((2, SB, DH), jnp.uint32), pltpu.SemaphoreType.DMA((2,))]),
        out_shape=jax.ShapeDtypeStruct((t, D), f32),
        compiler_params=_cparams(("arbitrary",)),
    )(cnt16, run_start, off16, tot, ys, pos_t, gates_t, hres, lnf)


def _row_tile(n, want):
    t = min(n, want)
    assert n % t == 0
    return t


def kernel(x, meta_tokens, ln1_g, w_in, mu_r, mu_k, mu_v, mu_w, mu_a, mu_g, w0, w_w1, w_w2, a0, w_a1, w_a2, w_g1, w_g2, k_k, k_a, r_k, lnx_g, lnx_b, w_o_rwkv, conv_w, w_o_conv, w_o, ln2_g, w_router, b_router, w_e_gate, b_e_gate, w_e_up, b_e_up, w_e_down, b_e_down, lnf_g):
    nb, seq, _ = x.shape
    t = nb * seq
    assert ln1_g.shape[0] == 1, "single layer"

    muw, mua, mug = mu_w[0][:, None], mu_a[0][:, None], mu_g[0][:, None]
    lora_cur = jnp.concatenate([(1 - muw) * w_w1[0], (1 - mua) * w_a1[0], (1 - mug) * w_g1[0]], axis=1)
    lora_prev = jnp.concatenate([muw * w_w1[0], mua * w_a1[0], mug * w_g1[0]], axis=1)
    zrow = jnp.zeros((1, R), f32)
    p = {
        "ln1": ln1_g[0][None, :],
        "w_in": jnp.concatenate([w_in[0][:, :6 * R], lora_cur, lora_prev, w_in[0][:, 6 * R:]],
                                axis=1).astype(bf16),
        "vec_mix": jnp.concatenate([mu_r, mu_k, mu_v, w0, a0, k_k, k_a, zrow], axis=0),
        "ww2": w_w2[0].astype(bf16), "wa2": w_a2[0].astype(bf16), "wg2": w_g2[0].astype(bf16),
        "cw": jnp.concatenate([conv_w[0], jnp.zeros((5, R), f32)], axis=0),
        "seg": (jnp.arange(R)[:, None] // HS == jnp.arange(R)[None, :] // HS).astype(bf16),
    }
    vec_merge = jnp.concatenate([lnx_g, lnx_b, r_k, jnp.zeros((5, R), f32)], axis=0)

    meta_out = _projmix(meta_tokens.astype(f32), NMETA, jnp.zeros((MARGIN, PW), f32), p, NMETA)
    pad = lambda z: jnp.pad(z, ((CHUNK - NMETA, 0), (0, 0)))[None]
    r_m, k_m, v_m, kk_m, b_m, lw_m = (pad(z) for z in meta_out[1:7])
    _, s_meta = _wkv(r_m, k_m, v_m, kk_m, b_m, lw_m, jnp.zeros((NH // 2, 2 * HS, 2 * HS), f32), 1)

    x2d = x.reshape(t, D)
    gt, r, k, v, kk, b, lw, g, cb, _ = _projmix(x2d, seq, meta_out[9], p, _row_tile(seq, 512))
    as3 = lambda z: z.reshape(nb, seq, R)
    y, _ = _wkv(as3(r), as3(k), as3(v), as3(kk), as3(b), as3(lw), s_meta[0], WKV_SEQS)
    hres, x2, logits_t = _merge(
        y.reshape(t, R), r, k, v, g, cb, gt, x2d, vec_merge, ln2_g[0][None, :], p["seg"],
        w_o_rwkv[0].astype(bf16), w_o_conv[0].astype(bf16), w_o[0].astype(bf16),
        w_router[0].T, b_router[0][:, None], _row_tile(t, 512))

    assert t % TT == 0
    nt = t // TT
    _, gates, pos, cnt = _route(logits_t)
    cnt = cnt[:, :, 0]
    cnt16 = ((cnt + ALIGN - 1) // ALIGN) * ALIGN
    base16 = jnp.cumsum(cnt16, axis=0) - cnt16
    tot16 = jnp.sum(cnt16, axis=0)
    padded = ((tot16 + EBLK - 1) // EBLK) * EBLK
    pend = jnp.cumsum(padded)
    pstart = pend - padded
    run_start = pstart[None, :] + base16
    off16 = jnp.cumsum(cnt16, axis=1) - cnt16
    n_rows = -(-(t * TOPK + nt * NE * (ALIGN - 1) + NE * EBLK) // EBLK) * EBLK
    n_blk = n_rows // EBLK
    blk_start = jnp.arange(n_blk, dtype=jnp.int32) * EBLK
    blk_e = jnp.minimum(jnp.sum(pend[None, :] <= blk_start[:, None], axis=1), NE - 1).astype(jnp.int32)
    n_used = (pend[NE - 1:] // EBLK).astype(jnp.int32)
    tail = jnp.concatenate([jnp.where(padded > 0, pend - EBLK, -1), n_used]).astype(jnp.int32)
    flat = lambda z: z.reshape(-1).astype(jnp.int32)
    tot = jnp.sum(cnt16, axis=1).astype(jnp.int32)
    xs = _dispatch(flat(cnt16), flat(run_start), flat(off16), tot, tail, pos, x2, n_rows)
    used = blk_start < pend[NE - 1]
    first = jnp.concatenate([jnp.ones((1,), bool), blk_e[1:] != blk_e[:-1]]) & used
    eidx = jnp.arange(NE, dtype=jnp.int32)
    later = (eidx[None, :] > eidx[:, None]) & (padded[None, :] > 0)
    next_of = jnp.where(jnp.any(later, axis=1), jnp.argmax(later, axis=1), -1).astype(jnp.int32)
    mine = blk_e[:, None] == eidx[None, :]
    nxt_e = jnp.sum(jnp.where(mine, next_of[None, :], 0), axis=1).astype(jnp.int32)
    run_end = jnp.sum(jnp.where(mine, (pstart + tot16)[None, :], 0), axis=1)
    nvalid = jnp.where(used, jnp.clip(run_end - blk_start, 0, EBLK), 0).astype(jnp.int32)
    ys = _experts(blk_e, nvalid, first.astype(jnp.int32), nxt_e, xs,
                  w_e_gate[0], b_e_gate[0][:, None, :],
                  w_e_up[0], b_e_up[0][:, None, :],
                  w_e_down[0], b_e_down[0][:, None, :])
    out = _combine(flat(cnt16), flat(run_start), flat(off16), tot, ys, pos.T, gates.T, hres, lnf_g[None, :])
    return out.reshape(nb, seq, D)
```

```python
import functools

import jax
import jax.numpy as jnp
from jax import lax
from jax.experimental import pallas as pl
from jax.experimental.pallas import tpu as pltpu

D = 1024
R = 512
NH = 8
HS = 64
NE = 32
TOPK = 4
NMETA = 16
CHUNK = 64
WKV_SEQS = 8
EBLK = 512
ESUB = 256
TT = 512
ALIGN = 8
SB = 4 * TT + 256
DH = D // 2
MERGE_SUB = 256
DCHUNK = 256
SCHUNK = 768
NORM_EPS = 1e-5
LNX_EPS = 64e-5
ALPHA = 1.702
LIMIT = 7.0
DECAY_SCALE = 0.6065306597126334
LORA_W, LORA_A, LORA_G = 64, 64, 128
LORA = LORA_W + LORA_A + LORA_G
NPROJ = 3 * R + 3 * R + 2 * D + 2 * LORA
VMEM_LIMIT = 56 * 1024 * 1024

f32 = jnp.float32
bf16 = jnp.bfloat16


def _bdot(a, b):
    return jnp.dot(a.astype(bf16), b.astype(bf16), preferred_element_type=f32)


def _bdot_nt(a, b):
    return lax.dot_general(a.astype(bf16), b.astype(bf16), (((1,), (1,)), ((), ())),
                           preferred_element_type=f32)


def _bdot_tn(a, b):
    return lax.dot_general(a.astype(bf16), b.astype(bf16), (((0,), (0,)), ((), ())),
                           preferred_element_type=f32)


def _split3(x):
    h = x.astype(bf16)
    r1 = x - h.astype(f32)
    m = r1.astype(bf16)
    l = (r1 - m.astype(f32)).astype(bf16)
    return h, m, l


def _head_sum(x, seg):
    return jnp.dot(x.astype(bf16), seg, preferred_element_type=f32)


def _pack_pairs(x):
    a = lax.bitcast_convert_type(x[:, :DH], jnp.uint32)
    b = lax.bitcast_convert_type(x[:, DH:], jnp.uint32)
    return (a & jnp.uint32(0xFFFF0000)) | (b >> 16)


def _unpack_pairs(p):
    lo = lax.bitcast_convert_type(p & jnp.uint32(0xFFFF0000), f32)
    hi = lax.bitcast_convert_type(p << 16, f32)
    return lo.astype(bf16), hi.astype(bf16)


def _cparams(sem):
    return pltpu.CompilerParams(dimension_semantics=sem, vmem_limit_bytes=VMEM_LIMIT)


MARGIN = 16
PW = 3 * R + 3 * R + 2 * LORA


def _mix_rows(pbuf, a, n, vec, cw, seg, ww2, wa2, wg2, outs, o):
    r_out, k_out, v_out, kk_out, b_out, lw_out, g_out, cb_out = outs
    mu_r, mu_k, mu_v = vec[0:1, :], vec[1:2, :], vec[2:3, :]
    w0, a0, k_k, k_a = vec[3:4, :], vec[4:5, :], vec[5:6, :], vec[6:7, :]
    cur = pbuf[a:a + n, :]
    prev = pbuf[a - 1:a - 1 + n, :]
    r = cur[:, :R] + (prev[:, :R] - cur[:, :R]) * mu_r
    k = cur[:, R:2 * R] + (prev[:, R:2 * R] - cur[:, R:2 * R]) * mu_k
    v = cur[:, 2 * R:3 * R] + (prev[:, 2 * R:3 * R] - cur[:, 2 * R:3 * R]) * mu_v
    lo = 6 * R
    mixed = cur[:, lo:lo + LORA] + prev[:, lo + LORA:lo + 2 * LORA]
    hw = jnp.tanh(mixed[:, :LORA_W])
    ha = mixed[:, LORA_W:LORA_W + LORA_A]
    hg = jax.nn.sigmoid(mixed[:, LORA_W + LORA_A:])
    lw = -DECAY_SCALE * jax.nn.sigmoid(w0 + _bdot(hw, ww2))
    aa = jax.nn.sigmoid(a0 + _bdot(ha, wa2))
    g = _bdot(hg, wg2)
    kk = k * k_k
    kk = kk * lax.rsqrt(jnp.maximum(_head_sum(kk * kk, seg), 1e-24))
    k = k * (1.0 + (aa - 1.0) * k_a)
    rows = slice(o, o + n)
    r_out[rows, :] = r.astype(r_out.dtype)
    k_out[rows, :] = k.astype(k_out.dtype)
    v_out[rows, :] = v.astype(v_out.dtype)
    kk_out[rows, :] = kk.astype(kk_out.dtype)
    b_out[rows, :] = (kk * aa).astype(b_out.dtype)
    lw_out[rows, :] = lw
    g_out[rows, :] = g.astype(g_out.dtype)
    prev2 = pbuf[a - 2:a - 2 + n, 4 * R:6 * R]
    u0 = cur[:, 4 * R:5 * R] * cur[:, 5 * R:6 * R]
    u1 = prev[:, 4 * R:5 * R] * prev[:, 5 * R:6 * R]
    u2 = prev2[:, :R] * prev2[:, R:]
    conv = cw[2:3, :] * u0 + cw[1:2, :] * u1 + cw[0:1, :] * u2
    cb_out[rows, :] = (cur[:, 3 * R:4 * R] * conv).astype(cb_out.dtype)


def _projmix_kernel(tiles_per_seq, group, x_ref, g_ref, w_ref, wl_ref, init_ref, vec_ref, ww2_ref, wa2_ref,
                    wg2_ref, cw_ref, seg_ref, gt_out, r_out, k_out, v_out, kk_out, b_out, lw_out,
                    g_out, cb_out, tail_out, pbuf_a, pbuf_b):
    i = pl.program_id(0)
    tm = x_ref.shape[0]

    @pl.when(i == 0)
    def _():
        pbuf_a[...] = jnp.zeros_like(pbuf_a)
        pbuf_b[...] = jnp.zeros_like(pbuf_b)

    def step(pcur, pprv):
        first = (i % tiles_per_seq) == 0
        pcur[0:MARGIN, :] = jnp.where(first, init_ref[...], pprv[tm:tm + MARGIN, :])
        tail_out[...] = pprv[tm:tm + MARGIN, :]

        x = x_ref[...]
        xb = (x * lax.rsqrt(jnp.mean(x * x, axis=-1, keepdims=True) + NORM_EPS)
              * g_ref[...]).astype(bf16)
        outs = (r_out, k_out, v_out, kk_out, b_out, lw_out, g_out, cb_out)
        vec, cw, seg = vec_ref[...], cw_ref[...], seg_ref[...]
        ww2, wa2, wg2 = ww2_ref[...], wa2_ref[...], wg2_ref[...]

        def project(c):
            if c < 6 * R:
                wc = w_ref[:, c:c + 512]
            elif c < PW:
                wc = wl_ref[:, c - 6 * R:c - 6 * R + 512]
            else:
                wc = w_ref[:, c - 2 * LORA:c - 2 * LORA + 512]
            res = jnp.dot(xb, wc, preferred_element_type=f32)
            if c < PW:
                pcur[MARGIN:MARGIN + tm, c:c + 512] = res
            else:
                gt_out[:, c - PW:c - PW + 512] = res.astype(gt_out.dtype)
            return res[0:8, :] * 0.0

        cols = list(range(0, NPROJ, 512))
        groups = list(range(0, tm, group))
        per = -(-len(cols) // len(groups))
        tie = jnp.zeros((8, R), f32)
        for n, o in enumerate(groups):
            for c in cols[n * per:(n + 1) * per]:
                tie = project(c)
            _mix_rows(pprv, MARGIN + o, group, vec + tie, cw + tie, seg, ww2, wa2, wg2, outs, o)

    @pl.when(i % 2 == 0)
    def _():
        step(pbuf_a, pbuf_b)

    @pl.when(i % 2 == 1)
    def _():
        step(pbuf_b, pbuf_a)


def _projmix(x2d, seq_len, init, p, tm):
    t = x2d.shape[0]
    n = t // tm
    group = min(tm, 64)

    def whole(shape):
        return pl.BlockSpec(shape, lambda i: (0,) * len(shape))

    lag = pl.BlockSpec((tm, R), lambda i: (jnp.maximum(i - 1, 0), 0))
    return pl.pallas_call(
        functools.partial(_projmix_kernel, seq_len // tm, group),
        name="projmix",
        grid=(n + 1,),
        in_specs=[pl.BlockSpec((tm, D), lambda i: (jnp.minimum(i, n - 1), 0)),
                  whole((1, D)),
                  pl.BlockSpec((D, NPROJ - 2 * LORA), lambda i: (0, 0), pipeline_mode=pl.Buffered(1)),
                  pl.BlockSpec((D, 2 * LORA), lambda i: (0, 0), pipeline_mode=pl.Buffered(1)),
                  whole((MARGIN, PW)), whole((8, R)), whole((LORA_W, R)), whole((LORA_A, R)),
                  whole((LORA_G, R)), whole((8, R)), whole((R, R))],
        out_specs=[pl.BlockSpec((tm, 2 * D), lambda i: (jnp.minimum(i, n - 1), 0))] + [lag] * 8
                  + [whole((MARGIN, PW))],
        out_shape=[jax.ShapeDtypeStruct((t, 2 * D), bf16)]
                  + [jax.ShapeDtypeStruct((t, R), f32 if m == 5 else bf16) for m in range(8)]
                  + [jax.ShapeDtypeStruct((MARGIN, PW), f32)],
        scratch_shapes=[pltpu.VMEM((MARGIN + tm, PW), f32)] * 2,
        compiler_params=_cparams(("arbitrary",)),
    )(x2d, p["ln1"], p["w_in"], p["w_lora"], init, p["vec_mix"], p["ww2"], p["wa2"], p["wg2"], p["cw"], p["seg"])


def _wkv_kernel(r_ref, k_ref, v_ref, kk_ref, b_ref, lw_ref, s0_ref, y_ref, sT_ref, s_scr):
    c = pl.program_id(1)
    nbw = r_ref.shape[0]

    @pl.when(c == 0)
    def _():
        for j in range(nbw):
            s_scr[j] = s0_ref[...]

    C = CHUNK
    W2 = 2 * HS
    ri = lax.broadcasted_iota(jnp.int32, (C, C), 0)
    ci = lax.broadcasted_iota(jnp.int32, (C, C), 1)
    tri = jnp.where(ri >= ci, 1.0, 0.0).astype(bf16)
    rp = lax.broadcasted_iota(jnp.int32, (C, W2), 0)
    cp = lax.broadcasted_iota(jnp.int32, (C, W2), 1)
    lo = cp < HS
    lo1 = lax.broadcasted_iota(jnp.int32, (1, W2), 1) < HS
    sp = jnp.where(lo, cp, cp - HS)
    strict = rp > sp
    incl = rp >= sp
    eye = jnp.where(rp == sp, 1.0, 0.0).astype(f32)
    rq = lax.broadcasted_iota(jnp.int32, (W2, W2), 0)
    cq = lax.broadcasted_iota(jnp.int32, (W2, W2), 1)
    diag_blocks = (rq < HS) == (cq < HS)
    dot = functools.partial(jnp.dot, preferred_element_type=f32)

    def first(z):
        return jnp.where(lo1, z, jnp.zeros_like(z))

    def second(z):
        return jnp.where(lo1, jnp.zeros_like(z), z)

    def bdiag(z):
        return jnp.concatenate([first(z), second(z)], axis=0)

    def adiag(z):
        return jnp.concatenate([second(z), first(z)], axis=0)

    at, rt, bt, kt, bh, kh, vv, g_tot = [], [], [], [], [], [], [], []
    for j in range(nbw):
        lw = lw_ref[j]
        h, m, l = _split3(lw)
        cum = dot(tri, h) + dot(tri, m) + dot(tri, l)
        tot = cum[C - 1:C, :]
        k = k_ref[j].astype(f32)
        b = b_ref[j].astype(f32)
        g_inv = jnp.exp(-cum)
        tail = jnp.exp(tot - cum)
        rt.append(r_ref[j].astype(f32) * jnp.exp(cum))
        kt.append(k * g_inv)
        bt.append(b * g_inv)
        at.append(-kk_ref[j].astype(f32) * jnp.exp(cum - lw))
        bh.append(b * tail)
        kh.append(k * tail)
        vv.append(v_ref[j].astype(f32))
        g_tot.append(jnp.exp(tot))

    chains = [(j, p) for j in range(nbw) for p in range(NH // 2)]
    cs = range(len(chains))
    sl = lambda p: slice(p * W2, (p + 1) * W2)
    s_old = [s_scr[j, p] for j, p in chains]
    lhs = [jnp.concatenate([at[j][:, sl(p)], rt[j][:, sl(p)]], axis=0) for j, p in chains]
    rhs_n = [jnp.concatenate([bt[j][:, sl(p)], kt[j][:, sl(p)]], axis=0).astype(bf16) for j, p in chains]
    rhs_s = [jnp.concatenate([kt[j][:, sl(p)], bt[j][:, sl(p)]], axis=0).astype(bf16) for j, p in chains]
    lhs_b = [x.astype(bf16) for x in lhs]
    am1 = [_bdot_nt(first(lhs[n]), rhs_n[n]) for n in cs]
    am2 = [_bdot_nt(second(lhs[n]), rhs_s[n]) for n in cs]
    a_ab = [jnp.where(strict, jnp.where(lo, am1[n][:C], am2[n][:C]), 0.0) for n in cs]
    a_ak = [jnp.where(strict, jnp.where(lo, am2[n][:C], am1[n][:C]), 0.0) for n in cs]
    a_r1 = [jnp.where(incl, am1[n][C:], 0.0) for n in cs]
    a_r2 = [jnp.where(incl, am2[n][C:], 0.0) for n in cs]
    pb = [a_ab[n].astype(bf16) for n in cs]
    pw = [dot(pb[n], bdiag(pb[n])) for n in cs]
    x = [eye + a_ab[n] for n in cs]
    for _ in range(4):
        pb = [pw[n].astype(bf16) for n in cs]
        both = [dot(jnp.concatenate([pb[n], x[n].astype(bf16)], axis=0), bdiag(pb[n])) for n in cs]
        pw = [both[n][:C] for n in cs]
        x = [x[n] + both[n][C:] for n in cs]
    x = [x[n] + dot(x[n].astype(bf16), bdiag(pw[n].astype(bf16))) for n in cs]
    vp = [vv[j][:, sl(p)] for j, p in chains]
    sh = [_bdot_nt(lhs_b[n], s_old[n]) for n in cs]
    av = [dot(a_ak[n].astype(bf16), adiag(vp[n].astype(bf16))) for n in cs]
    u = [dot(x[n].astype(bf16), bdiag((sh[n][:C] + av[n]).astype(bf16))) for n in cs]
    ub = [u[n].astype(bf16) for n in cs]
    vb = [vp[n].astype(bf16) for n in cs]
    mix4 = [jnp.concatenate([first(ub[n]), first(vb[n]), second(vb[n]), second(ub[n])], axis=0)
            for n in cs]
    ys = [sh[n][C:] + dot(jnp.concatenate([a_r1[n], a_r2[n]], axis=1).astype(bf16), mix4[n])
          for n in cs]
    uv = [jnp.concatenate([ub[n], vb[n]], axis=0) for n in cs]
    bk = [jnp.concatenate([bh[j][:, sl(p)], kh[j][:, sl(p)]], axis=0) for j, p in chains]
    s_new = [s_old[n] * g_tot[j][:, sl(p)] + jnp.where(diag_blocks, _bdot_tn(uv[n], bk[n]), 0.0)
             for n, (j, p) in enumerate(chains)]
    for n, (j, p) in enumerate(chains):
        s_scr[j, p] = s_new[n]
        y_ref[j, :, sl(p)] = ys[n]

    @pl.when(c == pl.num_programs(1) - 1)
    def _():
        sT_ref[...] = s_scr[...]


def _wkv(r, k, v, kk, b, lw, s0, nbw):
    nb, seq, _ = r.shape
    assert nb % nbw == 0 and seq % CHUNK == 0
    blk = pl.BlockSpec((nbw, CHUNK, R), lambda i, c: (i, c, 0))
    st = (NH // 2, 2 * HS, 2 * HS)
    return pl.pallas_call(
        _wkv_kernel,
        name="wkv",
        grid=(nb // nbw, seq // CHUNK),
        in_specs=[blk] * 6 + [pl.BlockSpec(st, lambda i, c: (0, 0, 0))],
        out_specs=[blk, pl.BlockSpec((nbw,) + st, lambda i, c: (i, 0, 0, 0))],
        out_shape=[jax.ShapeDtypeStruct((nb, seq, R), f32),
                   jax.ShapeDtypeStruct((nb,) + st, f32)],
        scratch_shapes=[pltpu.VMEM((nbw,) + st, f32)],
        compiler_params=_cparams(("parallel", "arbitrary")),
    )(r, k, v, kk, b, lw, s0)


def _merge_kernel(y_ref, r_ref, k_ref, v_ref, g_ref, cb_ref, gt_ref, x_ref, vec_ref, ln2_ref,
                  seg_ref, worw_ref, woc_ref, wo_ref, wr_ref, br_ref,
                  h_out, x2_out, lg_out):
    seg = seg_ref[...]
    vec = vec_ref[...]
    lnx_g, lnx_b, r_k = vec[0:1, :], vec[1:2, :], vec[2:3, :]
    tm = y_ref.shape[0]
    rows = [slice(s0, s0 + MERGE_SUB) for s0 in range(0, tm, MERGE_SUB)]
    gs = range(len(rows))
    y = [y_ref[sl, :] for sl in rows]
    mean = [_head_sum(y[n], seg) * (1.0 / HS) for n in gs]
    yc = [y[n] - mean[n] for n in gs]
    var = [_head_sum(yc[n] * yc[n], seg) * (1.0 / HS) for n in gs]
    yn = [yc[n] * lax.rsqrt(var[n] + LNX_EPS) * lnx_g + lnx_b for n in gs]
    rk = [_head_sum(r_ref[sl, :].astype(f32) * k_ref[sl, :].astype(f32) * r_k, seg) for sl in rows]
    z = [(yn[n] + rk[n] * v_ref[rows[n], :].astype(f32)) * g_ref[rows[n], :].astype(f32) for n in gs]
    y_a = [_bdot(z[n], worw_ref[...]) for n in gs]
    y_b = [_bdot(cb_ref[sl, :], woc_ref[...]) for sl in rows]
    merged = [jax.nn.sigmoid(gt_ref[rows[n], :D].astype(f32)) * y_a[n]
              + jax.nn.sigmoid(gt_ref[rows[n], D:].astype(f32)) * y_b[n] for n in gs]
    hres = [x_ref[rows[n], :] + _bdot(merged[n], wo_ref[...]) for n in gs]
    x2 = [hres[n] * lax.rsqrt(jnp.mean(hres[n] * hres[n], axis=-1, keepdims=True) + NORM_EPS)
          * ln2_ref[...] for n in gs]
    wh, wm, _ = _split3(wr_ref[...])
    nt = functools.partial(lax.dot_general, dimension_numbers=(((1,), (1,)), ((), ())),
                           preferred_element_type=f32)
    for n, sl in enumerate(rows):
        h_out[sl, :] = hres[n]
        x2_out[sl, :] = x2[n].astype(x2_out.dtype)
        xh, xm, _ = _split3(x2[n])
        lg_out[:, sl] = nt(wh, xh) + nt(wh, xm) + nt(wm, xh) + br_ref[...]


def _merge(y, r, k, v, g, cb, gt, x2d, vec, ln2, seg, worw, woc, wo, wr_t, br, tm):
    t = y.shape[0]

    def rows(n):
        return pl.BlockSpec((tm, n), lambda i: (i, 0))

    def whole(shape):
        return pl.BlockSpec(shape, lambda i: (0,) * len(shape))

    return pl.pallas_call(
        _merge_kernel,
        name="merge",
        grid=(t // tm,),
        in_specs=[rows(R)] * 6 + [rows(2 * D), rows(D), whole((8, R)), whole((1, D)),
                                   whole((R, R)), whole((R, D)), whole((R, D)), whole((D, D)),
                                   whole((NE, D)), whole((NE, 1))],
        out_specs=[rows(D), rows(D), pl.BlockSpec((NE, tm), lambda i: (0, i))],
        out_shape=[jax.ShapeDtypeStruct((t, D), f32), jax.ShapeDtypeStruct((t, D), bf16),
                   jax.ShapeDtypeStruct((NE, t), f32)],
        compiler_params=_cparams(("parallel",)),
    )(y, r, k, v, g, cb, gt, x2d, vec, ln2, seg, worw, woc, wo, wr_t, br)


def _route_kernel(lg_ref, e_out, w_out, pos_out, cnt_out):
    lg = lg_ref[...]
    tr = lg.shape[1]
    erow = lax.broadcasted_iota(jnp.int32, lg.shape, 0)
    work = lg
    hits, vals, idxs = [], [], []
    for _ in range(TOPK):
        m = jnp.max(work, axis=0, keepdims=True)
        idx = jnp.min(jnp.where(work == m, erow, NE), axis=0, keepdims=True)
        hit = erow == idx
        hits.append(hit)
        vals.append(m)
        idxs.append(idx)
        work = jnp.where(hit, -jnp.inf, work)
    ex = [jnp.exp(vk - vals[0]) for vk in vals]
    den = ex[0] + ex[1] + ex[2] + ex[3]
    multi = jnp.where(hits[0] | hits[1] | hits[2] | hits[3], 1.0, 0.0)
    ti = lax.broadcasted_iota(jnp.int32, (tr, tr), 0)
    tj = lax.broadcasted_iota(jnp.int32, (tr, tr), 1)
    before = jnp.where(ti < tj, 1.0, 0.0).astype(bf16)
    excl = jnp.dot(multi.astype(bf16), before, preferred_element_type=f32)
    total = jnp.sum(multi, axis=1, keepdims=True)
    units = jnp.floor((total + (ALIGN - 1)) * (1.0 / ALIGN))
    ei = lax.broadcasted_iota(jnp.int32, (NE, NE), 0)
    ej = lax.broadcasted_iota(jnp.int32, (NE, NE), 1)
    below = jnp.where(ej < ei, 1.0, 0.0).astype(bf16)
    off = ALIGN * jnp.dot(below, jnp.broadcast_to(units, (NE, 128)).astype(bf16),
                          preferred_element_type=f32)[:, 0:1]
    for kk in range(TOPK):
        e_out[kk:kk + 1, :] = idxs[kk]
        w_out[kk:kk + 1, :] = ex[kk] / den
        pos_out[kk:kk + 1, :] = jnp.sum(jnp.where(hits[kk], excl + off, 0.0), axis=0,
                                        keepdims=True).astype(jnp.int32)
    cnt_out[...] = jnp.broadcast_to(total, cnt_out.shape).astype(jnp.int32)


def _route(logits_t):
    t = logits_t.shape[1]
    sel = pl.BlockSpec((TOPK, TT), lambda i: (0, i))
    return pl.pallas_call(
        _route_kernel,
        name="route",
        grid=(t // TT,),
        in_specs=[pl.BlockSpec((NE, TT), lambda i: (0, i))],
        out_specs=[sel, sel, sel, pl.BlockSpec((None, NE, 128), lambda i: (i, 0, 0))],
        out_shape=[jax.ShapeDtypeStruct((TOPK, t), jnp.int32),
                   jax.ShapeDtypeStruct((TOPK, t), f32),
                   jax.ShapeDtypeStruct((TOPK, t), jnp.int32),
                   jax.ShapeDtypeStruct((t // TT, NE, 128), jnp.int32)],
        compiler_params=_cparams(("parallel",)),
    )(logits_t)


def _run_copies(tile, cnt_ref, make_copy, act):
    def per_expert(e, carry):
        cnt = pl.multiple_of(cnt_ref[tile * NE + e], ALIGN)

        @pl.when(cnt > 0)
        def _():
            act(make_copy(e, cnt))
        return carry
    lax.fori_loop(0, NE, per_expert, 0)


def _dispatch_kernel(cnt_ref, run_ref, off_ref, tot_ref, tail_ref, pos_ref, x2_ref, xs_ref,
                     sbuf, zbuf, sem, zsem):
    i = pl.program_id(0)
    last = pl.num_programs(0) - 1
    cur = i % 2

    @pl.when(i == 0)
    def _():
        zbuf[...] = jnp.zeros_like(zbuf)

        def zero_copy(start):
            return pltpu.make_async_copy(zbuf, xs_ref.at[pl.ds(pl.multiple_of(start, EBLK), EBLK)], zsem)

        for e in range(NE):
            @pl.when(tail_ref[e] >= 0)
            def _():
                zero_copy(tail_ref[e]).start()
        for e in range(NE):
            @pl.when(tail_ref[e] >= 0)
            def _():
                zero_copy(tail_ref[e]).wait()
        n_blk = xs_ref.shape[0] // EBLK
        lax.fori_loop(tail_ref[NE], n_blk, lambda blk, c: (zero_copy(blk * EBLK).start(), c)[1], 0)
        lax.fori_loop(tail_ref[NE], n_blk, lambda blk, c: (zero_copy(blk * EBLK).wait(), c)[1], 0)

    def slab(tile, half):
        def make(e, rows):
            src = pl.multiple_of(off_ref[tile * NE + e], ALIGN)
            dst = pl.multiple_of(run_ref[tile * NE + e], ALIGN)
            return pltpu.make_async_copy(sbuf.at[half, pl.ds(src, rows)],
                                         xs_ref.at[pl.ds(dst, rows)], sem.at[half])
        return make

    def drain(tile, half):
        n = pl.multiple_of(tot_ref[tile], ALIGN)
        pltpu.make_async_copy(sbuf.at[half, pl.ds(0, n)], xs_ref.at[pl.ds(0, n)], sem.at[half]).wait()

    @pl.when(i >= 2)
    def _():
        drain(i - 2, cur)

    pos = pos_ref[...]
    xt = x2_ref[...]
    for c in range(0, SB, DCHUNK):
        slot = lax.broadcasted_iota(jnp.int32, (DCHUNK, TT), 0) + c
        place = jnp.zeros((DCHUNK, TT), f32)
        for kk in range(TOPK):
            place = jnp.where(slot == pos[kk:kk + 1, :], 1.0, place)
        sbuf[cur, c:c + DCHUNK] = _pack_pairs(jnp.dot(place.astype(bf16), xt,
                                                      preferred_element_type=f32))
    _run_copies(i, cnt_ref, slab(i, cur), lambda cp: cp.start())

    @pl.when(i == last)
    def _():
        @pl.when(i >= 1)
        def _():
            drain(i - 1, 1 - cur)
        drain(i, cur)


def _dispatch(cnt16, run_start, off16, tot, tail, pos, x2, n_rows):
    t = x2.shape[0]
    return pl.pallas_call(
        _dispatch_kernel,
        name="dispatch",
        grid_spec=pltpu.PrefetchScalarGridSpec(
            num_scalar_prefetch=5, grid=(t // TT,),
            in_specs=[pl.BlockSpec((TOPK, TT), lambda i, *_: (0, i)),
                      pl.BlockSpec((TT, D), lambda i, *_: (i, 0))],
            out_specs=pl.BlockSpec(memory_space=pl.ANY),
            scratch_shapes=[pltpu.VMEM((2, SB, DH), jnp.uint32), pltpu.VMEM((EBLK, DH), jnp.uint32),
                            pltpu.SemaphoreType.DMA((2,)), pltpu.SemaphoreType.DMA(())]),
        out_shape=jax.ShapeDtypeStruct((n_rows, DH), jnp.uint32),
        compiler_params=_cparams(("arbitrary",)),
    )(cnt16, run_start, off16, tot, tail, pos, x2)


def _expert_kernel(be_ref, nv_ref, fx_ref, nx_ref, xs_ref, wg_ref, bg_ref, wu_ref, bu_ref, wd_ref, bd_ref,
                   ys_ref, wf32, wgb, wub, wdb, wsem):
    i = pl.program_id(0)

    def fetch(e):
        return [pltpu.make_async_copy(w.at[e], wf32.at[j], wsem)
                for j, w in enumerate((wg_ref, wu_ref, wd_ref))]

    @pl.when(fx_ref[i] == 1)
    def _():
        @pl.when(i == 0)
        def _():
            for cp in fetch(be_ref[0]):
                cp.start()
        for cp in fetch(be_ref[i]):
            cp.wait()
        wgb[...] = wf32[0].astype(bf16)
        wub[...] = wf32[1].astype(bf16)
        wdb[...] = wf32[2].astype(bf16)

        @pl.when(nx_ref[i] >= 0)
        def _():
            for cp in fetch(nx_ref[i]):
                cp.start()

    def swiglu(n):
        xlo, xhi = _unpack_pairs(xs_ref[0:n, :])
        dot = functools.partial(jnp.dot, preferred_element_type=f32)
        gt = jnp.minimum(dot(xlo, wgb[:DH, :]) + dot(xhi, wgb[DH:, :]) + bg_ref[...], LIMIT)
        up = jnp.clip(dot(xlo, wub[:DH, :]) + dot(xhi, wub[DH:, :]) + bu_ref[...], -LIMIT, LIMIT)
        act = (up + 1.0) * (gt * jax.nn.sigmoid(gt * ALPHA))
        y = dot(act.astype(bf16), wdb[...]) + bd_ref[...]
        ys_ref[0:n, :] = _pack_pairs(y.astype(bf16).astype(f32))

    nv = nv_ref[i]

    @pl.when(nv > ESUB)
    def _():
        swiglu(EBLK)

    @pl.when((nv > 0) & (nv <= ESUB))
    def _():
        swiglu(ESUB)
        ys_ref[ESUB:, :] = jnp.zeros((EBLK - ESUB, DH), jnp.uint32)

    @pl.when(nv == 0)
    def _():
        ys_ref[...] = jnp.zeros_like(ys_ref)


def _experts(blk_e, nvalid, first, nxt_e, xs, wg, bg, wu, bu, wd, bd):
    p = xs.shape[0]
    bspec = pl.BlockSpec((None, 1, D), lambda i, be, nv, fx, nx: (be[i], 0, 0))
    xrows = pl.BlockSpec((EBLK, DH), lambda i, be, nv, fx, nx: (jnp.where(nv[i] > 0, i, 0), 0))
    yrows = pl.BlockSpec((EBLK, DH), lambda i, be, nv, fx, nx: (i, 0))
    hbm = pl.BlockSpec(memory_space=pl.ANY)
    return pl.pallas_call(
        _expert_kernel,
        name="experts",
        grid_spec=pltpu.PrefetchScalarGridSpec(
            num_scalar_prefetch=4, grid=(p // EBLK,),
            in_specs=[xrows, hbm, bspec, hbm, bspec, hbm, bspec],
            out_specs=yrows,
            scratch_shapes=[pltpu.VMEM((3, D, D), f32)] + [pltpu.VMEM((D, D), bf16)] * 3
                           + [pltpu.SemaphoreType.DMA(())]),
        out_shape=jax.ShapeDtypeStruct((p, DH), jnp.uint32),
        compiler_params=_cparams(("arbitrary",)),
    )(blk_e, nvalid, first, nxt_e, xs, wg, bg, wu, bu, wd, bd)


def _combine_kernel(cnt_ref, run_ref, off_ref, tot_ref, ys_ref, pos_ref, w_ref, h_ref, g_ref, o_ref,
                    rbuf, sem):
    i = pl.program_id(0)
    cur = i % 2

    def fetch(tile, half):
        def slab(e, rows):
            src = pl.multiple_of(run_ref[tile * NE + e], ALIGN)
            dst = pl.multiple_of(off_ref[tile * NE + e], ALIGN)
            return pltpu.make_async_copy(ys_ref.at[pl.ds(src, rows)],
                                         rbuf.at[half, pl.ds(dst, rows)], sem.at[half])
        return slab

    @pl.when(i == 0)
    def _():
        rbuf[...] = jnp.zeros_like(rbuf)
        _run_copies(0, cnt_ref, fetch(0, 0), lambda cp: cp.start())

    @pl.when(i + 1 < pl.num_programs(0))
    def _():
        _run_copies(i + 1, cnt_ref, fetch(i + 1, 1 - cur), lambda cp: cp.start())

    n = pl.multiple_of(tot_ref[i], ALIGN)
    pltpu.make_async_copy(ys_ref.at[pl.ds(0, n)], rbuf.at[cur, pl.ds(0, n)], sem.at[cur]).wait()

    pos = pos_ref[...]
    w = w_ref[...]
    moe_lo = jnp.zeros((TT, DH), f32)
    moe_hi = jnp.zeros((TT, DH), f32)
    for c in range(0, SB, SCHUNK):
        slot = lax.broadcasted_iota(jnp.int32, (TT, SCHUNK), 1) + c
        gate = jnp.zeros((TT, SCHUNK), f32)
        for kk in range(TOPK):
            gate = jnp.where(slot == pos[:, kk:kk + 1], w[:, kk:kk + 1], gate)
        gb = gate.astype(bf16)
        lo, hi = _unpack_pairs(rbuf[cur, c:c + SCHUNK])
        moe_lo = moe_lo + jnp.dot(gb, lo, preferred_element_type=f32)
        moe_hi = moe_hi + jnp.dot(gb, hi, preferred_element_type=f32)
    acc = h_ref[...] + jnp.concatenate([moe_lo, moe_hi], axis=1)
    o_ref[...] = acc * lax.rsqrt(jnp.mean(acc * acc, axis=-1, keepdims=True) + NORM_EPS) * g_ref[...]


def _combine(cnt16, run_start, off16, tot, ys, pos_t, gates_t, hres, lnf):
    t = hres.shape[0]
    return pl.pallas_call(
        _combine_kernel,
        name="combine",
        grid_spec=pltpu.PrefetchScalarGridSpec(
            num_scalar_prefetch=4, grid=(t // TT,),
            in_specs=[pl.BlockSpec(memory_space=pl.ANY),
                      pl.BlockSpec((TT, TOPK), lambda i, *_: (i, 0)),
                      pl.BlockSpec((TT, TOPK), lambda i, *_: (i, 0)),
                      pl.BlockSpec((TT, D), lambda i, *_: (i, 0)),
                      pl.BlockSpec((1, D), lambda i, *_: (0, 0))],
            out_specs=pl.BlockSpec((TT, D), lambda i, *_: (i, 0)),
            scratch_shapes=[pltpu.VMEM((2, SB, DH), jnp.uint32), pltpu.SemaphoreType.DMA((2,))]),
        out_shape=jax.ShapeDtypeStruct((t, D), f32),
        compiler_params=_cparams(("arbitrary",)),
    )(cnt16, run_start, off16, tot, ys, pos_t, gates_t, hres, lnf)


def _row_tile(n, want):
    t = min(n, want)
    assert n % t == 0
    return t


def kernel(x, meta_tokens, ln1_g, w_in, mu_r, mu_k, mu_v, mu_w, mu_a, mu_g, w0, w_w1, w_w2, a0, w_a1, w_a2, w_g1, w_g2, k_k, k_a, r_k, lnx_g, lnx_b, w_o_rwkv, conv_w, w_o_conv, w_o, ln2_g, w_router, b_router, w_e_gate, b_e_gate, w_e_up, b_e_up, w_e_down, b_e_down, lnf_g):
    nb, seq, _ = x.shape
    t = nb * seq
    assert ln1_g.shape[0] == 1, "single layer"

    muw, mua, mug = mu_w[0][:, None], mu_a[0][:, None], mu_g[0][:, None]
    lora_cur = jnp.concatenate([(1 - muw) * w_w1[0], (1 - mua) * w_a1[0], (1 - mug) * w_g1[0]], axis=1)
    lora_prev = jnp.concatenate([muw * w_w1[0], mua * w_a1[0], mug * w_g1[0]], axis=1)
    zrow = jnp.zeros((1, R), f32)
    p = {
        "ln1": ln1_g[0][None, :],
        "w_in": w_in[0].astype(bf16),
        "w_lora": jnp.concatenate([lora_cur, lora_prev], axis=1).astype(bf16),
        "vec_mix": jnp.concatenate([mu_r, mu_k, mu_v, w0, a0, k_k, k_a, zrow], axis=0),
        "ww2": w_w2[0].astype(bf16), "wa2": w_a2[0].astype(bf16), "wg2": w_g2[0].astype(bf16),
        "cw": jnp.concatenate([conv_w[0], jnp.zeros((5, R), f32)], axis=0),
        "seg": (jnp.arange(R)[:, None] // HS == jnp.arange(R)[None, :] // HS).astype(bf16),
    }
    vec_merge = jnp.concatenate([lnx_g, lnx_b, r_k, jnp.zeros((5, R), f32)], axis=0)

    meta_out = _projmix(meta_tokens.astype(f32), NMETA, jnp.zeros((MARGIN, PW), f32), p, NMETA)
    pad = lambda z: jnp.pad(z, ((CHUNK - NMETA, 0), (0, 0)))[None]
    r_m, k_m, v_m, kk_m, b_m, lw_m = (pad(z) for z in meta_out[1:7])
    _, s_meta = _wkv(r_m, k_m, v_m, kk_m, b_m, lw_m, jnp.zeros((NH // 2, 2 * HS, 2 * HS), f32), 1)

    x2d = x.reshape(t, D)
    gt, r, k, v, kk, b, lw, g, cb, _ = _projmix(x2d, seq, meta_out[9], p, _row_tile(seq, 512))
    as3 = lambda z: z.reshape(nb, seq, R)
    y, _ = _wkv(as3(r), as3(k), as3(v), as3(kk), as3(b), as3(lw), s_meta[0], WKV_SEQS)
    hres, x2, logits_t = _merge(
        y.reshape(t, R), r, k, v, g, cb, gt, x2d, vec_merge, ln2_g[0][None, :], p["seg"],
        w_o_rwkv[0].astype(bf16), w_o_conv[0].astype(bf16), w_o[0].astype(bf16),
        w_router[0].T, b_router[0][:, None], _row_tile(t, 512))

    assert t % TT == 0
    nt = t // TT
    _, gates, pos, cnt = _route(logits_t)
    cnt = cnt[:, :, 0]
    cnt16 = ((cnt + ALIGN - 1) // ALIGN) * ALIGN
    base16 = jnp.cumsum(cnt16, axis=0) - cnt16
    tot16 = jnp.sum(cnt16, axis=0)
    padded = ((tot16 + EBLK - 1) // EBLK) * EBLK
    pend = jnp.cumsum(padded)
    pstart = pend - padded
    run_start = pstart[None, :] + base16
    off16 = jnp.cumsum(cnt16, axis=1) - cnt16
    n_rows = -(-(t * TOPK + nt * NE * (ALIGN - 1) + NE * EBLK) // EBLK) * EBLK
    n_blk = n_rows // EBLK
    blk_start = jnp.arange(n_blk, dtype=jnp.int32) * EBLK
    blk_e = jnp.minimum(jnp.sum(pend[None, :] <= blk_start[:, None], axis=1), NE - 1).astype(jnp.int32)
    n_used = (pend[NE - 1:] // EBLK).astype(jnp.int32)
    tail = jnp.concatenate([jnp.where(padded > 0, pend - EBLK, -1), n_used]).astype(jnp.int32)
    flat = lambda z: z.reshape(-1).astype(jnp.int32)
    tot = jnp.sum(cnt16, axis=1).astype(jnp.int32)
    xs = _dispatch(flat(cnt16), flat(run_start), flat(off16), tot, tail, pos, x2, n_rows)
    used = blk_start < pend[NE - 1]
    first = jnp.concatenate([jnp.ones((1,), bool), blk_e[1:] != blk_e[:-1]]) & used
    eidx = jnp.arange(NE, dtype=jnp.int32)
    later = (eidx[None, :] > eidx[:, None]) & (padded[None, :] > 0)
    next_of = jnp.where(jnp.any(later, axis=1), jnp.argmax(later, axis=1), -1).astype(jnp.int32)
    mine = blk_e[:, None] == eidx[None, :]
    nxt_e = jnp.sum(jnp.where(mine, next_of[None, :], 0), axis=1).astype(jnp.int32)
    run_end = jnp.sum(jnp.where(mine, (pstart + tot16)[None, :], 0), axis=1)
    nvalid = jnp.where(used, jnp.clip(run_end - blk_start, 0, EBLK), 0).astype(jnp.int32)
    ys = _experts(blk_e, nvalid, first.astype(jnp.int32), nxt_e, xs,
                  w_e_gate[0], b_e_gate[0][:, None, :],
                  w_e_up[0], b_e_up[0][:, None, :],
                  w_e_down[0], b_e_down[0][:, None, :])
    out = _combine(flat(cnt16), flat(run_start), flat(off16), tot, ys, pos.T, gates.T, hres, lnf_g[None, :])
    return out.reshape(nb, seq, D)
```

```python
import functools

import jax
import jax.numpy as jnp
from jax import lax
from jax.experimental import pallas as pl
from jax.experimental.pallas import tpu as pltpu

D = 1024
R = 512
NH = 8
HS = 64
NE = 32
TOPK = 4
NMETA = 16
CHUNK = 64
WKV_SEQS = 8
EBLK = 512
ESUB = 256
TT = 512
ALIGN = 8
SB = 4 * TT + 256
DH = D // 2
MERGE_SUB = 128
DCHUNK = 256
SCHUNK = 768
NORM_EPS = 1e-5
LNX_EPS = 64e-5
ALPHA = 1.702
LIMIT = 7.0
DECAY_SCALE = 0.6065306597126334
LORA_W, LORA_A, LORA_G = 64, 64, 128
LORA = LORA_W + LORA_A + LORA_G
NPROJ = 3 * R + 3 * R + 2 * D + 2 * LORA
VMEM_LIMIT = 56 * 1024 * 1024

f32 = jnp.float32
bf16 = jnp.bfloat16


def _bdot(a, b):
    return jnp.dot(a.astype(bf16), b.astype(bf16), preferred_element_type=f32)


def _bdot_nt(a, b):
    return lax.dot_general(a.astype(bf16), b.astype(bf16), (((1,), (1,)), ((), ())),
                           preferred_element_type=f32)


def _bdot_tn(a, b):
    return lax.dot_general(a.astype(bf16), b.astype(bf16), (((0,), (0,)), ((), ())),
                           preferred_element_type=f32)


def _split3(x):
    h = x.astype(bf16)
    r1 = x - h.astype(f32)
    m = r1.astype(bf16)
    l = (r1 - m.astype(f32)).astype(bf16)
    return h, m, l


def _head_sum(x, seg):
    return jnp.dot(x.astype(bf16), seg, preferred_element_type=f32)


def _pack_pairs(x):
    a = lax.bitcast_convert_type(x[:, :DH], jnp.uint32)
    b = lax.bitcast_convert_type(x[:, DH:], jnp.uint32)
    return (a & jnp.uint32(0xFFFF0000)) | (b >> 16)


def _unpack_pairs(p):
    lo = lax.bitcast_convert_type(p & jnp.uint32(0xFFFF0000), f32)
    hi = lax.bitcast_convert_type(p << 16, f32)
    return lo.astype(bf16), hi.astype(bf16)


def _cparams(sem):
    return pltpu.CompilerParams(dimension_semantics=sem, vmem_limit_bytes=VMEM_LIMIT)


MARGIN = 16
PW = 3 * R + 3 * R + 2 * LORA


def _mix_rows(pbuf, a, n, vec, cw, seg, ww2, wa2, wg2, outs, o):
    r_out, k_out, v_out, kk_out, b_out, lw_out, g_out, cb_out = outs
    mu_r, mu_k, mu_v = vec[0:1, :], vec[1:2, :], vec[2:3, :]
    w0, a0, k_k, k_a = vec[3:4, :], vec[4:5, :], vec[5:6, :], vec[6:7, :]
    cur = pbuf[a:a + n, :]
    prev = pbuf[a - 1:a - 1 + n, :]
    r = cur[:, :R] + (prev[:, :R] - cur[:, :R]) * mu_r
    k = cur[:, R:2 * R] + (prev[:, R:2 * R] - cur[:, R:2 * R]) * mu_k
    v = cur[:, 2 * R:3 * R] + (prev[:, 2 * R:3 * R] - cur[:, 2 * R:3 * R]) * mu_v
    lo = 6 * R
    mixed = cur[:, lo:lo + LORA] + prev[:, lo + LORA:lo + 2 * LORA]
    hw = jnp.tanh(mixed[:, :LORA_W])
    ha = mixed[:, LORA_W:LORA_W + LORA_A]
    hg = jax.nn.sigmoid(mixed[:, LORA_W + LORA_A:])
    lw = -DECAY_SCALE * jax.nn.sigmoid(w0 + _bdot(hw, ww2))
    aa = jax.nn.sigmoid(a0 + _bdot(ha, wa2))
    g = _bdot(hg, wg2)
    kk = k * k_k
    kk = kk * lax.rsqrt(jnp.maximum(_head_sum(kk * kk, seg), 1e-24))
    k = k * (1.0 + (aa - 1.0) * k_a)
    rows = slice(o, o + n)
    r_out[rows, :] = r.astype(r_out.dtype)
    k_out[rows, :] = k.astype(k_out.dtype)
    v_out[rows, :] = v.astype(v_out.dtype)
    kk_out[rows, :] = kk.astype(kk_out.dtype)
    b_out[rows, :] = (kk * aa).astype(b_out.dtype)
    lw_out[rows, :] = lw
    g_out[rows, :] = g.astype(g_out.dtype)
    prev2 = pbuf[a - 2:a - 2 + n, 4 * R:6 * R]
    u0 = cur[:, 4 * R:5 * R] * cur[:, 5 * R:6 * R]
    u1 = prev[:, 4 * R:5 * R] * prev[:, 5 * R:6 * R]
    u2 = prev2[:, :R] * prev2[:, R:]
    conv = cw[2:3, :] * u0 + cw[1:2, :] * u1 + cw[0:1, :] * u2
    cb_out[rows, :] = (cur[:, 3 * R:4 * R] * conv).astype(cb_out.dtype)


def _projmix_kernel(tiles_per_seq, group, x_ref, g_ref, w_ref, wl_ref, init_ref, vec_ref, ww2_ref, wa2_ref,
                    wg2_ref, cw_ref, seg_ref, gt_out, r_out, k_out, v_out, kk_out, b_out, lw_out,
                    g_out, cb_out, tail_out, pbuf_a, pbuf_b):
    i = pl.program_id(0)
    tm = x_ref.shape[0]

    @pl.when(i == 0)
    def _():
        pbuf_a[...] = jnp.zeros_like(pbuf_a)
        pbuf_b[...] = jnp.zeros_like(pbuf_b)

    def step(pcur, pprv):
        first = (i % tiles_per_seq) == 0
        pcur[0:MARGIN, :] = jnp.where(first, init_ref[...], pprv[tm:tm + MARGIN, :])
        tail_out[...] = pprv[tm:tm + MARGIN, :]

        x = x_ref[...]
        xb = (x * lax.rsqrt(jnp.mean(x * x, axis=-1, keepdims=True) + NORM_EPS)
              * g_ref[...]).astype(bf16)
        outs = (r_out, k_out, v_out, kk_out, b_out, lw_out, g_out, cb_out)
        vec, cw, seg = vec_ref[...], cw_ref[...], seg_ref[...]
        ww2, wa2, wg2 = ww2_ref[...], wa2_ref[...], wg2_ref[...]

        def project(c):
            if c < 6 * R:
                wc = w_ref[:, c:c + 512]
            elif c < PW:
                wc = wl_ref[:, c - 6 * R:c - 6 * R + 512]
            else:
                wc = w_ref[:, c - 2 * LORA:c - 2 * LORA + 512]
            res = jnp.dot(xb, wc, preferred_element_type=f32)
            if c < PW:
                pcur[MARGIN:MARGIN + tm, c:c + 512] = res
            else:
                gt_out[:, c - PW:c - PW + 512] = res.astype(gt_out.dtype)
            return res[0:8, :] * 0.0

        cols = list(range(0, NPROJ, 512))
        groups = list(range(0, tm, group))
        per = -(-len(cols) // len(groups))
        tie = jnp.zeros((8, R), f32)
        for n, o in enumerate(groups):
            for c in cols[n * per:(n + 1) * per]:
                tie = project(c)
            _mix_rows(pprv, MARGIN + o, group, vec + tie, cw + tie, seg, ww2, wa2, wg2, outs, o)

    @pl.when(i % 2 == 0)
    def _():
        step(pbuf_a, pbuf_b)

    @pl.when(i % 2 == 1)
    def _():
        step(pbuf_b, pbuf_a)


def _projmix(x2d, seq_len, init, p, tm):
    t = x2d.shape[0]
    n = t // tm
    group = min(tm, 128)

    def whole(shape):
        return pl.BlockSpec(shape, lambda i: (0,) * len(shape))

    lag = pl.BlockSpec((tm, R), lambda i: (jnp.maximum(i - 1, 0), 0))
    return pl.pallas_call(
        functools.partial(_projmix_kernel, seq_len // tm, group),
        name="projmix",
        grid=(n + 1,),
        in_specs=[pl.BlockSpec((tm, D), lambda i: (jnp.minimum(i, n - 1), 0)),
                  whole((1, D)),
                  pl.BlockSpec((D, NPROJ - 2 * LORA), lambda i: (0, 0), pipeline_mode=pl.Buffered(1)),
                  pl.BlockSpec((D, 2 * LORA), lambda i: (0, 0), pipeline_mode=pl.Buffered(1)),
                  whole((MARGIN, PW)), whole((8, R)), whole((LORA_W, R)), whole((LORA_A, R)),
                  whole((LORA_G, R)), whole((8, R)), whole((R, R))],
        out_specs=[pl.BlockSpec((tm, 2 * D), lambda i: (jnp.minimum(i, n - 1), 0))] + [lag] * 8
                  + [whole((MARGIN, PW))],
        out_shape=[jax.ShapeDtypeStruct((t, 2 * D), bf16)]
                  + [jax.ShapeDtypeStruct((t, R), f32 if m == 5 else bf16) for m in range(8)]
                  + [jax.ShapeDtypeStruct((MARGIN, PW), f32)],
        scratch_shapes=[pltpu.VMEM((MARGIN + tm, PW), f32)] * 2,
        compiler_params=_cparams(("arbitrary",)),
    )(x2d, p["ln1"], p["w_in"], p["w_lora"], init, p["vec_mix"], p["ww2"], p["wa2"], p["wg2"], p["cw"], p["seg"])


def _wkv_kernel(r_ref, k_ref, v_ref, kk_ref, b_ref, lw_ref, s0_ref, y_ref, sT_ref, s_scr):
    c = pl.program_id(1)
    nbw = r_ref.shape[0]

    @pl.when(c == 0)
    def _():
        for j in range(nbw):
            s_scr[j] = s0_ref[...]

    C = CHUNK
    W2 = 2 * HS
    ri = lax.broadcasted_iota(jnp.int32, (C, C), 0)
    ci = lax.broadcasted_iota(jnp.int32, (C, C), 1)
    tri = jnp.where(ri >= ci, 1.0, 0.0).astype(bf16)
    rp = lax.broadcasted_iota(jnp.int32, (C, W2), 0)
    cp = lax.broadcasted_iota(jnp.int32, (C, W2), 1)
    lo = cp < HS
    lo1 = lax.broadcasted_iota(jnp.int32, (1, W2), 1) < HS
    sp = jnp.where(lo, cp, cp - HS)
    strict = rp > sp
    incl = rp >= sp
    eye = jnp.where(rp == sp, 1.0, 0.0).astype(f32)
    rq = lax.broadcasted_iota(jnp.int32, (W2, W2), 0)
    cq = lax.broadcasted_iota(jnp.int32, (W2, W2), 1)
    diag_blocks = (rq < HS) == (cq < HS)
    dot = functools.partial(jnp.dot, preferred_element_type=f32)

    def first(z):
        return jnp.where(lo1, z, jnp.zeros_like(z))

    def second(z):
        return jnp.where(lo1, jnp.zeros_like(z), z)

    def bdiag(z):
        return jnp.concatenate([first(z), second(z)], axis=0)

    def adiag(z):
        return jnp.concatenate([second(z), first(z)], axis=0)

    at, rt, bt, kt, bh, kh, vv, g_tot = [], [], [], [], [], [], [], []
    for j in range(nbw):
        lw = lw_ref[j]
        h, m, l = _split3(lw)
        cum = dot(tri, h) + dot(tri, m) + dot(tri, l)
        tot = cum[C - 1:C, :]
        k = k_ref[j].astype(f32)
        b = b_ref[j].astype(f32)
        g_inv = jnp.exp(-cum)
        tail = jnp.exp(tot - cum)
        rt.append(r_ref[j].astype(f32) * jnp.exp(cum))
        kt.append(k * g_inv)
        bt.append(b * g_inv)
        at.append(-kk_ref[j].astype(f32) * jnp.exp(cum - lw))
        bh.append(b * tail)
        kh.append(k * tail)
        vv.append(v_ref[j].astype(f32))
        g_tot.append(jnp.exp(tot))

    chains = [(j, p) for j in range(nbw) for p in range(NH // 2)]
    cs = range(len(chains))
    sl = lambda p: slice(p * W2, (p + 1) * W2)
    s_old = [s_scr[j, p] for j, p in chains]
    lhs = [jnp.concatenate([at[j][:, sl(p)], rt[j][:, sl(p)]], axis=0) for j, p in chains]
    rhs_n = [jnp.concatenate([bt[j][:, sl(p)], kt[j][:, sl(p)]], axis=0).astype(bf16) for j, p in chains]
    rhs_s = [jnp.concatenate([kt[j][:, sl(p)], bt[j][:, sl(p)]], axis=0).astype(bf16) for j, p in chains]
    lhs_b = [x.astype(bf16) for x in lhs]
    am1 = [_bdot_nt(first(lhs[n]), rhs_n[n]) for n in cs]
    am2 = [_bdot_nt(second(lhs[n]), rhs_s[n]) for n in cs]
    a_ab = [jnp.where(strict, jnp.where(lo, am1[n][:C], am2[n][:C]), 0.0) for n in cs]
    a_ak = [jnp.where(strict, jnp.where(lo, am2[n][:C], am1[n][:C]), 0.0) for n in cs]
    a_r1 = [jnp.where(incl, am1[n][C:], 0.0) for n in cs]
    a_r2 = [jnp.where(incl, am2[n][C:], 0.0) for n in cs]
    pb = [a_ab[n].astype(bf16) for n in cs]
    pw = [dot(pb[n], bdiag(pb[n])) for n in cs]
    x = [eye + a_ab[n] for n in cs]
    for _ in range(4):
        pb = [pw[n].astype(bf16) for n in cs]
        both = [dot(jnp.concatenate([pb[n], x[n].astype(bf16)], axis=0), bdiag(pb[n])) for n in cs]
        pw = [both[n][:C] for n in cs]
        x = [x[n] + both[n][C:] for n in cs]
    x = [x[n] + dot(x[n].astype(bf16), bdiag(pw[n].astype(bf16))) for n in cs]
    vp = [vv[j][:, sl(p)] for j, p in chains]
    sh = [_bdot_nt(lhs_b[n], s_old[n]) for n in cs]
    av = [dot(a_ak[n].astype(bf16), adiag(vp[n].astype(bf16))) for n in cs]
    u = [dot(x[n].astype(bf16), bdiag((sh[n][:C] + av[n]).astype(bf16))) for n in cs]
    ub = [u[n].astype(bf16) for n in cs]
    vb = [vp[n].astype(bf16) for n in cs]
    mix4 = [jnp.concatenate([first(ub[n]), first(vb[n]), second(vb[n]), second(ub[n])], axis=0)
            for n in cs]
    ys = [sh[n][C:] + dot(jnp.concatenate([a_r1[n], a_r2[n]], axis=1).astype(bf16), mix4[n])
          for n in cs]
    uv = [jnp.concatenate([ub[n], vb[n]], axis=0) for n in cs]
    bk = [jnp.concatenate([bh[j][:, sl(p)], kh[j][:, sl(p)]], axis=0) for j, p in chains]
    s_new = [s_old[n] * g_tot[j][:, sl(p)] + jnp.where(diag_blocks, _bdot_tn(uv[n], bk[n]), 0.0)
             for n, (j, p) in enumerate(chains)]
    for n, (j, p) in enumerate(chains):
        s_scr[j, p] = s_new[n]
        y_ref[j, :, sl(p)] = ys[n]

    @pl.when(c == pl.num_programs(1) - 1)
    def _():
        sT_ref[...] = s_scr[...]


def _wkv(r, k, v, kk, b, lw, s0, nbw):
    nb, seq, _ = r.shape
    assert nb % nbw == 0 and seq % CHUNK == 0
    blk = pl.BlockSpec((nbw, CHUNK, R), lambda i, c: (i, c, 0))
    st = (NH // 2, 2 * HS, 2 * HS)
    return pl.pallas_call(
        _wkv_kernel,
        name="wkv",
        grid=(nb // nbw, seq // CHUNK),
        in_specs=[blk] * 6 + [pl.BlockSpec(st, lambda i, c: (0, 0, 0))],
        out_specs=[blk, pl.BlockSpec((nbw,) + st, lambda i, c: (i, 0, 0, 0))],
        out_shape=[jax.ShapeDtypeStruct((nb, seq, R), f32),
                   jax.ShapeDtypeStruct((nb,) + st, f32)],
        scratch_shapes=[pltpu.VMEM((nbw,) + st, f32)],
        compiler_params=_cparams(("parallel", "arbitrary")),
    )(r, k, v, kk, b, lw, s0)


def _merge_kernel(y_ref, r_ref, k_ref, v_ref, g_ref, cb_ref, gt_ref, x_ref, vec_ref, ln2_ref,
                  seg_ref, worw_ref, woc_ref, wo_ref, wr_ref, br_ref,
                  h_out, x2_out, lg_out):
    seg = seg_ref[...]
    vec = vec_ref[...]
    lnx_g, lnx_b, r_k = vec[0:1, :], vec[1:2, :], vec[2:3, :]
    tm = y_ref.shape[0]
    rows = [slice(s0, s0 + MERGE_SUB) for s0 in range(0, tm, MERGE_SUB)]
    gs = range(len(rows))
    y = [y_ref[sl, :] for sl in rows]
    mean = [_head_sum(y[n], seg) * (1.0 / HS) for n in gs]
    yc = [y[n] - mean[n] for n in gs]
    var = [_head_sum(yc[n] * yc[n], seg) * (1.0 / HS) for n in gs]
    yn = [yc[n] * lax.rsqrt(var[n] + LNX_EPS) * lnx_g + lnx_b for n in gs]
    rk = [_head_sum(r_ref[sl, :].astype(f32) * k_ref[sl, :].astype(f32) * r_k, seg) for sl in rows]
    z = [(yn[n] + rk[n] * v_ref[rows[n], :].astype(f32)) * g_ref[rows[n], :].astype(f32) for n in gs]
    y_a = [_bdot(z[n], worw_ref[...]) for n in gs]
    y_b = [_bdot(cb_ref[sl, :], woc_ref[...]) for sl in rows]
    merged = [jax.nn.sigmoid(gt_ref[rows[n], :D].astype(f32)) * y_a[n]
              + jax.nn.sigmoid(gt_ref[rows[n], D:].astype(f32)) * y_b[n] for n in gs]
    hres = [x_ref[rows[n], :] + _bdot(merged[n], wo_ref[...]) for n in gs]
    x2 = [hres[n] * lax.rsqrt(jnp.mean(hres[n] * hres[n], axis=-1, keepdims=True) + NORM_EPS)
          * ln2_ref[...] for n in gs]
    wh, wm, _ = _split3(wr_ref[...])
    nt = functools.partial(lax.dot_general, dimension_numbers=(((1,), (1,)), ((), ())),
                           preferred_element_type=f32)
    for n, sl in enumerate(rows):
        h_out[sl, :] = hres[n]
        x2_out[sl, :] = x2[n].astype(x2_out.dtype)
        xh, xm, _ = _split3(x2[n])
        lg_out[:, sl] = nt(wh, xh) + nt(wh, xm) + nt(wm, xh) + br_ref[...]


def _merge(y, r, k, v, g, cb, gt, x2d, vec, ln2, seg, worw, woc, wo, wr_t, br, tm):
    t = y.shape[0]

    def rows(n):
        return pl.BlockSpec((tm, n), lambda i: (i, 0))

    def whole(shape):
        return pl.BlockSpec(shape, lambda i: (0,) * len(shape))

    return pl.pallas_call(
        _merge_kernel,
        name="merge",
        grid=(t // tm,),
        in_specs=[rows(R)] * 6 + [rows(2 * D), rows(D), whole((8, R)), whole((1, D)),
                                   whole((R, R)), whole((R, D)), whole((R, D)), whole((D, D)),
                                   whole((NE, D)), whole((NE, 1))],
        out_specs=[rows(D), rows(D), pl.BlockSpec((NE, tm), lambda i: (0, i))],
        out_shape=[jax.ShapeDtypeStruct((t, D), f32), jax.ShapeDtypeStruct((t, D), bf16),
                   jax.ShapeDtypeStruct((NE, t), f32)],
        compiler_params=_cparams(("parallel",)),
    )(y, r, k, v, g, cb, gt, x2d, vec, ln2, seg, worw, woc, wo, wr_t, br)


def _route_kernel(lg_ref, e_out, w_out, pos_out, cnt_out):
    lg = lg_ref[...]
    tr = lg.shape[1]
    erow = lax.broadcasted_iota(jnp.int32, lg.shape, 0)
    work = lg
    hits, vals, idxs = [], [], []
    for _ in range(TOPK):
        m = jnp.max(work, axis=0, keepdims=True)
        idx = jnp.min(jnp.where(work == m, erow, NE), axis=0, keepdims=True)
        hit = erow == idx
        hits.append(hit)
        vals.append(m)
        idxs.append(idx)
        work = jnp.where(hit, -jnp.inf, work)
    ex = [jnp.exp(vk - vals[0]) for vk in vals]
    den = ex[0] + ex[1] + ex[2] + ex[3]
    multi = jnp.where(hits[0] | hits[1] | hits[2] | hits[3], 1.0, 0.0)
    ti = lax.broadcasted_iota(jnp.int32, (tr, tr), 0)
    tj = lax.broadcasted_iota(jnp.int32, (tr, tr), 1)
    before = jnp.where(ti < tj, 1.0, 0.0).astype(bf16)
    excl = jnp.dot(multi.astype(bf16), before, preferred_element_type=f32)
    total = jnp.sum(multi, axis=1, keepdims=True)
    units = jnp.floor((total + (ALIGN - 1)) * (1.0 / ALIGN))
    ei = lax.broadcasted_iota(jnp.int32, (NE, NE), 0)
    ej = lax.broadcasted_iota(jnp.int32, (NE, NE), 1)
    below = jnp.where(ej < ei, 1.0, 0.0).astype(bf16)
    off = ALIGN * jnp.dot(below, jnp.broadcast_to(units, (NE, 128)).astype(bf16),
                          preferred_element_type=f32)[:, 0:1]
    for kk in range(TOPK):
        e_out[kk:kk + 1, :] = idxs[kk]
        w_out[kk:kk + 1, :] = ex[kk] / den
        pos_out[kk:kk + 1, :] = jnp.sum(jnp.where(hits[kk], excl + off, 0.0), axis=0,
                                        keepdims=True).astype(jnp.int32)
    cnt_out[...] = jnp.broadcast_to(total, cnt_out.shape).astype(jnp.int32)


def _route(logits_t):
    t = logits_t.shape[1]
    sel = pl.BlockSpec((TOPK, TT), lambda i: (0, i))
    return pl.pallas_call(
        _route_kernel,
        name="route",
        grid=(t // TT,),
        in_specs=[pl.BlockSpec((NE, TT), lambda i: (0, i))],
        out_specs=[sel, sel, sel, pl.BlockSpec((None, NE, 128), lambda i: (i, 0, 0))],
        out_shape=[jax.ShapeDtypeStruct((TOPK, t), jnp.int32),
                   jax.ShapeDtypeStruct((TOPK, t), f32),
                   jax.ShapeDtypeStruct((TOPK, t), jnp.int32),
                   jax.ShapeDtypeStruct((t // TT, NE, 128), jnp.int32)],
        compiler_params=_cparams(("parallel",)),
    )(logits_t)


def _run_copies(tile, cnt_ref, make_copy, act):
    def per_expert(e, carry):
        cnt = pl.multiple_of(cnt_ref[tile * NE + e], ALIGN)

        @pl.when(cnt > 0)
        def _():
            act(make_copy(e, cnt))
        return carry
    lax.fori_loop(0, NE, per_expert, 0)


def _dispatch_kernel(cnt_ref, run_ref, off_ref, tot_ref, tail_ref, pos_ref, x2_ref, xs_ref,
                     sbuf, zbuf, sem, zsem):
    i = pl.program_id(0)
    last = pl.num_programs(0) - 1
    cur = i % 2

    @pl.when(i == 0)
    def _():
        zbuf[...] = jnp.zeros_like(zbuf)

        def zero_copy(start):
            return pltpu.make_async_copy(zbuf, xs_ref.at[pl.ds(pl.multiple_of(start, EBLK), EBLK)], zsem)

        for e in range(NE):
            @pl.when(tail_ref[e] >= 0)
            def _():
                zero_copy(tail_ref[e]).start()
        for e in range(NE):
            @pl.when(tail_ref[e] >= 0)
            def _():
                zero_copy(tail_ref[e]).wait()
        n_blk = xs_ref.shape[0] // EBLK
        lax.fori_loop(tail_ref[NE], n_blk, lambda blk, c: (zero_copy(blk * EBLK).start(), c)[1], 0)
        lax.fori_loop(tail_ref[NE], n_blk, lambda blk, c: (zero_copy(blk * EBLK).wait(), c)[1], 0)

    def slab(tile, half):
        def make(e, rows):
            src = pl.multiple_of(off_ref[tile * NE + e], ALIGN)
            dst = pl.multiple_of(run_ref[tile * NE + e], ALIGN)
            return pltpu.make_async_copy(sbuf.at[half, pl.ds(src, rows)],
                                         xs_ref.at[pl.ds(dst, rows)], sem.at[half])
        return make

    def drain(tile, half):
        n = pl.multiple_of(tot_ref[tile], ALIGN)
        pltpu.make_async_copy(sbuf.at[half, pl.ds(0, n)], xs_ref.at[pl.ds(0, n)], sem.at[half]).wait()

    @pl.when(i >= 2)
    def _():
        drain(i - 2, cur)

    pos = pos_ref[...]
    xt = x2_ref[...]
    for c in range(0, SB, DCHUNK):
        slot = lax.broadcasted_iota(jnp.int32, (DCHUNK, TT), 0) + c
        place = jnp.zeros((DCHUNK, TT), f32)
        for kk in range(TOPK):
            place = jnp.where(slot == pos[kk:kk + 1, :], 1.0, place)
        sbuf[cur, c:c + DCHUNK] = _pack_pairs(jnp.dot(place.astype(bf16), xt,
                                                      preferred_element_type=f32))
    _run_copies(i, cnt_ref, slab(i, cur), lambda cp: cp.start())

    @pl.when(i == last)
    def _():
        @pl.when(i >= 1)
        def _():
            drain(i - 1, 1 - cur)
        drain(i, cur)


def _dispatch(cnt16, run_start, off16, tot, tail, pos, x2, n_rows):
    t = x2.shape[0]
    return pl.pallas_call(
        _dispatch_kernel,
        name="dispatch",
        grid_spec=pltpu.PrefetchScalarGridSpec(
            num_scalar_prefetch=5, grid=(t // TT,),
            in_specs=[pl.BlockSpec((TOPK, TT), lambda i, *_: (0, i)),
                      pl.BlockSpec((TT, D), lambda i, *_: (i, 0))],
            out_specs=pl.BlockSpec(memory_space=pl.ANY),
            scratch_shapes=[pltpu.VMEM((2, SB, DH), jnp.uint32), pltpu.VMEM((EBLK, DH), jnp.uint32),
                            pltpu.SemaphoreType.DMA((2,)), pltpu.SemaphoreType.DMA(())]),
        out_shape=jax.ShapeDtypeStruct((n_rows, DH), jnp.uint32),
        compiler_params=_cparams(("arbitrary",)),
    )(cnt16, run_start, off16, tot, tail, pos, x2)


def _expert_kernel(be_ref, nv_ref, fx_ref, nx_ref, xs_ref, wg_ref, bg_ref, wu_ref, bu_ref, wd_ref, bd_ref,
                   ys_ref, wf32, wgb, wub, wdb, wsem):
    i = pl.program_id(0)

    def fetch(e):
        return [pltpu.make_async_copy(w.at[e], wf32.at[j], wsem)
                for j, w in enumerate((wg_ref, wu_ref, wd_ref))]

    @pl.when(fx_ref[i] == 1)
    def _():
        @pl.when(i == 0)
        def _():
            for cp in fetch(be_ref[0]):
                cp.start()
        for cp in fetch(be_ref[i]):
            cp.wait()
        wgb[...] = wf32[0].astype(bf16)
        wub[...] = wf32[1].astype(bf16)
        wdb[...] = wf32[2].astype(bf16)

        @pl.when(nx_ref[i] >= 0)
        def _():
            for cp in fetch(nx_ref[i]):
                cp.start()

    def swiglu(n):
        xlo, xhi = _unpack_pairs(xs_ref[0:n, :])
        dot = functools.partial(jnp.dot, preferred_element_type=f32)
        gt = jnp.minimum(dot(xlo, wgb[:DH, :]) + dot(xhi, wgb[DH:, :]) + bg_ref[...], LIMIT)
        up = jnp.clip(dot(xlo, wub[:DH, :]) + dot(xhi, wub[DH:, :]) + bu_ref[...], -LIMIT, LIMIT)
        act = (up + 1.0) * (gt * jax.nn.sigmoid(gt * ALPHA))
        y = dot(act.astype(bf16), wdb[...]) + bd_ref[...]
        ys_ref[0:n, :] = _pack_pairs(y.astype(bf16).astype(f32))

    nv = nv_ref[i]

    @pl.when(nv > ESUB)
    def _():
        swiglu(EBLK)

    @pl.when((nv > 0) & (nv <= ESUB))
    def _():
        swiglu(ESUB)
        ys_ref[ESUB:, :] = jnp.zeros((EBLK - ESUB, DH), jnp.uint32)

    @pl.when(nv == 0)
    def _():
        ys_ref[...] = jnp.zeros_like(ys_ref)


def _experts(blk_e, nvalid, first, nxt_e, xs, wg, bg, wu, bu, wd, bd):
    p = xs.shape[0]
    bspec = pl.BlockSpec((None, 1, D), lambda i, be, nv, fx, nx: (be[i], 0, 0))
    xrows = pl.BlockSpec((EBLK, DH), lambda i, be, nv, fx, nx: (jnp.where(nv[i] > 0, i, 0), 0))
    yrows = pl.BlockSpec((EBLK, DH), lambda i, be, nv, fx, nx: (i, 0))
    hbm = pl.BlockSpec(memory_space=pl.ANY)
    return pl.pallas_call(
        _expert_kernel,
        name="experts",
        grid_spec=pltpu.PrefetchScalarGridSpec(
            num_scalar_prefetch=4, grid=(p // EBLK,),
            in_specs=[xrows, hbm, bspec, hbm, bspec, hbm, bspec],
            out_specs=yrows,
            scratch_shapes=[pltpu.VMEM((3, D, D), f32)] + [pltpu.VMEM((D, D), bf16)] * 3
                           + [pltpu.SemaphoreType.DMA(())]),
        out_shape=jax.ShapeDtypeStruct((p, DH), jnp.uint32),
        compiler_params=_cparams(("arbitrary",)),
    )(blk_e, nvalid, first, nxt_e, xs, wg, bg, wu, bu, wd, bd)


def _combine_kernel(cnt_ref, run_ref, off_ref, tot_ref, ys_ref, pos_ref, w_ref, h_ref, g_ref, o_ref,
                    rbuf, sem):
    i = pl.program_id(0)
    cur = i % 2

    def fetch(tile, half):
        def slab(e, rows):
            src = pl.multiple_of(run_ref[tile * NE + e], ALIGN)
            dst = pl.multiple_of(off_ref[tile * NE + e], ALIGN)
            return pltpu.make_async_copy(ys_ref.at[pl.ds(src, rows)],
                                         rbuf.at[half, pl.ds(dst, rows)], sem.at[half])
        return slab

    @pl.when(i == 0)
    def _():
        rbuf[...] = jnp.zeros_like(rbuf)
        _run_copies(0, cnt_ref, fetch(0, 0), lambda cp: cp.start())

    @pl.when(i + 1 < pl.num_programs(0))
    def _():
        _run_copies(i + 1, cnt_ref, fetch(i + 1, 1 - cur), lambda cp: cp.start())

    n = pl.multiple_of(tot_ref[i], ALIGN)
    pltpu.make_async_copy(ys_ref.at[pl.ds(0, n)], rbuf.at[cur, pl.ds(0, n)], sem.at[cur]).wait()

    pos = pos_ref[...]
    w = w_ref[...]
    moe_lo = jnp.zeros((TT, DH), f32)
    moe_hi = jnp.zeros((TT, DH), f32)
    for c in range(0, SB, SCHUNK):
        slot = lax.broadcasted_iota(jnp.int32, (TT, SCHUNK), 1) + c
        gate = jnp.zeros((TT, SCHUNK), f32)
        for kk in range(TOPK):
            gate = jnp.where(slot == pos[:, kk:kk + 1], w[:, kk:kk + 1], gate)
        gb = gate.astype(bf16)
        lo, hi = _unpack_pairs(rbuf[cur, c:c + SCHUNK])
        moe_lo = moe_lo + jnp.dot(gb, lo, preferred_element_type=f32)
        moe_hi = moe_hi + jnp.dot(gb, hi, preferred_element_type=f32)
    acc = h_ref[...] + jnp.concatenate([moe_lo, moe_hi], axis=1)
    o_ref[...] = acc * lax.rsqrt(jnp.mean(acc * acc, axis=-1, keepdims=True) + NORM_EPS) * g_ref[...]


def _combine(cnt16, run_start, off16, tot, ys, pos_t, gates_t, hres, lnf):
    t = hres.shape[0]
    return pl.pallas_call(
        _combine_kernel,
        name="combine",
        grid_spec=pltpu.PrefetchScalarGridSpec(
            num_scalar_prefetch=4, grid=(t // TT,),
            in_specs=[pl.BlockSpec(memory_space=pl.ANY),
                      pl.BlockSpec((TT, TOPK), lambda i, *_: (i, 0)),
                      pl.BlockSpec((TT, TOPK), lambda i, *_: (i, 0)),
                      pl.BlockSpec((TT, D), lambda i, *_: (i, 0)),
                      pl.BlockSpec((1, D), lambda i, *_: (0, 0))],
            out_specs=pl.BlockSpec((TT, D), lambda i, *_: (i, 0)),
            scratch_shapes=[pltpu.VMEM((2, SB, DH), jnp.uint32), pltpu.SemaphoreType.DMA((2,))]),
        out_shape=jax.ShapeDtypeStruct((t, D), f32),
        compiler_params=_cparams(("arbitrary",)),
    )(cnt16, run_start, off16, tot, ys, pos_t, gates_t, hres, lnf)


def _row_tile(n, want):
    t = min(n, want)
    assert n % t == 0
    return t


def kernel(x, meta_tokens, ln1_g, w_in, mu_r, mu_k, mu_v, mu_w, mu_a, mu_g, w0, w_w1, w_w2, a0, w_a1, w_a2, w_g1, w_g2, k_k, k_a, r_k, lnx_g, lnx_b, w_o_rwkv, conv_w, w_o_conv, w_o, ln2_g, w_router, b_router, w_e_gate, b_e_gate, w_e_up, b_e_up, w_e_down, b_e_down, lnf_g):
    nb, seq, _ = x.shape
    t = nb * seq
    assert ln1_g.shape[0] == 1, "single layer"

    muw, mua, mug = mu_w[0][:, None], mu_a[0][:, None], mu_g[0][:, None]
    lora_cur = jnp.concatenate([(1 - muw) * w_w1[0], (1 - mua) * w_a1[0], (1 - mug) * w_g1[0]], axis=1)
    lora_prev = jnp.concatenate([muw * w_w1[0], mua * w_a1[0], mug * w_g1[0]], axis=1)
    zrow = jnp.zeros((1, R), f32)
    p = {
        "ln1": ln1_g[0][None, :],
        "w_in": w_in[0].astype(bf16),
        "w_lora": jnp.concatenate([lora_cur, lora_prev], axis=1).astype(bf16),
        "vec_mix": jnp.concatenate([mu_r, mu_k, mu_v, w0, a0, k_k, k_a, zrow], axis=0),
        "ww2": w_w2[0].astype(bf16), "wa2": w_a2[0].astype(bf16), "wg2": w_g2[0].astype(bf16),
        "cw": jnp.concatenate([conv_w[0], jnp.zeros((5, R), f32)], axis=0),
        "seg": (jnp.arange(R)[:, None] // HS == jnp.arange(R)[None, :] // HS).astype(bf16),
    }
    vec_merge = jnp.concatenate([lnx_g, lnx_b, r_k, jnp.zeros((5, R), f32)], axis=0)

    meta_out = _projmix(meta_tokens.astype(f32), NMETA, jnp.zeros((MARGIN, PW), f32), p, NMETA)
    pad = lambda z: jnp.pad(z, ((CHUNK - NMETA, 0), (0, 0)))[None]
    r_m, k_m, v_m, kk_m, b_m, lw_m = (pad(z) for z in meta_out[1:7])
    _, s_meta = _wkv(r_m, k_m, v_m, kk_m, b_m, lw_m, jnp.zeros((NH // 2, 2 * HS, 2 * HS), f32), 1)

    x2d = x.reshape(t, D)
    gt, r, k, v, kk, b, lw, g, cb, _ = _projmix(x2d, seq, meta_out[9], p, _row_tile(seq, 512))
    as3 = lambda z: z.reshape(nb, seq, R)
    y, _ = _wkv(as3(r), as3(k), as3(v), as3(kk), as3(b), as3(lw), s_meta[0], WKV_SEQS)
    hres, x2, logits_t = _merge(
        y.reshape(t, R), r, k, v, g, cb, gt, x2d, vec_merge, ln2_g[0][None, :], p["seg"],
        w_o_rwkv[0].astype(bf16), w_o_conv[0].astype(bf16), w_o[0].astype(bf16),
        w_router[0].T, b_router[0][:, None], _row_tile(t, 512))

    assert t % TT == 0
    nt = t // TT
    _, gates, pos, cnt = _route(logits_t)
    cnt = cnt[:, :, 0]
    cnt16 = ((cnt + ALIGN - 1) // ALIGN) * ALIGN
    base16 = jnp.cumsum(cnt16, axis=0) - cnt16
    tot16 = jnp.sum(cnt16, axis=0)
    padded = ((tot16 + EBLK - 1) // EBLK) * EBLK
    pend = jnp.cumsum(padded)
    pstart = pend - padded
    run_start = pstart[None, :] + base16
    off16 = jnp.cumsum(cnt16, axis=1) - cnt16
    n_rows = -(-(t * TOPK + nt * NE * (ALIGN - 1) + NE * EBLK) // EBLK) * EBLK
    n_blk = n_rows // EBLK
    blk_start = jnp.arange(n_blk, dtype=jnp.int32) * EBLK
    blk_e = jnp.minimum(jnp.sum(pend[None, :] <= blk_start[:, None], axis=1), NE - 1).astype(jnp.int32)
    n_used = (pend[NE - 1:] // EBLK).astype(jnp.int32)
    tail = jnp.concatenate([jnp.where(padded > 0, pend - EBLK, -1), n_used]).astype(jnp.int32)
    flat = lambda z: z.reshape(-1).astype(jnp.int32)
    tot = jnp.sum(cnt16, axis=1).astype(jnp.int32)
    xs = _dispatch(flat(cnt16), flat(run_start), flat(off16), tot, tail, pos, x2, n_rows)
    used = blk_start < pend[NE - 1]
    first = jnp.concatenate([jnp.ones((1,), bool), blk_e[1:] != blk_e[:-1]]) & used
    eidx = jnp.arange(NE, dtype=jnp.int32)
    later = (eidx[None, :] > eidx[:, None]) & (padded[None, :] > 0)
    next_of = jnp.where(jnp.any(later, axis=1), jnp.argmax(later, axis=1), -1).astype(jnp.int32)
    mine = blk_e[:, None] == eidx[None, :]
    nxt_e = jnp.sum(jnp.where(mine, next_of[None, :], 0), axis=1).astype(jnp.int32)
    run_end = jnp.sum(jnp.where(mine, (pstart + tot16)[None, :], 0), axis=1)
    nvalid = jnp.where(used, jnp.clip(run_end - blk_start, 0, EBLK), 0).astype(jnp.int32)
    ys = _experts(blk_e, nvalid, first.astype(jnp.int32), nxt_e, xs,
                  w_e_gate[0], b_e_gate[0][:, None, :],
                  w_e_up[0], b_e_up[0][:, None, :],
                  w_e_down[0], b_e_down[0][:, None, :])
    out = _combine(flat(cnt16), flat(run_start), flat(off16), tot, ys, pos.T, gates.T, hres, lnf_g[None, :])
    return out.reshape(nb, seq, D)
```
